```python
import math
import jax, jax.numpy as jnp
from jax import lax
import numpy as np

D_MODEL = 2048
BATCH = 16
SEQ = 256
DEPTH = 2
DEC_BATCH = 2
DEC_SEQ = 1024
PAST_LEN = 512

GRID_W = 64
HEAD_DIM = 128
MLA_HEADS = D_MODEL // (2 * HEAD_DIM)
MLA_Q_LORA = 3 * D_MODEL // 8
MLA_KV_LORA = D_MODEL // 4
MLA_NOPE_DIM = 128
MLA_ROPE_DIM = 64
MLA_V_DIM = 128
GQA_HEADS = D_MODEL // (2 * HEAD_DIM)
GQA_KV_HEADS = 2
DIFF_HEADS = D_MODEL // (2 * HEAD_DIM)
D_FF = ((8 * D_MODEL // 3 + 255) // 256) * 256
CONV_W = 3
ROPE_BASE = 10000.0
EPS = 1e-6
Q_BLOCK = 128
N_EVEN = (DEPTH + 1) // 2
N_ODD = DEPTH // 2
N_IN_AB = MLA_Q_LORA + MLA_KV_LORA + MLA_ROPE_DIM + (GQA_HEADS + 2 * GQA_KV_HEADS) * HEAD_DIM
N_IN_C = DIFF_HEADS * (2 * HEAD_DIM + 2 * HEAD_DIM + 2 * HEAD_DIM)
MLA_SCALE = (MLA_NOPE_DIM + MLA_ROPE_DIM) ** -0.5
HEAD_SCALE = HEAD_DIM ** -0.5

kernel_name = "hybrid_mla_gqa_diffattn_convffn_prefix_dit_step"


def rmsnorm(x, g):
    xf = x.astype(jnp.float32)
    y = xf * lax.rsqrt(jnp.mean(xf * xf, axis=-1, keepdims=True) + EPS)
    return (y * g.astype(jnp.float32)).astype(x.dtype)


def rope_tables(n_tokens, rot_dim):
    n_rows = n_tokens // GRID_W
    row = jnp.repeat(jnp.arange(n_rows), GRID_W).astype(jnp.float32)
    col = jnp.tile(jnp.arange(GRID_W), n_rows).astype(jnp.float32)
    n_freq = rot_dim // 4
    freqs = ROPE_BASE ** (-jnp.arange(n_freq, dtype=jnp.float32) / n_freq)
    ang = jnp.concatenate([row[:, None] * freqs, col[:, None] * freqs], axis=-1)
    return jnp.cos(ang), jnp.sin(ang)


def apply_rope(x, cos, sin):
    half = x.shape[-1] // 2
    x1, x2 = x[..., :half], x[..., half:]
    c = cos[None, :, None, :].astype(x.dtype)
    s = sin[None, :, None, :].astype(x.dtype)
    return jnp.concatenate([x1 * c - x2 * s, x1 * s + x2 * c], axis=-1)


def _to_blocks(q):
    b, t = q.shape[:2]
    return jnp.moveaxis(q.reshape(b, t // Q_BLOCK, Q_BLOCK, *q.shape[2:]), 1, 0)


def _from_blocks(o):
    o = jnp.moveaxis(o, 0, 1)
    return o.reshape(o.shape[0], -1, *o.shape[3:])


def softmax_attention(q, k, v, scale):
    b, t, h, dk = q.shape
    hkv = k.shape[2]
    qg = q.reshape(b, t, hkv, h // hkv, dk)

    def block(qb):
        s = jnp.einsum("bqhgd,bkhd->bhgqk", qb, k).astype(jnp.float32) * scale
        p = jax.nn.softmax(s, axis=-1).astype(v.dtype)
        return jnp.einsum("bhgqk,bkhd->bqhgd", p, v)

    o = _from_blocks(lax.map(block, _to_blocks(qg)))
    return o.reshape(b, t, h, v.shape[-1])


def diff_attention(q, k, v, lam, scale):
    def block(qb):
        s = jnp.einsum("bqhjd,bkhjd->bhjqk", qb, k).astype(jnp.float32) * scale
        p = jax.nn.softmax(s, axis=-1)
        a = (p[:, :, 0] - lam * p[:, :, 1]).astype(v.dtype)
        return jnp.einsum("bhqk,bkhd->bqhd", a, v)

    return _from_blocks(lax.map(block, _to_blocks(q)))


def mixer_ab(h, rope, ctx, w_in, g_q, w_q_up, g_kv, w_kv_up, g_qn, g_kn, w_out):
    b, t, _ = h.shape
    proj = h @ w_in
    i0 = MLA_Q_LORA
    i1 = i0 + MLA_KV_LORA
    i2 = i1 + MLA_ROPE_DIM
    i3 = i2 + GQA_HEADS * HEAD_DIM
    i4 = i3 + GQA_KV_HEADS * HEAD_DIM
    q_lat, ckv, k_rope = proj[..., :i0], proj[..., i0:i1], proj[..., i1:i2]
    qb = rmsnorm(proj[..., i2:i3].reshape(b, t, GQA_HEADS, HEAD_DIM), g_qn)
    kb = rmsnorm(proj[..., i3:i4].reshape(b, t, GQA_KV_HEADS, HEAD_DIM), g_kn)
    vb = proj[..., i4:].reshape(b, t, GQA_KV_HEADS, HEAD_DIM)
    qa = (rmsnorm(q_lat, g_q) @ w_q_up).reshape(b, t, MLA_HEADS, MLA_NOPE_DIM + MLA_ROPE_DIM)
    qa_nope, qa_rope = qa[..., :MLA_NOPE_DIM], qa[..., MLA_NOPE_DIM:]
    ckv = rmsnorm(ckv, g_kv)
    if rope is not None:
        cos_r, sin_r, cos_h, sin_h = rope
        qa_rope = apply_rope(qa_rope, cos_r, sin_r)
        k_rope = apply_rope(k_rope[:, :, None, :], cos_r, sin_r)[:, :, 0]
        qb = apply_rope(qb, cos_h, sin_h)
        kb = apply_rope(kb, cos_h, sin_h)
    new_ctx = (ckv, k_rope, kb, vb)
    if ctx is None:
        ckv_all, krope_all, kb_all, vb_all = new_ctx
    else:
        ckv_all = jnp.concatenate([ctx[0], ckv], axis=1)
        krope_all = jnp.concatenate([ctx[1], k_rope], axis=1)
        kb_all = jnp.concatenate([ctx[2], kb], axis=1)
        vb_all = jnp.concatenate([ctx[3], vb], axis=1)
    s = ckv_all.shape[1]
    kv = (ckv_all @ w_kv_up).reshape(b, s, MLA_HEADS, MLA_NOPE_DIM + MLA_V_DIM)
    ka = jnp.concatenate(
        [kv[..., :MLA_NOPE_DIM], jnp.broadcast_to(krope_all[:, :, None, :], (b, s, MLA_HEADS, MLA_ROPE_DIM))],
        axis=-1)
    va = kv[..., MLA_NOPE_DIM:]
    out_a = softmax_attention(jnp.concatenate([qa_nope, qa_rope], axis=-1), ka, va, MLA_SCALE)
    out_b = softmax_attention(qb, kb_all, vb_all, HEAD_SCALE)
    out = jnp.concatenate([out_a.reshape(b, t, -1), out_b.reshape(b, t, -1)], axis=-1) @ w_out
    return out, new_ctx


def mixer_c(h, rope, ctx, layer_idx, w_in, lq1, lk1, lq2, lk2, g_sub, w_out):
    b, t, _ = h.shape
    proj = h @ w_in
    nq = DIFF_HEADS * 2 * HEAD_DIM
    q = proj[..., :nq].reshape(b, t, DIFF_HEADS * 2, HEAD_DIM)
    k = proj[..., nq:2 * nq].reshape(b, t, DIFF_HEADS * 2, HEAD_DIM)
    v = proj[..., 2 * nq:].reshape(b, t, DIFF_HEADS, 2 * HEAD_DIM)
    if rope is not None:
        _, _, cos_h, sin_h = rope
        q = apply_rope(q, cos_h, sin_h)
        k = apply_rope(k, cos_h, sin_h)
    q = q.reshape(b, t, DIFF_HEADS, 2, HEAD_DIM)
    k = k.reshape(b, t, DIFF_HEADS, 2, HEAD_DIM)
    new_ctx = (k, v)
    if ctx is None:
        k_all, v_all = k, v
    else:
        k_all = jnp.concatenate([ctx[0], k], axis=1)
        v_all = jnp.concatenate([ctx[1], v], axis=1)
    lambda_init = 0.8 - 0.6 * math.exp(-0.3 * layer_idx)
    lam = (jnp.exp(jnp.sum((lq1 * lk1).astype(jnp.float32)))
           - jnp.exp(jnp.sum((lq2 * lk2).astype(jnp.float32))) + lambda_init)
    o = diff_attention(q, k_all, v_all, lam, HEAD_SCALE)
    o = rmsnorm(o, g_sub) * (1.0 - lambda_init)
    return o.reshape(b, t, -1) @ w_out, new_ctx


def conv_ffn(h, w_up, w_conv, b_conv, w_down):
    u = h @ w_up
    up = jnp.pad(u, ((0, 0), (1, 1), (0, 0)))
    u = up[:, :-2] * w_conv[0] + up[:, 1:-1] * w_conv[1] + up[:, 2:] * w_conv[2] + b_conv
    gate, val = u[..., :D_FF], u[..., D_FF:]
    return (jax.nn.silu(gate) * val) @ w_down


def _trunk(x, cond, rope, caches, w_ada, b_ada, g_norm_mix, g_norm_ffn,
           w_in_ab, g_mla_q, w_mla_q_up, g_mla_kv, w_mla_kv_up, g_gqa_q, g_gqa_k, w_out_ab,
           w_in_c, lambda_q1, lambda_k1, lambda_q2, lambda_k2, g_diff_sub, w_out_c,
           w_ffn_up, w_ffn_conv, b_ffn_conv, w_ffn_down, g_final):
    ab_ctx, c_ctx = [], []
    for l in range(DEPTH):
        mod = jax.nn.silu(cond) @ w_ada[l] + b_ada[l]
        sh1, sc1, gt1, sh2, sc2, gt2 = jnp.split(mod[:, None, :], 6, axis=-1)
        h = rmsnorm(x, g_norm_mix[l]) * (1 + sc1) + sh1
        i = l // 2
        if l % 2 == 0:
            ctx = None if caches is None else tuple(cc[:, i] for cc in caches[:4])
            m, new = mixer_ab(h, rope, ctx, w_in_ab[i], g_mla_q[i], w_mla_q_up[i], g_mla_kv[i],
                              w_mla_kv_up[i], g_gqa_q[i], g_gqa_k[i], w_out_ab[i])
            ab_ctx.append(new)
        else:
            ctx = None if caches is None else tuple(cc[:, i] for cc in caches[4:])
            m, new = mixer_c(h, rope, ctx, l, w_in_c[i], lambda_q1[i], lambda_k1[i], lambda_q2[i],
                             lambda_k2[i], g_diff_sub[i], w_out_c[i])
            c_ctx.append(new)
        x = x + gt1 * m
        h = rmsnorm(x, g_norm_ffn[l]) * (1 + sc2) + sh2
        x = x + gt2 * conv_ffn(h, w_ffn_up[l], w_ffn_conv[l], b_ffn_conv[l], w_ffn_down[l])
    return rmsnorm(x, g_final), ab_ctx, c_ctx


def _normal(key, shape, scale):
    return jax.random.normal(key, shape, jnp.float32) * scale


def setup_inputs(seed: int = 0) -> dict:
    key = jax.random.key(seed)
    ks = iter(jax.random.split(key, 40))
    D = D_MODEL
    inp = {}
    inp["x_prompt"] = _normal(next(ks), (BATCH, SEQ, D), 1.0)
    inp["x_sample"] = _normal(next(ks), (DEC_BATCH, DEC_SEQ, D), 1.0)
    inp["cache_mla_ckv"] = _normal(next(ks), (DEC_BATCH, N_EVEN, PAST_LEN, MLA_KV_LORA), 1.0)
    inp["cache_mla_krope"] = _normal(next(ks), (DEC_BATCH, N_EVEN, PAST_LEN, MLA_ROPE_DIM), 1.0)
    inp["cache_gqa_k"] = _normal(next(ks), (DEC_BATCH, N_EVEN, PAST_LEN, GQA_KV_HEADS, HEAD_DIM), 1.0)
    inp["cache_gqa_v"] = _normal(next(ks), (DEC_BATCH, N_EVEN, PAST_LEN, GQA_KV_HEADS, HEAD_DIM), 1.0)
    inp["cache_diff_k"] = _normal(next(ks), (DEC_BATCH, N_ODD, PAST_LEN, DIFF_HEADS, 2, HEAD_DIM), 1.0)
    inp["cache_diff_v"] = _normal(next(ks), (DEC_BATCH, N_ODD, PAST_LEN, DIFF_HEADS, 2 * HEAD_DIM), 1.0)
    inp["c"] = _normal(next(ks), (DEC_BATCH, D), 1.0)
    inp["c_ctx"] = _normal(next(ks), (D,), 1.0)
    inp["w_ada"] = _normal(next(ks), (DEPTH, D, 6 * D), D ** -0.5)
    inp["b_ada"] = _normal(next(ks), (DEPTH, 6 * D), 0.01)
    inp["g_norm_mix"] = 1.0 + _normal(next(ks), (DEPTH, D), 0.02)
    inp["g_norm_ffn"] = 1.0 + _normal(next(ks), (DEPTH, D), 0.02)
    inp["w_in_ab"] = _normal(next(ks), (N_EVEN, D, N_IN_AB), D ** -0.5)
    inp["g_mla_q"] = 1.0 + _normal(next(ks), (N_EVEN, MLA_Q_LORA), 0.02)
    inp["w_mla_q_up"] = _normal(next(ks), (N_EVEN, MLA_Q_LORA, MLA_HEADS * (MLA_NOPE_DIM + MLA_ROPE_DIM)), MLA_Q_LORA ** -0.5)
    inp["g_mla_kv"] = 1.0 + _normal(next(ks), (N_EVEN, MLA_KV_LORA), 0.02)
    inp["w_mla_kv_up"] = _normal(next(ks), (N_EVEN, MLA_KV_LORA, MLA_HEADS * (MLA_NOPE_DIM + MLA_V_DIM)), MLA_KV_LORA ** -0.5)
    inp["g_gqa_q"] = 1.0 + _normal(next(ks), (N_EVEN, HEAD_DIM), 0.02)
    inp["g_gqa_k"] = 1.0 + _normal(next(ks), (N_EVEN, HEAD_DIM), 0.02)
    n_ab_out = MLA_HEADS * MLA_V_DIM + GQA_HEADS * HEAD_DIM
    inp["w_out_ab"] = _normal(next(ks), (N_EVEN, n_ab_out, D), n_ab_out ** -0.5)
    inp["w_in_c"] = _normal(next(ks), (N_ODD, D, N_IN_C), D ** -0.5)
    inp["lambda_q1"] = _normal(next(ks), (N_ODD, HEAD_DIM), 0.1)
    inp["lambda_k1"] = _normal(next(ks), (N_ODD, HEAD_DIM), 0.1)
    inp["lambda_q2"] = _normal(next(ks), (N_ODD, HEAD_DIM), 0.1)
    inp["lambda_k2"] = _normal(next(ks), (N_ODD, HEAD_DIM), 0.1)
    inp["g_diff_sub"] = 1.0 + _normal(next(ks), (N_ODD, 2 * HEAD_DIM), 0.02)
    n_c_out = DIFF_HEADS * 2 * HEAD_DIM
    inp["w_out_c"] = _normal(next(ks), (N_ODD, n_c_out, D), n_c_out ** -0.5)
    inp["w_ffn_up"] = _normal(next(ks), (DEPTH, D, 2 * D_FF), D ** -0.5)
    inp["w_ffn_conv"] = _normal(next(ks), (DEPTH, CONV_W, 2 * D_FF), CONV_W ** -0.5)
    inp["b_ffn_conv"] = _normal(next(ks), (DEPTH, 2 * D_FF), 0.01)
    inp["w_ffn_down"] = _normal(next(ks), (DEPTH, D_FF, D), D_FF ** -0.5)
    inp["g_final"] = 1.0 + _normal(next(ks), (D,), 0.02)
    return inp


def reference(x_prompt, x_sample, cache_mla_ckv, cache_mla_krope, cache_gqa_k, cache_gqa_v,
              cache_diff_k, cache_diff_v, c, c_ctx, w_ada, b_ada, g_norm_mix, g_norm_ffn,
              w_in_ab, g_mla_q, w_mla_q_up, g_mla_kv, w_mla_kv_up, g_gqa_q, g_gqa_k, w_out_ab,
              w_in_c, lambda_q1, lambda_k1, lambda_q2, lambda_k2, g_diff_sub, w_out_c,
              w_ffn_up, w_ffn_conv, b_ffn_conv, w_ffn_down, g_final):
    y_prompt, ab_p, c_p = _trunk(
        x_prompt, c_ctx[None, :], None, None, w_ada, b_ada, g_norm_mix, g_norm_ffn,
        w_in_ab, g_mla_q, w_mla_q_up, g_mla_kv, w_mla_kv_up, g_gqa_q, g_gqa_k, w_out_ab,
        w_in_c, lambda_q1, lambda_k1, lambda_q2, lambda_k2, g_diff_sub, w_out_c,
        w_ffn_up, w_ffn_conv, b_ffn_conv, w_ffn_down, g_final)
    n_lat = x_sample.shape[1]
    rope = (*rope_tables(n_lat, MLA_ROPE_DIM), *rope_tables(n_lat, HEAD_DIM))
    caches = (cache_mla_ckv, cache_mla_krope, cache_gqa_k, cache_gqa_v, cache_diff_k, cache_diff_v)
    y_sample, _, _ = _trunk(
        x_sample, c, rope, caches, w_ada, b_ada, g_norm_mix, g_norm_ffn,
        w_in_ab, g_mla_q, w_mla_q_up, g_mla_kv, w_mla_kv_up, g_gqa_q, g_gqa_k, w_out_ab,
        w_in_c, lambda_q1, lambda_k1, lambda_q2, lambda_k2, g_diff_sub, w_out_c,
        w_ffn_up, w_ffn_conv, b_ffn_conv, w_ffn_down, g_final)
    new_mla_ckv = jnp.stack([t[0] for t in ab_p], axis=1)
    new_mla_krope = jnp.stack([t[1] for t in ab_p], axis=1)
    new_gqa_k = jnp.stack([t[2] for t in ab_p], axis=1)
    new_gqa_v = jnp.stack([t[3] for t in ab_p], axis=1)
    new_diff_k = jnp.stack([t[0] for t in c_p], axis=1)
    new_diff_v = jnp.stack([t[1] for t in c_p], axis=1)
    return (y_prompt, y_sample, new_mla_ckv, new_mla_krope, new_gqa_k, new_gqa_v, new_diff_k, new_diff_v)
```

```python
import functools
import math

import jax
import jax.numpy as jnp
from jax import lax
from jax.experimental import pallas as pl
from jax.experimental.pallas import tpu as pltpu

F32 = jnp.float32
BF16 = jnp.bfloat16

HEAD_DIM = 128
GRID_W = 64
ROPE_BASE = 10000.0
EPS = 1e-6
MLA_NOPE = 128
MLA_ROPE = 64
MLA_V = 128
GQA_KV_HEADS = 2
LANES = 128
VMEM_LIMIT = 56 * 1024 * 1024


def _params(*sem):
    return pltpu.CompilerParams(dimension_semantics=sem, vmem_limit_bytes=VMEM_LIMIT)


def _rms(x, g):
    ms = jnp.mean(x * x, axis=-1, keepdims=True)
    return x * lax.rsqrt(ms + EPS) * g


def _silu(x):
    return x / (1.0 + jnp.exp(-x))


def _rope_half(x, c, s, half):
    if 2 * half == LANES:
        sw = pltpu.roll(x, half, 1)
    else:
        lane = lax.broadcasted_iota(jnp.int32, x.shape, 1)
        first = (lane % (2 * half)) < half
        sw = jnp.where(first, pltpu.roll(x, LANES - half, 1), pltpu.roll(x, half, 1))
    return x * c + sw * s


def _mod_kernel(cond_ref, w_ref, b_ref, o_ref):
    s = _silu(cond_ref[...]).astype(BF16)
    o_ref[...] = jnp.dot(s, w_ref[...].astype(BF16), preferred_element_type=F32) + b_ref[...]


def _modulation(cond, w_ada, b_ada, tn=1024):
    depth, d, n = w_ada.shape
    rows = cond.shape[0]
    return pl.pallas_call(
        _mod_kernel,
        out_shape=jax.ShapeDtypeStruct((depth, rows, n), F32),
        grid=(depth, n // tn),
        in_specs=[
            pl.BlockSpec((rows, d), lambda l, j: (0, 0)),
            pl.BlockSpec((None, d, tn), lambda l, j: (l, 0, j)),
            pl.BlockSpec((None, 1, tn), lambda l, j: (l, 0, j)),
        ],
        out_specs=pl.BlockSpec((None, rows, tn), lambda l, j: (l, 0, j)),
        compiler_params=_params("arbitrary", "arbitrary"),
        name="adaln_mod",
    )(cond, w_ada, b_ada.reshape(depth, 1, n))


def _norm_mod_kernel(x_ref, g_ref, sc_ref, sh_ref, o_ref):
    y = _rms(x_ref[...], g_ref[...])
    o_ref[...] = (y * (1.0 + sc_ref[...]) + sh_ref[...]).astype(o_ref.dtype)


def _norm_kernel(x_ref, g_ref, o_ref):
    o_ref[...] = _rms(x_ref[...], g_ref[...]).astype(o_ref.dtype)


class _Group:
    def __init__(self, batch, seq, mod_row0, per_batch_mod, rope):
        self.batch = batch
        self.seq = seq
        self.m = batch * seq
        self.mod_row0 = mod_row0
        self.per_batch_mod = per_batch_mod
        self.rope = rope

    def mod_row(self, m, tm):
        if self.per_batch_mod:
            return self.mod_row0 + (m * tm) // self.seq
        return self.mod_row0

    def mod_spec(self, layer, chunk, width, tm):
        return pl.BlockSpec((None, None, 1, width),
                            lambda *idx: (layer, self.mod_row(idx[0], tm), 0, chunk))


def _norm_mod(x, g, mod, layer, chunk_sc, chunk_sh, grp, tm=512):
    m, d = x.shape
    return pl.pallas_call(
        _norm_mod_kernel,
        out_shape=jax.ShapeDtypeStruct((m, d), BF16),
        grid=(m // tm,),
        in_specs=[
            pl.BlockSpec((tm, d), lambda i: (i, 0)),
            pl.BlockSpec((None, 1, d), lambda i: (layer, 0, 0)),
            grp.mod_spec(layer, chunk_sc, d, tm),
            grp.mod_spec(layer, chunk_sh, d, tm),
        ],
        out_specs=pl.BlockSpec((tm, d), lambda i: (i, 0)),
        compiler_params=_params("arbitrary"),
        name="norm_mod",
    )(x, g, mod, mod)


def _final_norm(x, g, tm=512):
    m, d = x.shape
    return pl.pallas_call(
        _norm_kernel,
        out_shape=jax.ShapeDtypeStruct((m, d), F32),
        grid=(m // tm,),
        in_specs=[pl.BlockSpec((tm, d), lambda i: (i, 0)),
                  pl.BlockSpec((1, d), lambda i: (0, 0))],
        out_specs=pl.BlockSpec((tm, d), lambda i: (i, 0)),
        compiler_params=_params("arbitrary"),
        name="final_norm",
    )(x, g.reshape(1, d))


def _inproj_ab_kernel(*refs, rope, n_qlat, n_ckv, n_qb, n_kb, n_vb, mla_heads):
    if rope:
        (h_ref, w_ref, wq_ref, wkv_ref, gq_ref, gkv_ref, gqn_ref, gkn_ref,
         c128_ref, s128_ref, c64_ref, s64_ref,
         qmla_ref, kv_ref, ckv_ref, kr_ref, krp_ref, qb_ref, kb_ref, vb_ref) = refs
        c128, s128 = c128_ref[...], s128_ref[...]
        c64, s64 = c64_ref[...], s64_ref[...]
    else:
        (h_ref, w_ref, wq_ref, wkv_ref, gq_ref, gkv_ref, gqn_ref, gkn_ref,
         qmla_ref, kv_ref, ckv_ref, kr_ref, krp_ref, qb_ref, kb_ref, vb_ref) = refs
    h = h_ref[...]

    def seg(a, b):
        return jnp.dot(h, w_ref[:, a:b], preferred_element_type=F32)

    o0 = n_qlat
    o1 = o0 + n_ckv
    o2 = o1 + n_qb
    o3 = o2 + n_kb
    o4 = o3 + n_vb
    qn = _rms(seg(0, o0), gq_ref[...]).astype(BF16)
    qa = jnp.dot(qn, wq_ref[...], preferred_element_type=F32)
    hw = MLA_NOPE + LANES
    for i in range(mla_heads):
        qmla_ref[:, i * hw:i * hw + MLA_NOPE] = qa[:, i * hw:i * hw + MLA_NOPE].astype(BF16)
        rp = qa[:, i * hw + MLA_NOPE:(i + 1) * hw]
        if rope:
            rp = _rope_half(rp, c64, s64, MLA_ROPE // 2)
        qmla_ref[:, i * hw + MLA_NOPE:(i + 1) * hw] = rp.astype(BF16)
    cn = _rms(seg(o0, o1), gkv_ref[...])
    ckv_ref[...] = cn
    kv_ref[...] = jnp.dot(cn.astype(BF16), wkv_ref[...], preferred_element_type=F32).astype(BF16)
    r = seg(o1, o2)
    for i in range(n_qb // HEAD_DIM):
        y = _rms(r[:, i * HEAD_DIM:(i + 1) * HEAD_DIM], gqn_ref[...])
        if rope:
            y = _rope_half(y, c128, s128, HEAD_DIM // 2)
        qb_ref[:, i * HEAD_DIM:(i + 1) * HEAD_DIM] = y.astype(BF16)
    r = seg(o2, o3)
    for i in range(n_kb // HEAD_DIM):
        y = _rms(r[:, i * HEAD_DIM:(i + 1) * HEAD_DIM], gkn_ref[...])
        if rope:
            y = _rope_half(y, c128, s128, HEAD_DIM // 2)
        kb_ref[:, i * HEAD_DIM:(i + 1) * HEAD_DIM] = y
    vb_ref[...] = seg(o3, o4)
    r = seg(o4, o4 + LANES)
    if rope:
        r = _rope_half(r, c64, s64, MLA_ROPE // 2)
    kr_ref[...] = r[:, :MLA_ROPE]
    krp_ref[...] = r.astype(BF16)


def _inproj_ab(h, w_p, wq_p, wkv, g_q, g_kv, g_qn, g_kn, grp, dims, tm=256):
    m, d = h.shape
    n_qlat, n_ckv, n_qb, n_kb, n_vb, mla_heads = dims
    rope = grp.rope is not None
    one = pl.Buffered(1)
    nq_out = mla_heads * (MLA_NOPE + LANES)
    nkv_out = wkv.shape[1]
    in_specs = [
        pl.BlockSpec((tm, d), lambda i: (i, 0)),
        pl.BlockSpec(w_p.shape, lambda i: (0, 0), pipeline_mode=one),
        pl.BlockSpec(wq_p.shape, lambda i: (0, 0), pipeline_mode=one),
        pl.BlockSpec(wkv.shape, lambda i: (0, 0), pipeline_mode=one),
        pl.BlockSpec((1, n_qlat), lambda i: (0, 0)),
        pl.BlockSpec((1, n_ckv), lambda i: (0, 0)),
        pl.BlockSpec((1, HEAD_DIM), lambda i: (0, 0)),
        pl.BlockSpec((1, HEAD_DIM), lambda i: (0, 0)),
    ]
    args = [h, w_p, wq_p, wkv, g_q, g_kv, g_qn, g_kn]
    if rope:
        per = grp.seq // tm
        for name in ("c128", "s128", "c64", "s64"):
            in_specs.append(pl.BlockSpec((tm, LANES), lambda i: (i % per, 0)))
            args.append(grp.rope[name])
    widths = [(nq_out, BF16), (nkv_out, BF16), (n_ckv, F32), (MLA_ROPE, F32), (LANES, BF16),
              (n_qb, BF16), (n_kb, F32), (n_vb, F32)]
    return pl.pallas_call(
        functools.partial(_inproj_ab_kernel, rope=rope, n_qlat=n_qlat, n_ckv=n_ckv, n_qb=n_qb,
                          n_kb=n_kb, n_vb=n_vb, mla_heads=mla_heads),
        out_shape=[jax.ShapeDtypeStruct((m, w), dt) for w, dt in widths],
        grid=(m // tm,),
        in_specs=in_specs,
        out_specs=[pl.BlockSpec((tm, w), lambda i: (i, 0)) for w, _ in widths],
        compiler_params=_params("arbitrary"),
        name="inproj_ab",
    )(*args)


def _kvup_kernel(c_ref, w_ref, o_ref):
    o_ref[...] = jnp.dot(c_ref[...].astype(BF16), w_ref[...],
                         preferred_element_type=F32).astype(o_ref.dtype)


def _kvup(ckv, wkv, tm=512):
    m, k = ckv.shape
    n = wkv.shape[1]
    return pl.pallas_call(
        _kvup_kernel,
        out_shape=jax.ShapeDtypeStruct((m, n), BF16),
        grid=(m // tm,),
        in_specs=[pl.BlockSpec((tm, k), lambda i: (i, 0)),
                  pl.BlockSpec((k, n), lambda i: (0, 0))],
        out_specs=pl.BlockSpec((tm, n), lambda i: (i, 0)),
        compiler_params=_params("arbitrary"),
        name="mla_kv_up",
    )(ckv, wkv)


def _proj_kernel(*refs, rope):
    if rope:
        h_ref, w_ref, c_ref, s_ref, o_ref = refs
    else:
        h_ref, w_ref, o_ref = refs
    r = jnp.dot(h_ref[...], w_ref[...].astype(BF16), preferred_element_type=F32)
    if rope:
        c, s = c_ref[...], s_ref[...]
        for i in range(r.shape[1] // HEAD_DIM):
            sl = slice(i * HEAD_DIM, (i + 1) * HEAD_DIM)
            o_ref[:, sl] = _rope_half(r[:, sl], c, s, HEAD_DIM // 2).astype(o_ref.dtype)
    else:
        o_ref[...] = r.astype(o_ref.dtype)


def _proj(h, w3, col0, ncols, out_dtype, grp, use_rope, tm=1024, tn=512):
    m, d = h.shape
    rope = use_rope and grp.rope is not None
    tm = min(tm, m)
    off = col0 // tn
    in_specs = [pl.BlockSpec((tm, d), lambda i, j: (i, 0)),
                pl.BlockSpec((None, d, tn), lambda i, j: (0, 0, off + j))]
    args = [h, w3]
    if rope:
        per = max(grp.seq // tm, 1)
        tr = min(tm, grp.seq)
        assert tr == tm
        for name in ("c128", "s128"):
            in_specs.append(pl.BlockSpec((tm, LANES), lambda i, j: (i % per, 0)))
            args.append(grp.rope[name])
    return pl.pallas_call(
        functools.partial(_proj_kernel, rope=rope),
        out_shape=jax.ShapeDtypeStruct((m, ncols), out_dtype),
        grid=(m // tm, ncols // tn),
        in_specs=in_specs,
        out_specs=pl.BlockSpec((tm, tn), lambda i, j: (i, j)),
        compiler_params=_params("arbitrary", "arbitrary"),
        name="proj_c",
    )(*args)


def _softmax_parts(scores):
    mx = jnp.max(scores[0], axis=-1, keepdims=True)
    for s in scores[1:]:
        mx = jnp.maximum(mx, jnp.max(s, axis=-1, keepdims=True))
    ps = [jnp.exp(s - mx) for s in scores]
    l = jnp.sum(ps[0], axis=-1, keepdims=True)
    for p in ps[1:]:
        l = l + jnp.sum(p, axis=-1, keepdims=True)
    return ps, l


_NT = (((1,), (1,)), ((), ()))


def _mla_kernel(*refs, n_seg, scale):
    q_ref = refs[0]
    o_ref = refs[-1]
    q = q_ref[...]
    scores, vals = [], []
    for i in range(n_seg):
        kv = refs[1 + 2 * i][...]
        kr = refs[2 + 2 * i][...]
        kcat = jnp.concatenate([kv[:, :MLA_NOPE], kr], axis=1)
        scores.append(lax.dot_general(q, kcat, _NT, preferred_element_type=F32) * scale)
        vals.append(kv[:, MLA_NOPE:])
    ps, l = _softmax_parts(scores)
    o = jnp.dot(ps[0].astype(BF16), vals[0], preferred_element_type=F32)
    for p, v in zip(ps[1:], vals[1:]):
        o = o + jnp.dot(p.astype(BF16), v, preferred_element_type=F32)
    o_ref[...] = (o / l).astype(o_ref.dtype)


def _mla_attention(q, segs, grp, heads, tq):
    m = q.shape[0]
    nq = grp.seq // tq
    hw = MLA_NOPE + LANES
    in_specs = [pl.BlockSpec((tq, hw), lambda b, h, i: (b * nq + i, h))]
    args = [q]
    for kv, kr, rows in segs:
        in_specs.append(pl.BlockSpec((rows, MLA_NOPE + MLA_V), lambda b, h, i: (b, h)))
        in_specs.append(pl.BlockSpec((rows, LANES), lambda b, h, i: (b, 0)))
        args += [kv, kr]
    return pl.pallas_call(
        functools.partial(_mla_kernel, n_seg=len(segs), scale=(MLA_NOPE + MLA_ROPE) ** -0.5),
        out_shape=jax.ShapeDtypeStruct((m, heads * MLA_V), BF16),
        grid=(grp.batch, heads, nq),
        in_specs=in_specs,
        out_specs=pl.BlockSpec((tq, MLA_V), lambda b, h, i: (b * nq + i, h)),
        compiler_params=_params("arbitrary", "arbitrary", "arbitrary"),
        name="attn_mla",
    )(*args)


def _gqa_kernel(*refs, n_seg, group, scale):
    q_ref = refs[0]
    o_ref = refs[-1]
    ks = [refs[1 + 2 * i][...].astype(BF16) for i in range(n_seg)]
    vs = [refs[2 + 2 * i][...].astype(BF16) for i in range(n_seg)]
    for g in range(group):
        sl = slice(g * HEAD_DIM, (g + 1) * HEAD_DIM)
        q = q_ref[:, sl]
        scores = [lax.dot_general(q, k, _NT, preferred_element_type=F32) * scale for k in ks]
        ps, l = _softmax_parts(scores)
        o = jnp.dot(ps[0].astype(BF16), vs[0], preferred_element_type=F32)
        for p, v in zip(ps[1:], vs[1:]):
            o = o + jnp.dot(p.astype(BF16), v, preferred_element_type=F32)
        o_ref[:, sl] = (o / l).astype(o_ref.dtype)


def _gqa_attention(q, segs, grp, heads, kv_heads, tq):
    m = q.shape[0]
    nq = grp.seq // tq
    group = heads // kv_heads
    gw = group * HEAD_DIM
    in_specs = [pl.BlockSpec((tq, gw), lambda b, h, i: (b * nq + i, h))]
    args = [q]
    for k, v, rows in segs:
        in_specs.append(pl.BlockSpec((rows, HEAD_DIM), lambda b, h, i: (b, h)))
        in_specs.append(pl.BlockSpec((rows, HEAD_DIM), lambda b, h, i: (b, h)))
        args += [k, v]
    return pl.pallas_call(
        functools.partial(_gqa_kernel, n_seg=len(segs), group=group, scale=HEAD_DIM ** -0.5),
        out_shape=jax.ShapeDtypeStruct((m, heads * HEAD_DIM), BF16),
        grid=(grp.batch, kv_heads, nq),
        in_specs=in_specs,
        out_specs=pl.BlockSpec((tq, gw), lambda b, h, i: (b * nq + i, h)),
        compiler_params=_params("arbitrary", "arbitrary", "arbitrary"),
        name="attn_gqa",
    )(*args)


def _diff_kernel(*refs, n_seg, scale, lambda_init):
    q_ref, lq1_ref, lk1_ref, lq2_ref, lk2_ref, g_ref = refs[:6]
    o_ref = refs[-1]
    lam = (jnp.exp(jnp.sum(lq1_ref[...] * lk1_ref[...], axis=-1, keepdims=True))
           - jnp.exp(jnp.sum(lq2_ref[...] * lk2_ref[...], axis=-1, keepdims=True)) + lambda_init)
    ks = [refs[6 + 2 * i][...].astype(BF16) for i in range(n_seg)]
    vs = [refs[7 + 2 * i][...].astype(BF16) for i in range(n_seg)]
    probs = []
    for j in range(2):
        sl = slice(j * HEAD_DIM, (j + 1) * HEAD_DIM)
        q = q_ref[:, sl]
        scores = [lax.dot_general(q, k[:, sl], _NT, preferred_element_type=F32) * scale for k in ks]
        ps, l = _softmax_parts(scores)
        inv = 1.0 / l
        probs.append([p * inv for p in ps])
    o = None
    for i in range(n_seg):
        a = (probs[0][i] - lam * probs[1][i]).astype(BF16)
        part = jnp.dot(a, vs[i], preferred_element_type=F32)
        o = part if o is None else o + part
    o_ref[...] = (_rms(o, g_ref[...]) * (1.0 - lambda_init)).astype(o_ref.dtype)


def _diff_attention(q, segs, lams, g_sub, grp, heads, tq, lambda_init):
    m = q.shape[0]
    nq = grp.seq // tq
    hw = 2 * HEAD_DIM
    vec = lambda w: pl.BlockSpec((1, w), lambda b, h, i: (0, 0))
    in_specs = [pl.BlockSpec((tq, hw), lambda b, h, i: (b * nq + i, h))]
    in_specs += [vec(HEAD_DIM)] * 4 + [vec(hw)]
    args = [q, *lams, g_sub]
    for k, v, rows in segs:
        in_specs.append(pl.BlockSpec((rows, hw), lambda b, h, i: (b, h)))
        in_specs.append(pl.BlockSpec((rows, hw), lambda b, h, i: (b, h)))
        args += [k, v]
    return pl.pallas_call(
        functools.partial(_diff_kernel, n_seg=len(segs), scale=HEAD_DIM ** -0.5,
                          lambda_init=lambda_init),
        out_shape=jax.ShapeDtypeStruct((m, heads * hw), BF16),
        grid=(grp.batch, heads, nq),
        in_specs=in_specs,
        out_specs=pl.BlockSpec((tq, hw), lambda b, h, i: (b * nq + i, h)),
        compiler_params=_params("arbitrary", "arbitrary", "arbitrary"),
        name="attn_diff",
    )(*args)


def _mm_res_kernel(*refs, n_a, nk):
    a_refs = refs[:n_a]
    w_refs = refs[n_a:2 * n_a]
    x_ref, gt_ref, o_ref = refs[2 * n_a:2 * n_a + 3]
    part = None
    for a_ref, w_ref in zip(a_refs, w_refs):
        p = jnp.dot(a_ref[...], w_ref[...].astype(BF16), preferred_element_type=F32)
        part = p if part is None else part + p
    if nk == 1:
        o_ref[...] = x_ref[...] + gt_ref[...] * part
        return
    acc_ref = refs[-1]
    k = pl.program_id(2)

    @pl.when(k == 0)
    def _():
        acc_ref[...] = part

    @pl.when(k > 0)
    def _():
        acc_ref[...] += part

    @pl.when(k == nk - 1)
    def _():
        o_ref[...] = x_ref[...] + gt_ref[...] * acc_ref[...]


def _mm_res(a_list, w3, w_layer, x, mod, layer, chunk, grp, tk, tm=1024, tn=1024):
    m, n = x.shape
    n_a = len(a_list)
    ka = a_list[0].shape[1]
    if n_a > 1:
        tk = ka
    nk = ka // tk
    ncol = n // tn
    in_specs = [pl.BlockSpec((tm, tk), lambda i, j, k: (i, k)) for _ in a_list]
    for idx in range(n_a):
        in_specs.append(pl.BlockSpec((None, tk, tn),
                                     lambda i, j, k, idx=idx: (w_layer, idx * (ka // tk) + k, j)))
    in_specs.append(pl.BlockSpec((tm, tn), lambda i, j, k: (i, j)))
    in_specs.append(pl.BlockSpec(
        (None, None, 1, tn),
        lambda i, j, k: (layer, grp.mod_row(i, tm), 0, chunk * ncol + j)))
    scratch = [pltpu.VMEM((tm, tn), F32)] if nk > 1 else []
    return pl.pallas_call(
        functools.partial(_mm_res_kernel, n_a=n_a, nk=nk),
        out_shape=jax.ShapeDtypeStruct((m, n), F32),
        grid=(m // tm, ncol, nk),
        in_specs=in_specs,
        out_specs=pl.BlockSpec((tm, tn), lambda i, j, k: (i, j)),
        scratch_shapes=scratch,
        compiler_params=_params("arbitrary", "arbitrary", "arbitrary"),
        name="mm_res",
    )(*a_list, *([w3] * n_a), x, mod)


def _ffn_up_kernel(h_ref, wg_ref, wv_ref, cg_ref, cv_ref, bg_ref, bv_ref, o_ref, *, seq):
    h = h_ref[...]
    tm = h.shape[0]
    row = lax.broadcasted_iota(jnp.int32, (tm, 1), 0) % seq
    first = row == 0
    last = row == seq - 1

    def conv(w_ref, c_ref, b_ref):
        u = jnp.dot(h, w_ref[...].astype(BF16), preferred_element_type=F32)
        prev = jnp.where(first, 0.0, pltpu.roll(u, 1, 0))
        nxt = jnp.where(last, 0.0, pltpu.roll(u, tm - 1, 0))
        return prev * c_ref[0:1, :] + u * c_ref[1:2, :] + nxt * c_ref[2:3, :] + b_ref[...]

    gate = conv(wg_ref, cg_ref, bg_ref)
    val = conv(wv_ref, cv_ref, bv_ref)
    o_ref[...] = (_silu(gate) * val).astype(o_ref.dtype)


def _ffn_up(h, w_up, w_conv, b_conv, layer, grp, tm=1024, tn=256):
    m, d = h.shape
    d_ff = w_up.shape[2] // 2
    nj = d_ff // tn
    assert tm % grp.seq == 0 or grp.seq % tm == 0
    assert grp.seq >= tm or tm % grp.seq == 0
    b3 = b_conv.reshape(b_conv.shape[0], 1, b_conv.shape[1])
    return pl.pallas_call(
        functools.partial(_ffn_up_kernel, seq=grp.seq),
        out_shape=jax.ShapeDtypeStruct((m, d_ff), BF16),
        grid=(m // tm, nj),
        in_specs=[
            pl.BlockSpec((tm, d), lambda i, j: (i, 0)),
            pl.BlockSpec((None, d, tn), lambda i, j: (layer, 0, j)),
            pl.BlockSpec((None, d, tn), lambda i, j: (layer, 0, nj + j)),
            pl.BlockSpec((None, w_conv.shape[1], tn), lambda i, j: (layer, 0, j)),
            pl.BlockSpec((None, w_conv.shape[1], tn), lambda i, j: (layer, 0, nj + j)),
            pl.BlockSpec((None, 1, tn), lambda i, j: (layer, 0, j)),
            pl.BlockSpec((None, 1, tn), lambda i, j: (layer, 0, nj + j)),
        ],
        out_specs=pl.BlockSpec((tm, tn), lambda i, j: (i, j)),
        compiler_params=_params("arbitrary", "arbitrary"),
        name="ffn_up",
    )(h, w_up, w_up, w_conv, w_conv, b3, b3)


def _rope_tables(n_tokens, rot_dim):
    n_rows = n_tokens // GRID_W
    row = jnp.repeat(jnp.arange(n_rows), GRID_W).astype(F32)
    col = jnp.tile(jnp.arange(GRID_W), n_rows).astype(F32)
    n_freq = rot_dim // 4
    freqs = ROPE_BASE ** (-jnp.arange(n_freq, dtype=F32) / n_freq)
    ang = jnp.concatenate([row[:, None] * freqs, col[:, None] * freqs], axis=-1)
    cos, sin = jnp.cos(ang), jnp.sin(ang)
    reps = LANES // rot_dim
    c = jnp.tile(jnp.concatenate([cos, cos], axis=-1), (1, reps))
    s = jnp.tile(jnp.concatenate([-sin, sin], axis=-1), (1, reps))
    return c, s


def _trunk(x, grp, mod, caches, P):
    d = x.shape[1]
    ckv_p = krope_p = kb_p = vb_p = k_c = v_c = None
    depth = P["w_ffn_up"].shape[0]
    for l in range(depth):
        i = l // 2
        h = _norm_mod(x, P["g_norm_mix"], mod, l, 1, 0, grp)
        if l % 2 == 0:
            (q_mla, kv, ckv, krope, krope_pad, qb, kb, vb) = _inproj_ab(
                h, P["w_in_ab_p"][i], P["w_q_up_p"][i], P["w_kv_up_b"][i],
                P["g_mla_q"][i:i + 1], P["g_mla_kv"][i:i + 1], P["g_gqa_q"][i:i + 1],
                P["g_gqa_k"][i:i + 1], grp, P["dims_ab"])
            ckv_p, krope_p, kb_p, vb_p = ckv, krope, kb, vb
            heads = P["dims_ab"][5]
            segs_a, segs_b = [], []
            if caches is not None:
                c_ckv, c_kr, c_k, c_v = (cc[i] for cc in caches[:4])
                past = c_ckv.shape[0] // grp.batch
                segs_a.append((_kvup(c_ckv, P["w_kv_up_b"][i]), c_kr, past))
                segs_b.append((c_k, c_v, past))
            segs_a.append((kv, krope_pad, grp.seq))
            segs_b.append((kb, vb, grp.seq))
            tq = min(grp.seq, 512)
            out_a = _mla_attention(q_mla, segs_a, grp, heads, tq)
            out_b = _gqa_attention(qb, segs_b, grp, qb.shape[1] // HEAD_DIM, GQA_KV_HEADS, tq)
            x = _mm_res([out_a, out_b], P["w_out_ab"], i, x, mod, l, 2, grp, tk=None)
        else:
            w_in_c = P["w_in_c"][i:i + 1]
            nq = w_in_c.shape[2] // 3
            q = _proj(h, w_in_c, 0, nq, BF16, grp, True)
            k = _proj(h, w_in_c, nq, nq, F32, grp, True)
            v = _proj(h, w_in_c, 2 * nq, nq, F32, grp, False)
            k_c, v_c = k, v
            segs = []
            if caches is not None:
                c_k, c_v = (cc[i] for cc in caches[4:])
                segs.append((c_k, c_v, c_k.shape[0] // grp.batch))
            segs.append((k, v, grp.seq))
            lambda_init = 0.8 - 0.6 * math.exp(-0.3 * l)
            lams = [P[n][i:i + 1] for n in ("lambda_q1", "lambda_k1", "lambda_q2", "lambda_k2")]
            o = _diff_attention(q, segs, lams, P["g_diff_sub"][i:i + 1], grp,
                                nq // (2 * HEAD_DIM), min(grp.seq, 512), lambda_init)
            x = _mm_res([o], P["w_out_c"], i, x, mod, l, 2, grp, tk=1024)
        h = _norm_mod(x, P["g_norm_ffn"], mod, l, 4, 3, grp)
        g = _ffn_up(h, P["w_ffn_up"], P["w_ffn_conv"], P["b_ffn_conv"], l, grp)
        x = _mm_res([g], P["w_ffn_down"], l, x, mod, l, 5, grp, tk=512)
    y = _final_norm(x, P["g_final"])
    return y, (ckv_p, krope_p, kb_p, vb_p), (k_c, v_c)


def kernel(x_prompt, x_sample, cache_mla_ckv, cache_mla_krope, cache_gqa_k, cache_gqa_v,
           cache_diff_k, cache_diff_v, c, c_ctx, w_ada, b_ada, g_norm_mix, g_norm_ffn,
           w_in_ab, g_mla_q, w_mla_q_up, g_mla_kv, w_mla_kv_up, g_gqa_q, g_gqa_k, w_out_ab,
           w_in_c, lambda_q1, lambda_k1, lambda_q2, lambda_k2, g_diff_sub, w_out_c,
           w_ffn_up, w_ffn_conv, b_ffn_conv, w_ffn_down, g_final):
    bp, tp, d = x_prompt.shape
    bs, ts, _ = x_sample.shape
    depth = w_ada.shape[0]
    n_even = w_in_ab.shape[0]
    assert depth == 2 and n_even == 1 and w_in_c.shape[0] == 1, "one MLA/GQA layer + one diff layer"
    past = cache_mla_ckv.shape[2]
    n_qlat = g_mla_q.shape[1]
    n_ckv = g_mla_kv.shape[1]
    mla_heads = w_mla_q_up.shape[2] // (MLA_NOPE + MLA_ROPE)
    n_kb = cache_gqa_k.shape[3] * HEAD_DIM
    n_vb = n_kb
    n_qb = w_in_ab.shape[2] - n_qlat - n_ckv - MLA_ROPE - n_kb - n_vb

    rows = 8
    cond = jnp.zeros((rows, d), F32).at[0].set(c_ctx).at[1:1 + bs].set(c)
    mod = _modulation(cond, w_ada, b_ada)
    mod = mod.reshape(depth, rows, 1, mod.shape[2])

    o0 = n_qlat + n_ckv
    o1 = o0 + MLA_ROPE
    w_in_ab_p = jnp.concatenate(
        [w_in_ab[:, :, :o0], w_in_ab[:, :, o1:], w_in_ab[:, :, o0:o1],
         jnp.zeros((n_even, d, LANES - MLA_ROPE), F32)], axis=2).astype(BF16)
    wq = w_mla_q_up.reshape(n_even, n_qlat, mla_heads, MLA_NOPE + MLA_ROPE)
    w_q_up_p = jnp.pad(wq, ((0, 0), (0, 0), (0, 0), (0, LANES - MLA_ROPE))).reshape(
        n_even, n_qlat, mla_heads * (MLA_NOPE + LANES)).astype(BF16)
    P = dict(
        w_in_ab_p=w_in_ab_p, w_q_up_p=w_q_up_p, w_kv_up_b=w_mla_kv_up.astype(BF16),
        g_mla_q=g_mla_q, g_mla_kv=g_mla_kv, g_gqa_q=g_gqa_q, g_gqa_k=g_gqa_k,
        w_out_ab=w_out_ab, w_in_c=w_in_c, lambda_q1=lambda_q1, lambda_k1=lambda_k1,
        lambda_q2=lambda_q2, lambda_k2=lambda_k2, g_diff_sub=g_diff_sub, w_out_c=w_out_c,
        w_ffn_up=w_ffn_up, w_ffn_conv=w_ffn_conv, b_ffn_conv=b_ffn_conv, w_ffn_down=w_ffn_down,
        g_final=g_final,
        g_norm_mix=g_norm_mix.reshape(depth, 1, d), g_norm_ffn=g_norm_ffn.reshape(depth, 1, d),
        dims_ab=(n_qlat, n_ckv, n_qb, n_kb, n_vb, mla_heads),
    )

    grp_p = _Group(bp, tp, 0, False, None)
    y_p, ab_p, c_p = _trunk(x_prompt.reshape(bp * tp, d), grp_p, mod, None, P)

    c128, s128 = _rope_tables(ts, HEAD_DIM)
    c64, s64 = _rope_tables(ts, MLA_ROPE)
    grp_s = _Group(bs, ts, 1, True, dict(c128=c128, s128=s128, c64=c64, s64=s64))
    n_odd = cache_diff_k.shape[1]
    kr_pad = jnp.pad(cache_mla_krope, ((0, 0), (0, 0), (0, 0), (0, LANES - MLA_ROPE))).astype(BF16)
    per_layer = lambda a, n: [a[:, j].reshape(bs * past, -1) for j in range(n)]
    caches = (per_layer(cache_mla_ckv, n_even), per_layer(kr_pad, n_even),
              per_layer(cache_gqa_k, n_even), per_layer(cache_gqa_v, n_even),
              per_layer(cache_diff_k, n_odd), per_layer(cache_diff_v, n_odd))
    y_s, _, _ = _trunk(x_sample.reshape(bs * ts, d), grp_s, mod, caches, P)

    ckv, krope, kb, vb = ab_p
    k_c, v_c = c_p
    kvh = n_kb // HEAD_DIM
    dh = k_c.shape[1] // (2 * HEAD_DIM)
    return (y_p.reshape(bp, tp, d), y_s.reshape(bs, ts, d),
            ckv.reshape(bp, 1, tp, n_ckv), krope.reshape(bp, 1, tp, MLA_ROPE),
            kb.reshape(bp, 1, tp, kvh, HEAD_DIM), vb.reshape(bp, 1, tp, kvh, HEAD_DIM),
            k_c.reshape(bp, 1, tp, dh, 2, HEAD_DIM), v_c.reshape(bp, 1, tp, dh, 2 * HEAD_DIM))
```

```python
import functools
import math

import jax
import jax.numpy as jnp
from jax import lax
from jax.experimental import pallas as pl
from jax.experimental.pallas import tpu as pltpu

F32 = jnp.float32
BF16 = jnp.bfloat16

HEAD_DIM = 128
GRID_W = 64
ROPE_BASE = 10000.0
EPS = 1e-6
MLA_NOPE = 128
MLA_ROPE = 64
MLA_V = 128
GQA_KV_HEADS = 2
LOG2E = 1.4426950408889634
LANES = 128
VMEM_LIMIT = 56 * 1024 * 1024


def _params(*sem):
    return pltpu.CompilerParams(dimension_semantics=sem, vmem_limit_bytes=VMEM_LIMIT)


def _rms(x, g):
    ms = jnp.mean(x * x, axis=-1, keepdims=True)
    return x * lax.rsqrt(ms + EPS) * g


def _silu(x):
    return x / (1.0 + jnp.exp(-x))


def _rope_half(x, c, s, half):
    if 2 * half == LANES:
        sw = pltpu.roll(x, half, 1)
    else:
        lane = lax.broadcasted_iota(jnp.int32, x.shape, 1)
        first = (lane % (2 * half)) < half
        sw = jnp.where(first, pltpu.roll(x, LANES - half, 1), pltpu.roll(x, half, 1))
    return x * c + sw * s


def _mod_kernel(cond_ref, w_ref, b_ref, o_ref):
    s = _silu(cond_ref[...]).astype(BF16)
    o_ref[...] = jnp.dot(s, w_ref[...].astype(BF16), preferred_element_type=F32) + b_ref[...]


def _modulation(cond, w_ada, b_ada, tn=1024):
    depth, d, n = w_ada.shape
    rows = cond.shape[0]
    return pl.pallas_call(
        _mod_kernel,
        out_shape=jax.ShapeDtypeStruct((depth, rows, n), F32),
        grid=(depth, n // tn),
        in_specs=[
            pl.BlockSpec((rows, d), lambda l, j: (0, 0)),
            pl.BlockSpec((None, d, tn), lambda l, j: (l, 0, j)),
            pl.BlockSpec((None, 1, tn), lambda l, j: (l, 0, j)),
        ],
        out_specs=pl.BlockSpec((None, rows, tn), lambda l, j: (l, 0, j)),
        compiler_params=_params("arbitrary", "arbitrary"),
        name="adaln_mod",
    )(cond, w_ada, b_ada.reshape(depth, 1, n))


def _norm_mod_kernel(x_ref, g_ref, sc_ref, sh_ref, o_ref):
    y = _rms(x_ref[...], g_ref[...])
    o_ref[...] = (y * (1.0 + sc_ref[...]) + sh_ref[...]).astype(o_ref.dtype)


def _norm_kernel(x_ref, g_ref, o_ref):
    o_ref[...] = _rms(x_ref[...], g_ref[...]).astype(o_ref.dtype)


class _Group:
    def __init__(self, batch, seq, mod_row0, per_batch_mod, rope):
        self.batch = batch
        self.seq = seq
        self.m = batch * seq
        self.mod_row0 = mod_row0
        self.per_batch_mod = per_batch_mod
        self.rope = rope

    def mod_tile(self, tm):
        return min(tm, self.seq) if self.per_batch_mod else tm

    def mod_row(self, m, tm):
        if self.per_batch_mod:
            assert self.seq % tm == 0
            return self.mod_row0 + (m * tm) // self.seq
        return self.mod_row0

    def mod_spec(self, layer, chunk, width, tm):
        return pl.BlockSpec((None, None, 1, width),
                            lambda *idx: (layer, self.mod_row(idx[0], tm), 0, chunk))


def _norm_mod(x, g, mod, layer, chunk_sc, chunk_sh, grp, tm=512):
    m, d = x.shape
    return pl.pallas_call(
        _norm_mod_kernel,
        out_shape=jax.ShapeDtypeStruct((m, d), BF16),
        grid=(m // tm,),
        in_specs=[
            pl.BlockSpec((tm, d), lambda i: (i, 0)),
            pl.BlockSpec((None, 1, d), lambda i: (layer, 0, 0)),
            grp.mod_spec(layer, chunk_sc, d, tm),
            grp.mod_spec(layer, chunk_sh, d, tm),
        ],
        out_specs=pl.BlockSpec((tm, d), lambda i: (i, 0)),
        compiler_params=_params("arbitrary"),
        name="norm_mod",
    )(x, g, mod, mod)


def _final_norm(x, g, tm=512):
    m, d = x.shape
    return pl.pallas_call(
        _norm_kernel,
        out_shape=jax.ShapeDtypeStruct((m, d), F32),
        grid=(m // tm,),
        in_specs=[pl.BlockSpec((tm, d), lambda i: (i, 0)),
                  pl.BlockSpec((1, d), lambda i: (0, 0))],
        out_specs=pl.BlockSpec((tm, d), lambda i: (i, 0)),
        compiler_params=_params("arbitrary"),
        name="final_norm",
    )(x, g.reshape(1, d))


def _inproj_ab_kernel(*refs, rope, n_qlat, n_ckv, n_qb, n_kb, n_vb, mla_heads):
    if rope:
        (h_ref, w_ref, wq_ref, wkv_ref, gq_ref, gkv_ref, gqn_ref, gkn_ref,
         c128_ref, s128_ref, c64_ref, s64_ref,
         qmla_ref, kv_ref, ckv_ref, kr_ref, krp_ref, qb_ref, kb_ref, vb_ref) = refs
        c128, s128 = c128_ref[...], s128_ref[...]
        c64, s64 = c64_ref[...], s64_ref[...]
    else:
        (h_ref, w_ref, wq_ref, wkv_ref, gq_ref, gkv_ref, gqn_ref, gkn_ref,
         qmla_ref, kv_ref, ckv_ref, kr_ref, krp_ref, qb_ref, kb_ref, vb_ref) = refs
    h = h_ref[...]

    def seg(a, b):
        return jnp.dot(h, w_ref[:, a:b], preferred_element_type=F32)

    o0 = n_qlat
    o1 = o0 + n_ckv
    o2 = o1 + n_qb
    o3 = o2 + n_kb
    o4 = o3 + n_vb
    qn = _rms(seg(0, o0), gq_ref[...]).astype(BF16)
    qa = jnp.dot(qn, wq_ref[...], preferred_element_type=F32)
    hw = MLA_NOPE + LANES
    for i in range(mla_heads):
        qmla_ref[:, i * hw:i * hw + MLA_NOPE] = qa[:, i * hw:i * hw + MLA_NOPE].astype(BF16)
        rp = qa[:, i * hw + MLA_NOPE:(i + 1) * hw]
        if rope:
            rp = _rope_half(rp, c64, s64, MLA_ROPE // 2)
        qmla_ref[:, i * hw + MLA_NOPE:(i + 1) * hw] = rp.astype(BF16)
    cn = _rms(seg(o0, o1), gkv_ref[...])
    ckv_ref[...] = cn
    kv_ref[...] = jnp.dot(cn.astype(BF16), wkv_ref[...], preferred_element_type=F32).astype(BF16)
    r = seg(o1, o2)
    for i in range(n_qb // HEAD_DIM):
        y = _rms(r[:, i * HEAD_DIM:(i + 1) * HEAD_DIM], gqn_ref[...])
        if rope:
            y = _rope_half(y, c128, s128, HEAD_DIM // 2)
        qb_ref[:, i * HEAD_DIM:(i + 1) * HEAD_DIM] = y.astype(BF16)
    r = seg(o2, o3)
    for i in range(n_kb // HEAD_DIM):
        y = _rms(r[:, i * HEAD_DIM:(i + 1) * HEAD_DIM], gkn_ref[...])
        if rope:
            y = _rope_half(y, c128, s128, HEAD_DIM // 2)
        kb_ref[:, i * HEAD_DIM:(i + 1) * HEAD_DIM] = y
    vb_ref[...] = seg(o3, o4)
    r = seg(o4, o4 + LANES)
    if rope:
        r = _rope_half(r, c64, s64, MLA_ROPE // 2)
    kr_ref[...] = r[:, :MLA_ROPE]
    krp_ref[...] = r.astype(BF16)


def _inproj_ab(h, w_p, wq_p, wkv, g_q, g_kv, g_qn, g_kn, grp, dims, tm=256):
    m, d = h.shape
    n_qlat, n_ckv, n_qb, n_kb, n_vb, mla_heads = dims
    rope = grp.rope is not None
    one = pl.Buffered(1)
    nq_out = mla_heads * (MLA_NOPE + LANES)
    nkv_out = wkv.shape[1]
    in_specs = [
        pl.BlockSpec((tm, d), lambda i: (i, 0)),
        pl.BlockSpec(w_p.shape, lambda i: (0, 0), pipeline_mode=one),
        pl.BlockSpec(wq_p.shape, lambda i: (0, 0), pipeline_mode=one),
        pl.BlockSpec(wkv.shape, lambda i: (0, 0), pipeline_mode=one),
        pl.BlockSpec((1, n_qlat), lambda i: (0, 0)),
        pl.BlockSpec((1, n_ckv), lambda i: (0, 0)),
        pl.BlockSpec((1, HEAD_DIM), lambda i: (0, 0)),
        pl.BlockSpec((1, HEAD_DIM), lambda i: (0, 0)),
    ]
    args = [h, w_p, wq_p, wkv, g_q, g_kv, g_qn, g_kn]
    if rope:
        per = grp.seq // tm
        for name in ("c128", "s128", "c64", "s64"):
            in_specs.append(pl.BlockSpec((tm, LANES), lambda i: (i % per, 0)))
            args.append(grp.rope[name])
    widths = [(nq_out, BF16), (nkv_out, BF16), (n_ckv, F32), (MLA_ROPE, F32), (LANES, BF16),
              (n_qb, BF16), (n_kb, F32), (n_vb, F32)]
    return pl.pallas_call(
        functools.partial(_inproj_ab_kernel, rope=rope, n_qlat=n_qlat, n_ckv=n_ckv, n_qb=n_qb,
                          n_kb=n_kb, n_vb=n_vb, mla_heads=mla_heads),
        out_shape=[jax.ShapeDtypeStruct((m, w), dt) for w, dt in widths],
        grid=(m // tm,),
        in_specs=in_specs,
        out_specs=[pl.BlockSpec((tm, w), lambda i: (i, 0)) for w, _ in widths],
        compiler_params=_params("arbitrary"),
        name="inproj_ab",
    )(*args)


def _kvup_kernel(c_ref, w_ref, o_ref):
    o_ref[...] = jnp.dot(c_ref[...].astype(BF16), w_ref[...],
                         preferred_element_type=F32).astype(o_ref.dtype)


def _kvup(ckv, wkv, tm=512):
    m, k = ckv.shape
    n = wkv.shape[1]
    return pl.pallas_call(
        _kvup_kernel,
        out_shape=jax.ShapeDtypeStruct((m, n), BF16),
        grid=(m // tm,),
        in_specs=[pl.BlockSpec((tm, k), lambda i: (i, 0)),
                  pl.BlockSpec((k, n), lambda i: (0, 0))],
        out_specs=pl.BlockSpec((tm, n), lambda i: (i, 0)),
        compiler_params=_params("arbitrary"),
        name="mla_kv_up",
    )(ckv, wkv)


def _proj_kernel(*refs, rope, scale):
    if rope:
        h_ref, w_ref, c_ref, s_ref, o_ref = refs
    else:
        h_ref, w_ref, o_ref = refs
    r = jnp.dot(h_ref[...], w_ref[...].astype(BF16), preferred_element_type=F32)
    if rope:
        c, s = c_ref[...], s_ref[...]
        for i in range(r.shape[1] // HEAD_DIM):
            sl = slice(i * HEAD_DIM, (i + 1) * HEAD_DIM)
            y = _rope_half(r[:, sl], c, s, HEAD_DIM // 2)
            o_ref[:, sl] = (y if scale is None else y * scale).astype(o_ref.dtype)
    else:
        o_ref[...] = (r if scale is None else r * scale).astype(o_ref.dtype)


def _proj(h, w3, col0, ncols, out_dtype, grp, use_rope, scale=None, tm=2048, tn=512):
    m, d = h.shape
    rope = use_rope and grp.rope is not None
    tm = min(tm, m)
    off = col0 // tn
    in_specs = [pl.BlockSpec((tm, d), lambda i, j: (i, 0)),
                pl.BlockSpec((None, d, tn), lambda i, j: (0, 0, off + j))]
    args = [h, w3]
    if rope:
        assert tm % grp.seq == 0
        for name in ("c128", "s128"):
            in_specs.append(pl.BlockSpec((tm, LANES), lambda i, j: (0, 0)))
            args.append(jnp.tile(grp.rope[name], (tm // grp.seq, 1)))
    return pl.pallas_call(
        functools.partial(_proj_kernel, rope=rope, scale=scale),
        out_shape=jax.ShapeDtypeStruct((m, ncols), out_dtype),
        grid=(m // tm, ncols // tn),
        in_specs=in_specs,
        out_specs=pl.BlockSpec((tm, tn), lambda i, j: (i, j)),
        compiler_params=_params("arbitrary", "arbitrary"),
        name="proj_c",
    )(*args)


def _softmax_parts(scores):
    mx = jnp.max(scores[0], axis=-1, keepdims=True)
    for s in scores[1:]:
        mx = jnp.maximum(mx, jnp.max(s, axis=-1, keepdims=True))
    ps = [jnp.exp2(s - mx) for s in scores]
    l = jnp.sum(ps[0], axis=-1, keepdims=True)
    for p in ps[1:]:
        l = l + jnp.sum(p, axis=-1, keepdims=True)
    return ps, l


def _attend(q, ks, vs):
    scores = [lax.dot_general(q, k, _NT, preferred_element_type=F32) for k in ks]
    ps, l = _softmax_parts(scores)
    o = jnp.dot(ps[0].astype(BF16), vs[0], preferred_element_type=F32)
    for p, v in zip(ps[1:], vs[1:]):
        o = o + jnp.dot(p.astype(BF16), v, preferred_element_type=F32)
    return o, l


_NT = (((1,), (1,)), ((), ()))


def _mla_kernel(*refs, n_seg, hg):
    q_ref = refs[0]
    o_ref = refs[-1]
    hw = MLA_NOPE + LANES
    for h in range(hg):
        ks, vs = [], []
        for i in range(n_seg):
            kv_ref, kr_ref = refs[1 + 2 * i], refs[2 + 2 * i]
            ks.append(jnp.concatenate([kv_ref[:, h * hw:h * hw + MLA_NOPE], kr_ref[...]], axis=1))
            vs.append(kv_ref[:, h * hw + MLA_NOPE:(h + 1) * hw])
        o, l = _attend(q_ref[:, h * hw:(h + 1) * hw], ks, vs)
        o_ref[:, h * MLA_V:(h + 1) * MLA_V] = (o / l).astype(o_ref.dtype)


def _mla_attention(q, segs, grp, heads, tq, hg):
    m = q.shape[0]
    nq = grp.seq // tq
    hw = MLA_NOPE + LANES
    in_specs = [pl.BlockSpec((tq, hg * hw), lambda b, h, i: (b * nq + i, h))]
    args = [q]
    for kv, kr, rows in segs:
        in_specs.append(pl.BlockSpec((rows, hg * (MLA_NOPE + MLA_V)), lambda b, h, i: (b, h)))
        in_specs.append(pl.BlockSpec((rows, LANES), lambda b, h, i: (b, 0)))
        args += [kv, kr]
    return pl.pallas_call(
        functools.partial(_mla_kernel, n_seg=len(segs), hg=hg),
        out_shape=jax.ShapeDtypeStruct((m, heads * MLA_V), BF16),
        grid=(grp.batch, heads // hg, nq),
        in_specs=in_specs,
        out_specs=pl.BlockSpec((tq, hg * MLA_V), lambda b, h, i: (b * nq + i, h)),
        compiler_params=_params("arbitrary", "arbitrary", "arbitrary"),
        name="attn_mla",
    )(*args)


def _gqa_kernel(*refs, n_seg, group, kvg):
    q_ref = refs[0]
    o_ref = refs[-1]
    for kh in range(kvg):
        ksl = slice(kh * HEAD_DIM, (kh + 1) * HEAD_DIM)
        ks = [refs[1 + 2 * i][:, ksl].astype(BF16) for i in range(n_seg)]
        vs = [refs[2 + 2 * i][:, ksl].astype(BF16) for i in range(n_seg)]
        for g in range(group):
            sl = slice((kh * group + g) * HEAD_DIM, (kh * group + g + 1) * HEAD_DIM)
            o, l = _attend(q_ref[:, sl], ks, vs)
            o_ref[:, sl] = (o / l).astype(o_ref.dtype)


def _gqa_attention(q, segs, grp, heads, kv_heads, tq, kvg):
    m = q.shape[0]
    nq = grp.seq // tq
    group = heads // kv_heads
    gw = kvg * group * HEAD_DIM
    in_specs = [pl.BlockSpec((tq, gw), lambda b, h, i: (b * nq + i, h))]
    args = [q]
    for k, v, rows in segs:
        in_specs.append(pl.BlockSpec((rows, kvg * HEAD_DIM), lambda b, h, i: (b, h)))
        in_specs.append(pl.BlockSpec((rows, kvg * HEAD_DIM), lambda b, h, i: (b, h)))
        args += [k, v]
    return pl.pallas_call(
        functools.partial(_gqa_kernel, n_seg=len(segs), group=group, kvg=kvg),
        out_shape=jax.ShapeDtypeStruct((m, heads * HEAD_DIM), BF16),
        grid=(grp.batch, kv_heads // kvg, nq),
        in_specs=in_specs,
        out_specs=pl.BlockSpec((tq, gw), lambda b, h, i: (b * nq + i, h)),
        compiler_params=_params("arbitrary", "arbitrary", "arbitrary"),
        name="attn_gqa",
    )(*args)


def _diff_kernel(*refs, n_seg, hg, lambda_init):
    q_ref, lq1_ref, lk1_ref, lq2_ref, lk2_ref, g_ref = refs[:6]
    o_ref = refs[-1]
    lam = (jnp.exp(jnp.sum(lq1_ref[...] * lk1_ref[...], axis=-1, keepdims=True))
           - jnp.exp(jnp.sum(lq2_ref[...] * lk2_ref[...], axis=-1, keepdims=True)) + lambda_init)
    hw = 2 * HEAD_DIM
    for h in range(hg):
        vs = [refs[7 + 2 * i][:, h * hw:(h + 1) * hw].astype(BF16) for i in range(n_seg)]
        halves = []
        for j in range(2):
            sl = slice(h * hw + j * HEAD_DIM, h * hw + (j + 1) * HEAD_DIM)
            ks = [refs[6 + 2 * i][:, sl].astype(BF16) for i in range(n_seg)]
            o, l = _attend(q_ref[:, sl], ks, vs)
            halves.append(o * (1.0 / l))
        o = halves[0] - lam * halves[1]
        o_ref[:, h * hw:(h + 1) * hw] = (_rms(o, g_ref[...]) * (1.0 - lambda_init)).astype(o_ref.dtype)


def _diff_attention(q, segs, lams, g_sub, grp, heads, tq, hg, lambda_init):
    m = q.shape[0]
    nq = grp.seq // tq
    hw = 2 * HEAD_DIM
    vec = lambda w: pl.BlockSpec((1, w), lambda b, h, i: (0, 0))
    in_specs = [pl.BlockSpec((tq, hg * hw), lambda b, h, i: (b * nq + i, h))]
    in_specs += [vec(HEAD_DIM)] * 4 + [vec(hw)]
    args = [q, *lams, g_sub]
    for k, v, rows in segs:
        in_specs.append(pl.BlockSpec((rows, hg * hw), lambda b, h, i: (b, h)))
        in_specs.append(pl.BlockSpec((rows, hg * hw), lambda b, h, i: (b, h)))
        args += [k, v]
    return pl.pallas_call(
        functools.partial(_diff_kernel, n_seg=len(segs), hg=hg, lambda_init=lambda_init),
        out_shape=jax.ShapeDtypeStruct((m, heads * hw), BF16),
        grid=(grp.batch, heads // hg, nq),
        in_specs=in_specs,
        out_specs=pl.BlockSpec((tq, hg * hw), lambda b, h, i: (b * nq + i, h)),
        compiler_params=_params("arbitrary", "arbitrary", "arbitrary"),
        name="attn_diff",
    )(*args)


def _mm_res_kernel(*refs, n_a, nk):
    a_refs = refs[:n_a]
    w_refs = refs[n_a:2 * n_a]
    x_ref, gt_ref, o_ref = refs[2 * n_a:2 * n_a + 3]
    if nk == 1:
        part = None
        for a_ref, w_ref in zip(a_refs, w_refs):
            p = jnp.dot(a_ref[...], w_ref[...].astype(BF16), preferred_element_type=F32)
            part = p if part is None else part + p
        o_ref[...] = x_ref[...] + gt_ref[...] * part
        return
    k = pl.program_id(2)

    def part():
        w = (w_refs[0][...] * gt_ref[...]).astype(BF16)
        return jnp.dot(a_refs[0][...], w, preferred_element_type=F32)

    @pl.when(k == 0)
    def _():
        o_ref[...] = x_ref[...] + part()

    @pl.when(k > 0)
    def _():
        o_ref[...] += part()


def _mm_res(a_list, w3, w_layer, x, mod, layer, chunk, grp, tk, tm=2048, tn=512):
    m, n = x.shape
    tm = grp.mod_tile(tm)
    n_a = len(a_list)
    ka = a_list[0].shape[1]
    if n_a > 1:
        tk = ka
    nk = ka // tk
    ncol = n // tn
    in_specs = [pl.BlockSpec((tm, tk), lambda i, j, k: (i, k)) for _ in a_list]
    for idx in range(n_a):
        in_specs.append(pl.BlockSpec((None, tk, tn),
                                     lambda i, j, k, idx=idx: (w_layer, idx * (ka // tk) + k, j)))
    in_specs.append(pl.BlockSpec((tm, tn), lambda i, j, k: (i, j)))
    in_specs.append(pl.BlockSpec(
        (None, None, 1, tn),
        lambda i, j, k: (layer, grp.mod_row(i, tm), 0, chunk * ncol + j)))
    return pl.pallas_call(
        functools.partial(_mm_res_kernel, n_a=n_a, nk=nk),
        out_shape=jax.ShapeDtypeStruct((m, n), F32),
        grid=(m // tm, ncol, nk),
        in_specs=in_specs,
        out_specs=pl.BlockSpec((tm, tn), lambda i, j, k: (i, j)),
        compiler_params=_params("arbitrary", "arbitrary", "arbitrary"),
        name="mm_res",
    )(*a_list, *([w3] * n_a), x, mod)


def _ffn_up_kernel(h_ref, wg_ref, wv_ref, cg_ref, cv_ref, bg_ref, bv_ref, o_ref, *, seq):
    h = h_ref[...]
    tm = h.shape[0]
    row = lax.broadcasted_iota(jnp.int32, (tm, 1), 0) % seq
    first = row == 0
    last = row == seq - 1

    def conv(w_ref, c_ref, b_ref):
        u = jnp.dot(h, w_ref[...].astype(BF16), preferred_element_type=F32)
        prev = jnp.where(first, 0.0, pltpu.roll(u, 1, 0))
        nxt = jnp.where(last, 0.0, pltpu.roll(u, tm - 1, 0))
        return prev * c_ref[0:1, :] + u * c_ref[1:2, :] + nxt * c_ref[2:3, :] + b_ref[...]

    gate = conv(wg_ref, cg_ref, bg_ref)
    val = conv(wv_ref, cv_ref, bv_ref)
    o_ref[...] = (_silu(gate) * val).astype(o_ref.dtype)


def _ffn_up(h, w_up, w_conv, b_conv, layer, grp, tm=1024, tn=256):
    m, d = h.shape
    d_ff = w_up.shape[2] // 2
    nj = d_ff // tn
    assert tm % grp.seq == 0 and m % tm == 0
    b3 = b_conv.reshape(b_conv.shape[0], 1, b_conv.shape[1])
    return pl.pallas_call(
        functools.partial(_ffn_up_kernel, seq=grp.seq),
        out_shape=jax.ShapeDtypeStruct((m, d_ff), BF16),
        grid=(m // tm, nj),
        in_specs=[
            pl.BlockSpec((tm, d), lambda i, j: (i, 0)),
            pl.BlockSpec((None, d, tn), lambda i, j: (layer, 0, j)),
            pl.BlockSpec((None, d, tn), lambda i, j: (layer, 0, nj + j)),
            pl.BlockSpec((None, w_conv.shape[1], tn), lambda i, j: (layer, 0, j)),
            pl.BlockSpec((None, w_conv.shape[1], tn), lambda i, j: (layer, 0, nj + j)),
            pl.BlockSpec((None, 1, tn), lambda i, j: (layer, 0, j)),
            pl.BlockSpec((None, 1, tn), lambda i, j: (layer, 0, nj + j)),
        ],
        out_specs=pl.BlockSpec((tm, tn), lambda i, j: (i, j)),
        compiler_params=_params("arbitrary", "arbitrary"),
        name="ffn_up",
    )(h, w_up, w_up, w_conv, w_conv, b3, b3)


def _rope_tables(n_tokens, rot_dim):
    n_rows = n_tokens // GRID_W
    row = jnp.repeat(jnp.arange(n_rows), GRID_W).astype(F32)
    col = jnp.tile(jnp.arange(GRID_W), n_rows).astype(F32)
    n_freq = rot_dim // 4
    freqs = ROPE_BASE ** (-jnp.arange(n_freq, dtype=F32) / n_freq)
    ang = jnp.concatenate([row[:, None] * freqs, col[:, None] * freqs], axis=-1)
    cos, sin = jnp.cos(ang), jnp.sin(ang)
    reps = LANES // rot_dim
    c = jnp.tile(jnp.concatenate([cos, cos], axis=-1), (1, reps))
    s = jnp.tile(jnp.concatenate([-sin, sin], axis=-1), (1, reps))
    return c, s


def _trunk(x, grp, mod, caches, P):
    d = x.shape[1]
    ckv_p = krope_p = kb_p = vb_p = k_c = v_c = None
    depth = P["w_ffn_up"].shape[0]
    for l in range(depth):
        i = l // 2
        h = _norm_mod(x, P["g_norm_mix"], mod, l, 1, 0, grp)
        if l % 2 == 0:
            (q_mla, kv, ckv, krope, krope_pad, qb, kb, vb) = _inproj_ab(
                h, P["w_in_ab_p"][i], P["w_q_up_p"][i], P["w_kv_up_b"][i],
                P["g_mla_q"][i:i + 1], P["g_mla_kv"][i:i + 1], P["g_gqa_q"][i:i + 1],
                P["g_gqa_k"][i:i + 1], grp, P["dims_ab"])
            ckv_p, krope_p, kb_p, vb_p = ckv, krope, kb, vb
            heads = P["dims_ab"][5]
            segs_a, segs_b = [], []
            if caches is not None:
                c_ckv, c_kr, c_k, c_v = (cc[i] for cc in caches[:4])
                past = c_ckv.shape[0] // grp.batch
                segs_a.append((_kvup(c_ckv, P["w_kv_up_b"][i]), c_kr, past))
                segs_b.append((c_k, c_v, past))
            segs_a.append((kv, krope_pad, grp.seq))
            segs_b.append((kb, vb, grp.seq))
            tq = min(grp.seq, 512)
            small = grp.seq <= 256
            out_a = _mla_attention(q_mla, segs_a, grp, heads, tq, heads if small else 2)
            out_b = _gqa_attention(qb, segs_b, grp, qb.shape[1] // HEAD_DIM, GQA_KV_HEADS, tq,
                                   GQA_KV_HEADS if small else 1)
            x = _mm_res([out_a, out_b], P["w_out_ab"], i, x, mod, l, 2, grp, tk=None)
        else:
            w_in_c = P["w_in_c"][i:i + 1]
            nq = w_in_c.shape[2] // 3
            kv_dtype = F32 if caches is None else BF16
            q = _proj(h, w_in_c, 0, nq, BF16, grp, True, scale=HEAD_DIM ** -0.5 * LOG2E)
            k = _proj(h, w_in_c, nq, nq, kv_dtype, grp, True)
            v = _proj(h, w_in_c, 2 * nq, nq, kv_dtype, grp, False)
            k_c, v_c = k, v
            segs = []
            if caches is not None:
                c_k, c_v = (cc[i] for cc in caches[4:])
                segs.append((c_k, c_v, c_k.shape[0] // grp.batch))
            segs.append((k, v, grp.seq))
            lambda_init = 0.8 - 0.6 * math.exp(-0.3 * l)
            lams = [P[n][i:i + 1] for n in ("lambda_q1", "lambda_k1", "lambda_q2", "lambda_k2")]
            dheads = nq // (2 * HEAD_DIM)
            o = _diff_attention(q, segs, lams, P["g_diff_sub"][i:i + 1], grp, dheads,
                                min(grp.seq, 512), dheads if grp.seq <= 256 else 1, lambda_init)
            x = _mm_res([o], P["w_out_c"], i, x, mod, l, 2, grp, tk=o.shape[1])
        h = _norm_mod(x, P["g_norm_ffn"], mod, l, 4, 3, grp)
        g = _ffn_up(h, P["w_ffn_up"], P["w_ffn_conv"], P["b_ffn_conv"], l, grp)
        x = _mm_res([g], P["w_ffn_down"], l, x, mod, l, 5, grp, tk=512, tn=1024)
    y = _final_norm(x, P["g_final"])
    return y, (ckv_p, krope_p, kb_p, vb_p), (k_c, v_c)


def kernel(x_prompt, x_sample, cache_mla_ckv, cache_mla_krope, cache_gqa_k, cache_gqa_v,
           cache_diff_k, cache_diff_v, c, c_ctx, w_ada, b_ada, g_norm_mix, g_norm_ffn,
           w_in_ab, g_mla_q, w_mla_q_up, g_mla_kv, w_mla_kv_up, g_gqa_q, g_gqa_k, w_out_ab,
           w_in_c, lambda_q1, lambda_k1, lambda_q2, lambda_k2, g_diff_sub, w_out_c,
           w_ffn_up, w_ffn_conv, b_ffn_conv, w_ffn_down, g_final):
    bp, tp, d = x_prompt.shape
    bs, ts, _ = x_sample.shape
    depth = w_ada.shape[0]
    n_even = w_in_ab.shape[0]
    assert depth == 2 and n_even == 1 and w_in_c.shape[0] == 1, "one MLA/GQA layer + one diff layer"
    past = cache_mla_ckv.shape[2]
    n_qlat = g_mla_q.shape[1]
    n_ckv = g_mla_kv.shape[1]
    mla_heads = w_mla_q_up.shape[2] // (MLA_NOPE + MLA_ROPE)
    n_kb = cache_gqa_k.shape[3] * HEAD_DIM
    n_vb = n_kb
    n_qb = w_in_ab.shape[2] - n_qlat - n_ckv - MLA_ROPE - n_kb - n_vb

    rows = 8
    cond = jnp.zeros((rows, d), F32).at[0].set(c_ctx).at[1:1 + bs].set(c)
    mod = _modulation(cond, w_ada, b_ada)
    mod = mod.reshape(depth, rows, 1, mod.shape[2])

    o0 = n_qlat + n_ckv
    o1 = o0 + MLA_ROPE
    w_in_ab_p = jnp.concatenate(
        [w_in_ab[:, :, :o0], w_in_ab[:, :, o1:], w_in_ab[:, :, o0:o1],
         jnp.zeros((n_even, d, LANES - MLA_ROPE), F32)], axis=2).astype(BF16)
    wq = w_mla_q_up.reshape(n_even, n_qlat, mla_heads, MLA_NOPE + MLA_ROPE)
    wq = wq * ((MLA_NOPE + MLA_ROPE) ** -0.5 * LOG2E)
    w_q_up_p = jnp.pad(wq, ((0, 0), (0, 0), (0, 0), (0, LANES - MLA_ROPE))).reshape(
        n_even, n_qlat, mla_heads * (MLA_NOPE + LANES)).astype(BF16)
    P = dict(
        w_in_ab_p=w_in_ab_p, w_q_up_p=w_q_up_p, w_kv_up_b=w_mla_kv_up.astype(BF16),
        g_mla_q=g_mla_q, g_mla_kv=g_mla_kv, g_gqa_q=g_gqa_q * (HEAD_DIM ** -0.5 * LOG2E),
        g_gqa_k=g_gqa_k,
        w_out_ab=w_out_ab, w_in_c=w_in_c, lambda_q1=lambda_q1, lambda_k1=lambda_k1,
        lambda_q2=lambda_q2, lambda_k2=lambda_k2, g_diff_sub=g_diff_sub, w_out_c=w_out_c,
        w_ffn_up=w_ffn_up, w_ffn_conv=w_ffn_conv, b_ffn_conv=b_ffn_conv, w_ffn_down=w_ffn_down,
        g_final=g_final,
        g_norm_mix=g_norm_mix.reshape(depth, 1, d), g_norm_ffn=g_norm_ffn.reshape(depth, 1, d),
        dims_ab=(n_qlat, n_ckv, n_qb, n_kb, n_vb, mla_heads),
    )

    grp_p = _Group(bp, tp, 0, False, None)
    y_p, ab_p, c_p = _trunk(x_prompt.reshape(bp * tp, d), grp_p, mod, None, P)

    c128, s128 = _rope_tables(ts, HEAD_DIM)
    c64, s64 = _rope_tables(ts, MLA_ROPE)
    grp_s = _Group(bs, ts, 1, True, dict(c128=c128, s128=s128, c64=c64, s64=s64))
    n_odd = cache_diff_k.shape[1]
    kr_pad = jnp.pad(cache_mla_krope, ((0, 0), (0, 0), (0, 0), (0, LANES - MLA_ROPE))).astype(BF16)
    per_layer = lambda a, n: [a[:, j].reshape(bs * past, -1) for j in range(n)]
    caches = (per_layer(cache_mla_ckv, n_even), per_layer(kr_pad, n_even),
              per_layer(cache_gqa_k, n_even), per_layer(cache_gqa_v, n_even),
              per_layer(cache_diff_k.astype(BF16), n_odd),
              per_layer(cache_diff_v.astype(BF16), n_odd))
    y_s, _, _ = _trunk(x_sample.reshape(bs * ts, d), grp_s, mod, caches, P)

    ckv, krope, kb, vb = ab_p
    k_c, v_c = c_p
    kvh = n_kb // HEAD_DIM
    dh = k_c.shape[1] // (2 * HEAD_DIM)
    return (y_p.reshape(bp, tp, d), y_s.reshape(bs, ts, d),
            ckv.reshape(bp, 1, tp, n_ckv), krope.reshape(bp, 1, tp, MLA_ROPE),
            kb.reshape(bp, 1, tp, kvh, HEAD_DIM), vb.reshape(bp, 1, tp, kvh, HEAD_DIM),
            k_c.reshape(bp, 1, tp, dh, 2, HEAD_DIM), v_c.reshape(bp, 1, tp, dh, 2 * HEAD_DIM))
```

```python
import functools
import math

import jax
import jax.numpy as jnp
from jax import lax
from jax.experimental import pallas as pl
from jax.experimental.pallas import tpu as pltpu

F32 = jnp.float32
BF16 = jnp.bfloat16

HEAD_DIM = 128
GRID_W = 64
ROPE_BASE = 10000.0
EPS = 1e-6
MLA_NOPE = 128
MLA_ROPE = 64
MLA_V = 128
GQA_KV_HEADS = 2
LOG2E = 1.4426950408889634
LANES = 128
FFN_EDGE_ROWS = 16
VMEM_LIMIT = 56 * 1024 * 1024


def _params(*sem):
    return pltpu.CompilerParams(dimension_semantics=sem, vmem_limit_bytes=VMEM_LIMIT)


def _rms(x, g):
    ms = jnp.mean(x * x, axis=-1, keepdims=True)
    return x * lax.rsqrt(ms + EPS) * g


def _silu(x):
    return x / (1.0 + jnp.exp(-x))


def _rope_half(x, c, s, half):
    if 2 * half == LANES:
        sw = pltpu.roll(x, half, 1)
    else:
        lane = lax.broadcasted_iota(jnp.int32, x.shape, 1)
        first = (lane % (2 * half)) < half
        sw = jnp.where(first, pltpu.roll(x, LANES - half, 1), pltpu.roll(x, half, 1))
    return x * c + sw * s


def _mod_kernel(cond_ref, w_ref, b_ref, o_ref):
    s = _silu(cond_ref[...]).astype(BF16)
    o_ref[...] = jnp.dot(s, w_ref[...].astype(BF16), preferred_element_type=F32) + b_ref[...]


def _modulation(cond, w_ada, b_ada, tn=1024):
    depth, d, n = w_ada.shape
    rows = cond.shape[0]
    return pl.pallas_call(
        _mod_kernel,
        out_shape=jax.ShapeDtypeStruct((depth, rows, n), F32),
        grid=(depth, n // tn),
        in_specs=[
            pl.BlockSpec((rows, d), lambda l, j: (0, 0)),
            pl.BlockSpec((None, d, tn), lambda l, j: (l, 0, j)),
            pl.BlockSpec((None, 1, tn), lambda l, j: (l, 0, j)),
        ],
        out_specs=pl.BlockSpec((None, rows, tn), lambda l, j: (l, 0, j)),
        compiler_params=_params("arbitrary", "arbitrary"),
        name="adaln_mod",
    )(cond, w_ada, b_ada.reshape(depth, 1, n))


def _norm_mod_tile(x_ref, g_ref, sc_ref, sh_ref):
    y = _rms(x_ref[...], g_ref[...])
    return (y * (1.0 + sc_ref[...]) + sh_ref[...]).astype(BF16)


def _norm_kernel(x_ref, g_ref, o_ref):
    o_ref[...] = _rms(x_ref[...], g_ref[...]).astype(o_ref.dtype)


class _Group:
    def __init__(self, batch, seq, mod_row0, per_batch_mod, rope):
        self.batch = batch
        self.seq = seq
        self.m = batch * seq
        self.mod_row0 = mod_row0
        self.per_batch_mod = per_batch_mod
        self.rope = rope

    def mod_rows(self, tm):
        if self.per_batch_mod and tm > self.seq:
            assert tm % self.seq == 0 and self.mod_row0 % (tm // self.seq) == 0
            return tm // self.seq
        return 1

    def mod_spec(self, layer, col, width, tm):
        nb = self.mod_rows(tm)
        if self.per_batch_mod:
            assert tm % self.seq == 0 or self.seq % tm == 0
            row = lambda i: (self.mod_row0 + (i * tm) // self.seq) // nb
        else:
            row = lambda i: self.mod_row0
        col_fn = col if callable(col) else (lambda *idx: col)
        return pl.BlockSpec((None, nb, 1, width),
                            lambda *idx: (layer, row(idx[0]), 0, col_fn(*idx)))

    def norm_specs(self, layer, chunk_sc, chunk_sh, d, tm):
        assert self.mod_rows(tm) == 1
        vec = lambda chunk: pl.BlockSpec(
            (None, None, 1, d), self.mod_spec(layer, chunk, d, tm).index_map)
        return [pl.BlockSpec((None, 1, d), lambda *idx: (layer, 0, 0)), vec(chunk_sc), vec(chunk_sh)]


def _final_norm(x, g, tm=512):
    m, d = x.shape
    return pl.pallas_call(
        _norm_kernel,
        out_shape=jax.ShapeDtypeStruct((m, d), F32),
        grid=(m // tm,),
        in_specs=[pl.BlockSpec((tm, d), lambda i: (i, 0)),
                  pl.BlockSpec((1, d), lambda i: (0, 0))],
        out_specs=pl.BlockSpec((tm, d), lambda i: (i, 0)),
        compiler_params=_params("arbitrary"),
        name="final_norm",
    )(x, g.reshape(1, d))


def _inproj_ab_kernel(*refs, rope, n_qlat, n_ckv, n_qb, n_kb, n_vb, mla_heads):
    (x_ref, gn_ref, sc_ref, sh_ref, wa_ref, wb_ref, wk_ref, wq_ref, wkv_ref,
     gq_ref, gkv_ref, gqn_ref, gkn_ref) = refs[:13]
    (qmla_ref, kv_ref, ckv_ref, kr_ref, krp_ref, qb_ref, kb_ref, vb_ref) = refs[-8:]
    if rope:
        c128, s128, c64, s64 = (r[...] for r in refs[13:17])
    h = _norm_mod_tile(x_ref, gn_ref, sc_ref, sh_ref)

    def seg(w_ref, a, b):
        return jnp.dot(h, w_ref[:, a:b], preferred_element_type=F32)

    qn = _rms(seg(wa_ref, 0, n_qlat), gq_ref[...]).astype(BF16)
    qa = jnp.dot(qn, wq_ref[...], preferred_element_type=F32)
    hw = MLA_NOPE + LANES
    for i in range(mla_heads):
        qmla_ref[:, i * hw:i * hw + MLA_NOPE] = qa[:, i * hw:i * hw + MLA_NOPE].astype(BF16)
        rp = qa[:, i * hw + MLA_NOPE:(i + 1) * hw]
        if rope:
            rp = _rope_half(rp, c64, s64, MLA_ROPE // 2)
        qmla_ref[:, i * hw + MLA_NOPE:(i + 1) * hw] = rp.astype(BF16)
    cn = _rms(seg(wa_ref, n_qlat, n_qlat + n_ckv), gkv_ref[...])
    ckv_ref[...] = cn
    kv_ref[...] = jnp.dot(cn.astype(BF16), wkv_ref[...], preferred_element_type=F32).astype(BF16)
    r = seg(wb_ref, 0, n_qb)
    for i in range(n_qb // HEAD_DIM):
        y = _rms(r[:, i * HEAD_DIM:(i + 1) * HEAD_DIM], gqn_ref[...])
        if rope:
            y = _rope_half(y, c128, s128, HEAD_DIM // 2)
        qb_ref[:, i * HEAD_DIM:(i + 1) * HEAD_DIM] = y.astype(BF16)
    r = seg(wb_ref, n_qb, n_qb + n_kb)
    for i in range(n_kb // HEAD_DIM):
        y = _rms(r[:, i * HEAD_DIM:(i + 1) * HEAD_DIM], gkn_ref[...])
        if rope:
            y = _rope_half(y, c128, s128, HEAD_DIM // 2)
        kb_ref[:, i * HEAD_DIM:(i + 1) * HEAD_DIM] = y
    vb_ref[...] = seg(wb_ref, n_qb + n_kb, n_qb + n_kb + n_vb)
    r = seg(wk_ref, 0, LANES)
    if rope:
        r = _rope_half(r, c64, s64, MLA_ROPE // 2)
    kr_ref[...] = r[:, :MLA_ROPE]
    krp_ref[...] = r.astype(BF16)


def _inproj_ab(x, norm, weights, g_q, g_kv, g_qn, g_kn, grp, dims, tm=256):
    m, d = x.shape
    n_qlat, n_ckv, n_qb, n_kb, n_vb, mla_heads = dims
    g_norm, mod, layer, chunk_sc, chunk_sh = norm
    rope = grp.rope is not None
    one = pl.Buffered(1)
    nq_out = mla_heads * (MLA_NOPE + LANES)
    nkv_out = weights[4].shape[1]
    in_specs = [pl.BlockSpec((tm, d), lambda i: (i, 0))]
    in_specs += grp.norm_specs(layer, chunk_sc, chunk_sh, d, tm)
    in_specs += [pl.BlockSpec(w.shape, lambda i: (0, 0), pipeline_mode=one) for w in weights]
    in_specs += [
        pl.BlockSpec((1, n_qlat), lambda i: (0, 0)),
        pl.BlockSpec((1, n_ckv), lambda i: (0, 0)),
        pl.BlockSpec((1, HEAD_DIM), lambda i: (0, 0)),
        pl.BlockSpec((1, HEAD_DIM), lambda i: (0, 0)),
    ]
    args = [x, g_norm, mod, mod, *weights, g_q, g_kv, g_qn, g_kn]
    if rope:
        per = grp.seq // tm
        for name in ("c128", "s128", "c64", "s64"):
            in_specs.append(pl.BlockSpec((tm, LANES), lambda i: (i % per, 0)))
            args.append(grp.rope[name])
    widths = [(nq_out, BF16), (nkv_out, BF16), (n_ckv, F32), (MLA_ROPE, F32), (LANES, BF16),
              (n_qb, BF16), (n_kb, F32), (n_vb, F32)]
    return pl.pallas_call(
        functools.partial(_inproj_ab_kernel, rope=rope, n_qlat=n_qlat, n_ckv=n_ckv, n_qb=n_qb,
                          n_kb=n_kb, n_vb=n_vb, mla_heads=mla_heads),
        out_shape=[jax.ShapeDtypeStruct((m, w), dt) for w, dt in widths],
        grid=(m // tm,),
        in_specs=in_specs,
        out_specs=[pl.BlockSpec((tm, w), lambda i: (i, 0)) for w, _ in widths],
        compiler_params=_params("arbitrary"),
        name="inproj_ab",
    )(*args)


def _kvup_kernel(c_ref, w_ref, o_ref):
    o_ref[...] = jnp.dot(c_ref[...].astype(BF16), w_ref[...],
                         preferred_element_type=F32).astype(o_ref.dtype)


def _kvup(ckv, wkv, tm=512):
    m, k = ckv.shape
    n = wkv.shape[1]
    return pl.pallas_call(
        _kvup_kernel,
        out_shape=jax.ShapeDtypeStruct((m, n), BF16),
        grid=(m // tm,),
        in_specs=[pl.BlockSpec((tm, k), lambda i: (i, 0)),
                  pl.BlockSpec((k, n), lambda i: (0, 0))],
        out_specs=pl.BlockSpec((tm, n), lambda i: (i, 0)),
        compiler_params=_params("arbitrary"),
        name="mla_kv_up",
    )(ckv, wkv)


def _proj_kernel(*refs, rope, scale, norm):
    if norm:
        x_ref, gn_ref, sc_ref, sh_ref, *rest = refs
        h_ref = rest.pop()

        @pl.when(pl.program_id(1) == 0)
        def _():
            h_ref[...] = _norm_mod_tile(x_ref, gn_ref, sc_ref, sh_ref)
    else:
        h_ref, *rest = refs
    if rope:
        w_ref, c_ref, s_ref, o_ref = rest
    else:
        w_ref, o_ref = rest
    r = jnp.dot(h_ref[...], w_ref[...].astype(BF16), preferred_element_type=F32)
    if rope:
        c, s = c_ref[...], s_ref[...]
        for i in range(r.shape[1] // HEAD_DIM):
            sl = slice(i * HEAD_DIM, (i + 1) * HEAD_DIM)
            y = _rope_half(r[:, sl], c, s, HEAD_DIM // 2)
            o_ref[:, sl] = (y if scale is None else y * scale).astype(o_ref.dtype)
    else:
        o_ref[...] = (r if scale is None else r * scale).astype(o_ref.dtype)


def _proj(h, w3, col0, ncols, out_dtype, grp, use_rope, scale=None, norm=None, tm=2048, tn=512):
    m, d = h.shape
    rope = use_rope and grp.rope is not None
    tm = min(tm, m)
    off = col0 // tn
    in_specs = [pl.BlockSpec((tm, d), lambda i, j: (i, 0))]
    args = [h]
    out_shape = [jax.ShapeDtypeStruct((m, ncols), out_dtype)]
    out_specs = [pl.BlockSpec((tm, tn), lambda i, j: (i, j))]
    if norm is not None:
        g_norm, mod, layer, chunk_sc, chunk_sh = norm
        in_specs += grp.norm_specs(layer, chunk_sc, chunk_sh, d, tm)
        args += [g_norm, mod, mod]
        out_shape.append(jax.ShapeDtypeStruct((m, d), BF16))
        out_specs.append(pl.BlockSpec((tm, d), lambda i, j: (i, 0)))
    in_specs.append(pl.BlockSpec((None, d, tn), lambda i, j: (0, 0, off + j)))
    args.append(w3)
    if rope:
        assert tm % grp.seq == 0
        for name in ("c128", "s128"):
            in_specs.append(pl.BlockSpec((tm, LANES), lambda i, j: (0, 0)))
            args.append(jnp.tile(grp.rope[name], (tm // grp.seq, 1)))
    res = pl.pallas_call(
        functools.partial(_proj_kernel, rope=rope, scale=scale, norm=norm is not None),
        out_shape=out_shape,
        grid=(m // tm, ncols // tn),
        in_specs=in_specs,
        out_specs=out_specs,
        compiler_params=_params("arbitrary", "arbitrary"),
        name="proj_c",
    )(*args)
    return res if norm is not None else res[0]


def _softmax_parts(scores):
    mx = jnp.max(scores[0], axis=-1, keepdims=True)
    for s in scores[1:]:
        mx = jnp.maximum(mx, jnp.max(s, axis=-1, keepdims=True))
    ps = [jnp.exp2(s - mx) for s in scores]
    l = jnp.sum(ps[0], axis=-1, keepdims=True)
    for p in ps[1:]:
        l = l + jnp.sum(p, axis=-1, keepdims=True)
    return ps, l


def _attend(q, ks, vs):
    scores = [lax.dot_general(q, k, _NT, preferred_element_type=F32) for k in ks]
    ps, l = _softmax_parts(scores)
    o = jnp.dot(ps[0].astype(BF16), vs[0], preferred_element_type=F32)
    for p, v in zip(ps[1:], vs[1:]):
        o = o + jnp.dot(p.astype(BF16), v, preferred_element_type=F32)
    return o, l


_NT = (((1,), (1,)), ((), ()))


def _mla_kernel(*refs, n_seg, hg):
    q_ref = refs[0]
    o_ref = refs[-1]
    hw = MLA_NOPE + LANES
    for h in range(hg):
        ks, vs = [], []
        for i in range(n_seg):
            kv_ref, kr_ref = refs[1 + 2 * i], refs[2 + 2 * i]
            ks.append(jnp.concatenate([kv_ref[:, h * hw:h * hw + MLA_NOPE], kr_ref[...]], axis=1))
            vs.append(kv_ref[:, h * hw + MLA_NOPE:(h + 1) * hw])
        o, l = _attend(q_ref[:, h * hw:(h + 1) * hw], ks, vs)
        o_ref[:, h * MLA_V:(h + 1) * MLA_V] = (o / l).astype(o_ref.dtype)


def _mla_attention(q, segs, grp, heads, tq, hg):
    m = q.shape[0]
    nq = grp.seq // tq
    hw = MLA_NOPE + LANES
    in_specs = [pl.BlockSpec((tq, hg * hw), lambda b, h, i: (b * nq + i, h))]
    args = [q]
    for kv, kr, rows in segs:
        in_specs.append(pl.BlockSpec((rows, hg * (MLA_NOPE + MLA_V)), lambda b, h, i: (b, h)))
        in_specs.append(pl.BlockSpec((rows, LANES), lambda b, h, i: (b, 0)))
        args += [kv, kr]
    return pl.pallas_call(
        functools.partial(_mla_kernel, n_seg=len(segs), hg=hg),
        out_shape=jax.ShapeDtypeStruct((m, heads * MLA_V), BF16),
        grid=(grp.batch, heads // hg, nq),
        in_specs=in_specs,
        out_specs=pl.BlockSpec((tq, hg * MLA_V), lambda b, h, i: (b * nq + i, h)),
        compiler_params=_params("arbitrary", "arbitrary", "arbitrary"),
        name="attn_mla",
    )(*args)


def _gqa_kernel(*refs, n_seg, group, kvg):
    q_ref = refs[0]
    o_ref = refs[-1]
    for kh in range(kvg):
        ksl = slice(kh * HEAD_DIM, (kh + 1) * HEAD_DIM)
        ks = [refs[1 + 2 * i][:, ksl].astype(BF16) for i in range(n_seg)]
        vs = [refs[2 + 2 * i][:, ksl].astype(BF16) for i in range(n_seg)]
        for g in range(group):
            sl = slice((kh * group + g) * HEAD_DIM, (kh * group + g + 1) * HEAD_DIM)
            o, l = _attend(q_ref[:, sl], ks, vs)
            o_ref[:, sl] = (o / l).astype(o_ref.dtype)


def _gqa_attention(q, segs, grp, heads, kv_heads, tq, kvg):
    m = q.shape[0]
    nq = grp.seq // tq
    group = heads // kv_heads
    gw = kvg * group * HEAD_DIM
    in_specs = [pl.BlockSpec((tq, gw), lambda b, h, i: (b * nq + i, h))]
    args = [q]
    for k, v, rows in segs:
        in_specs.append(pl.BlockSpec((rows, kvg * HEAD_DIM), lambda b, h, i: (b, h)))
        in_specs.append(pl.BlockSpec((rows, kvg * HEAD_DIM), lambda b, h, i: (b, h)))
        args += [k, v]
    return pl.pallas_call(
        functools.partial(_gqa_kernel, n_seg=len(segs), group=group, kvg=kvg),
        out_shape=jax.ShapeDtypeStruct((m, heads * HEAD_DIM), BF16),
        grid=(grp.batch, kv_heads // kvg, nq),
        in_specs=in_specs,
        out_specs=pl.BlockSpec((tq, gw), lambda b, h, i: (b * nq + i, h)),
        compiler_params=_params("arbitrary", "arbitrary", "arbitrary"),
        name="attn_gqa",
    )(*args)


def _diff_kernel(*refs, n_seg, hg, lambda_init):
    q_ref, lq1_ref, lk1_ref, lq2_ref, lk2_ref, g_ref = refs[:6]
    o_ref = refs[-1]
    lam = (jnp.exp(jnp.sum(lq1_ref[...] * lk1_ref[...], axis=-1, keepdims=True))
           - jnp.exp(jnp.sum(lq2_ref[...] * lk2_ref[...], axis=-1, keepdims=True)) + lambda_init)
    hw = 2 * HEAD_DIM
    for h in range(hg):
        vs = [refs[7 + 2 * i][:, h * hw:(h + 1) * hw].astype(BF16) for i in range(n_seg)]
        halves = []
        for j in range(2):
            sl = slice(h * hw + j * HEAD_DIM, h * hw + (j + 1) * HEAD_DIM)
            ks = [refs[6 + 2 * i][:, sl].astype(BF16) for i in range(n_seg)]
            o, l = _attend(q_ref[:, sl], ks, vs)
            halves.append(o * (1.0 / l))
        o = halves[0] - lam * halves[1]
        o_ref[:, h * hw:(h + 1) * hw] = (_rms(o, g_ref[...]) * (1.0 - lambda_init)).astype(o_ref.dtype)


def _diff_attention(q, segs, lams, g_sub, grp, heads, tq, hg, lambda_init):
    m = q.shape[0]
    nq = grp.seq // tq
    hw = 2 * HEAD_DIM
    vec = lambda w: pl.BlockSpec((1, w), lambda b, h, i: (0, 0))
    in_specs = [pl.BlockSpec((tq, hg * hw), lambda b, h, i: (b * nq + i, h))]
    in_specs += [vec(HEAD_DIM)] * 4 + [vec(hw)]
    args = [q, *lams, g_sub]
    for k, v, rows in segs:
        in_specs.append(pl.BlockSpec((rows, hg * hw), lambda b, h, i: (b, h)))
        in_specs.append(pl.BlockSpec((rows, hg * hw), lambda b, h, i: (b, h)))
        args += [k, v]
    return pl.pallas_call(
        functools.partial(_diff_kernel, n_seg=len(segs), hg=hg, lambda_init=lambda_init),
        out_shape=jax.ShapeDtypeStruct((m, heads * hw), BF16),
        grid=(grp.batch, heads // hg, nq),
        in_specs=in_specs,
        out_specs=pl.BlockSpec((tq, hg * hw), lambda b, h, i: (b * nq + i, h)),
        compiler_params=_params("arbitrary", "arbitrary", "arbitrary"),
        name="attn_diff",
    )(*args)


def _mm_res_kernel(*refs, n_a, nk, nb):
    a_refs = refs[:n_a]
    w_refs = refs[n_a:2 * n_a]
    x_ref, gt_ref, o_ref = refs[2 * n_a:2 * n_a + 3]
    seg = o_ref.shape[0] // nb
    if nk == 1:
        ws = [w_ref[...].astype(BF16) for w_ref in w_refs]
        for r in range(nb):
            rows = slice(r * seg, (r + 1) * seg)
            part = None
            for a_ref, w in zip(a_refs, ws):
                p = jnp.dot(a_ref[rows, :], w, preferred_element_type=F32)
                part = p if part is None else part + p
            o_ref[rows, :] = x_ref[rows, :] + gt_ref[r] * part
        return
    k = pl.program_id(2)

    def step(first):
        w32 = w_refs[0][...]
        for r in range(nb):
            rows = slice(r * seg, (r + 1) * seg)
            w = (w32 * gt_ref[r]).astype(BF16)
            p = jnp.dot(a_refs[0][rows, :], w, preferred_element_type=F32)
            if first:
                o_ref[rows, :] = x_ref[rows, :] + p
            else:
                o_ref[rows, :] += p

    @pl.when(k == 0)
    def _():
        step(True)

    @pl.when(k > 0)
    def _():
        step(False)


def _mm_res(a_list, w3, w_layer, x, mod, layer, chunk, grp, tk, tm=2048, tn=512):
    m, n = x.shape
    n_a = len(a_list)
    ka = a_list[0].shape[1]
    if n_a > 1:
        tk = ka
    nk = ka // tk
    ncol = n // tn
    in_specs = [pl.BlockSpec((tm, tk), lambda i, j, k: (i, k)) for _ in a_list]
    for idx in range(n_a):
        in_specs.append(pl.BlockSpec((None, tk, tn),
                                     lambda i, j, k, idx=idx: (w_layer, idx * (ka // tk) + k, j)))
    in_specs.append(pl.BlockSpec((tm, tn), lambda i, j, k: (i, j)))
    in_specs.append(grp.mod_spec(layer, lambda i, j, k: chunk * ncol + j, tn, tm))
    return pl.pallas_call(
        functools.partial(_mm_res_kernel, n_a=n_a, nk=nk, nb=grp.mod_rows(tm)),
        out_shape=jax.ShapeDtypeStruct((m, n), F32),
        grid=(m // tm, ncol, nk),
        in_specs=in_specs,
        out_specs=pl.BlockSpec((tm, tn), lambda i, j, k: (i, j)),
        compiler_params=_params("arbitrary", "arbitrary", "arbitrary"),
        name="mm_res",
    )(*a_list, *([w3] * n_a), x, mod)


def _ffn_up_kernel(x_ref, gn_ref, sc_ref, sh_ref, wg_ref, wv_ref, cg_ref, cv_ref, bg_ref, bv_ref,
                   o_ref, h_ref, *, seq):
    @pl.when(pl.program_id(1) == 0)
    def _():
        h_ref[...] = _norm_mod_tile(x_ref, gn_ref, sc_ref, sh_ref)

    h = h_ref[...]
    tm = h.shape[0]

    def taps(w_ref):
        u = jnp.dot(h, w_ref[...].astype(BF16), preferred_element_type=F32)
        return pltpu.roll(u, 1, 0), u, pltpu.roll(u, tm - 1, 0)

    def conv(t, c_ref, b_ref):
        return t[0] * c_ref[0:1, :] + t[1] * c_ref[1:2, :] + t[2] * c_ref[2:3, :] + b_ref[...]

    tg, tv = taps(wg_ref), taps(wv_ref)
    o_ref[...] = (_silu(conv(tg, cg_ref, bg_ref)) * conv(tv, cv_ref, bv_ref)).astype(o_ref.dtype)
    sub = lax.broadcasted_iota(jnp.int32, (FFN_EDGE_ROWS, 1), 0)
    for r0 in sorted({r for b in range(0, tm, seq) for r in (b, b + seq - FFN_EDGE_ROWS)}):
        rows = slice(r0, r0 + FFN_EDGE_ROWS)

        def edge(t):
            prev, cur, nxt = (a[rows, :] for a in t)
            if r0 % seq == 0:
                prev = jnp.where(sub == 0, 0.0, prev)
            if (r0 + FFN_EDGE_ROWS) % seq == 0:
                nxt = jnp.where(sub == FFN_EDGE_ROWS - 1, 0.0, nxt)
            return prev, cur, nxt

        o_ref[rows, :] = (_silu(conv(edge(tg), cg_ref, bg_ref))
                          * conv(edge(tv), cv_ref, bv_ref)).astype(o_ref.dtype)


def _ffn_up(x, norm, w_up, w_conv, b_conv, layer, grp, tm=1024, tn=256):
    m, d = x.shape
    d_ff = w_up.shape[2] // 2
    nj = d_ff // tn
    assert tm % grp.seq == 0 and m % tm == 0
    g_norm, mod, chunk_sc, chunk_sh = norm
    b3 = b_conv.reshape(b_conv.shape[0], 1, b_conv.shape[1])
    return pl.pallas_call(
        functools.partial(_ffn_up_kernel, seq=grp.seq),
        out_shape=jax.ShapeDtypeStruct((m, d_ff), BF16),
        grid=(m // tm, nj),
        in_specs=[
            pl.BlockSpec((tm, d), lambda i, j: (i, 0)),
            *grp.norm_specs(layer, chunk_sc, chunk_sh, d, tm),
            pl.BlockSpec((None, d, tn), lambda i, j: (layer, 0, j)),
            pl.BlockSpec((None, d, tn), lambda i, j: (layer, 0, nj + j)),
            pl.BlockSpec((None, w_conv.shape[1], tn), lambda i, j: (layer, 0, j)),
            pl.BlockSpec((None, w_conv.shape[1], tn), lambda i, j: (layer, 0, nj + j)),
            pl.BlockSpec((None, 1, tn), lambda i, j: (layer, 0, j)),
            pl.BlockSpec((None, 1, tn), lambda i, j: (layer, 0, nj + j)),
        ],
        out_specs=pl.BlockSpec((tm, tn), lambda i, j: (i, j)),
        scratch_shapes=[pltpu.VMEM((tm, d), BF16)],
        compiler_params=_params("arbitrary", "arbitrary"),
        name="ffn_up",
    )(x, g_norm, mod, mod, w_up, w_up, w_conv, w_conv, b3, b3)


def _rope_tables(n_tokens, rot_dim):
    n_rows = n_tokens // GRID_W
    row = jnp.repeat(jnp.arange(n_rows), GRID_W).astype(F32)
    col = jnp.tile(jnp.arange(GRID_W), n_rows).astype(F32)
    n_freq = rot_dim // 4
    freqs = ROPE_BASE ** (-jnp.arange(n_freq, dtype=F32) / n_freq)
    ang = jnp.concatenate([row[:, None] * freqs, col[:, None] * freqs], axis=-1)
    cos, sin = jnp.cos(ang), jnp.sin(ang)
    reps = LANES // rot_dim
    c = jnp.tile(jnp.concatenate([cos, cos], axis=-1), (1, reps))
    s = jnp.tile(jnp.concatenate([-sin, sin], axis=-1), (1, reps))
    return c, s


def _trunk(x, grp, mod, caches, P):
    d = x.shape[1]
    ckv_p = krope_p = kb_p = vb_p = k_c = v_c = None
    depth = P["w_ffn_up"].shape[0]
    for l in range(depth):
        i = l // 2
        norm_mix = (P["g_norm_mix"], mod, l, 1, 0)
        if l % 2 == 0:
            (q_mla, kv, ckv, krope, krope_pad, qb, kb, vb) = _inproj_ab(
                x, norm_mix, [w[i] for w in P["w_ab"]],
                P["g_mla_q"][i:i + 1], P["g_mla_kv"][i:i + 1], P["g_gqa_q"][i:i + 1],
                P["g_gqa_k"][i:i + 1], grp, P["dims_ab"])
            ckv_p, krope_p, kb_p, vb_p = ckv, krope, kb, vb
            heads = P["dims_ab"][5]
            segs_a, segs_b = [], []
            if caches is not None:
                c_ckv, c_kr, c_k, c_v = (cc[i] for cc in caches[:4])
                past = c_ckv.shape[0] // grp.batch
                segs_a.append((_kvup(c_ckv, P["w_ab"][4][i]), c_kr, past))
                segs_b.append((c_k, c_v, past))
            segs_a.append((kv, krope_pad, grp.seq))
            segs_b.append((kb, vb, grp.seq))
            tq = min(grp.seq, 512)
            small = grp.seq <= 256
            out_a = _mla_attention(q_mla, segs_a, grp, heads, tq, heads if small else 2)
            out_b = _gqa_attention(qb, segs_b, grp, qb.shape[1] // HEAD_DIM, GQA_KV_HEADS, tq,
                                   GQA_KV_HEADS if small else 1)
            x = _mm_res([out_a, out_b], P["w_out_ab"], i, x, mod, l, 2, grp, tk=None)
        else:
            w_in_c = P["w_in_c"][i:i + 1]
            nq = w_in_c.shape[2] // 3
            kv_dtype = F32 if caches is None else BF16
            q, h = _proj(x, w_in_c, 0, nq, BF16, grp, True, scale=HEAD_DIM ** -0.5 * LOG2E,
                         norm=norm_mix, tm=1024)
            k = _proj(h, w_in_c, nq, nq, kv_dtype, grp, True)
            v = _proj(h, w_in_c, 2 * nq, nq, kv_dtype, grp, False)
            k_c, v_c = k, v
            segs = []
            if caches is not None:
                c_k, c_v = (cc[i] for cc in caches[4:])
                segs.append((c_k, c_v, c_k.shape[0] // grp.batch))
            segs.append((k, v, grp.seq))
            lambda_init = 0.8 - 0.6 * math.exp(-0.3 * l)
            lams = [P[n][i:i + 1] for n in ("lambda_q1", "lambda_k1", "lambda_q2", "lambda_k2")]
            dheads = nq // (2 * HEAD_DIM)
            o = _diff_attention(q, segs, lams, P["g_diff_sub"][i:i + 1], grp, dheads,
                                min(grp.seq, 512), dheads if grp.seq <= 256 else 1, lambda_init)
            x = _mm_res([o], P["w_out_c"], i, x, mod, l, 2, grp, tk=o.shape[1])
        g = _ffn_up(x, (P["g_norm_ffn"], mod, 4, 3), P["w_ffn_up"], P["w_ffn_conv"],
                    P["b_ffn_conv"], l, grp)
        x = _mm_res([g], P["w_ffn_down"], l, x, mod, l, 5, grp, tk=512, tn=1024)
    y = _final_norm(x, P["g_final"])
    return y, (ckv_p, krope_p, kb_p, vb_p), (k_c, v_c)


def kernel(x_prompt, x_sample, cache_mla_ckv, cache_mla_krope, cache_gqa_k, cache_gqa_v,
           cache_diff_k, cache_diff_v, c, c_ctx, w_ada, b_ada, g_norm_mix, g_norm_ffn,
           w_in_ab, g_mla_q, w_mla_q_up, g_mla_kv, w_mla_kv_up, g_gqa_q, g_gqa_k, w_out_ab,
           w_in_c, lambda_q1, lambda_k1, lambda_q2, lambda_k2, g_diff_sub, w_out_c,
           w_ffn_up, w_ffn_conv, b_ffn_conv, w_ffn_down, g_final):
    bp, tp, d = x_prompt.shape
    bs, ts, _ = x_sample.shape
    depth = w_ada.shape[0]
    n_even = w_in_ab.shape[0]
    assert depth == 2 and n_even == 1 and w_in_c.shape[0] == 1, "one MLA/GQA layer + one diff layer"
    past = cache_mla_ckv.shape[2]
    n_qlat = g_mla_q.shape[1]
    n_ckv = g_mla_kv.shape[1]
    mla_heads = w_mla_q_up.shape[2] // (MLA_NOPE + MLA_ROPE)
    n_kb = cache_gqa_k.shape[3] * HEAD_DIM
    n_vb = n_kb
    n_qb = w_in_ab.shape[2] - n_qlat - n_ckv - MLA_ROPE - n_kb - n_vb

    rows = 8
    cond = jnp.zeros((rows, d), F32).at[:bs].set(c).at[bs].set(c_ctx)
    mod = _modulation(cond, w_ada, b_ada)
    mod = mod.reshape(depth, rows, 1, mod.shape[2])

    o0 = n_qlat + n_ckv
    o1 = o0 + MLA_ROPE
    w_a = w_in_ab[:, :, :o0].astype(BF16)
    w_b = w_in_ab[:, :, o1:].astype(BF16)
    w_k = jnp.pad(w_in_ab[:, :, o0:o1], ((0, 0), (0, 0), (0, LANES - MLA_ROPE))).astype(BF16)
    wq = w_mla_q_up.reshape(n_even, n_qlat, mla_heads, MLA_NOPE + MLA_ROPE)
    wq = wq * ((MLA_NOPE + MLA_ROPE) ** -0.5 * LOG2E)
    w_q_up_p = jnp.pad(wq, ((0, 0), (0, 0), (0, 0), (0, LANES - MLA_ROPE))).reshape(
        n_even, n_qlat, mla_heads * (MLA_NOPE + LANES)).astype(BF16)
    P = dict(
        w_ab=(w_a, w_b, w_k, w_q_up_p, w_mla_kv_up.astype(BF16)),
        g_mla_q=g_mla_q, g_mla_kv=g_mla_kv, g_gqa_q=g_gqa_q * (HEAD_DIM ** -0.5 * LOG2E),
        g_gqa_k=g_gqa_k,
        w_out_ab=w_out_ab, w_in_c=w_in_c, lambda_q1=lambda_q1, lambda_k1=lambda_k1,
        lambda_q2=lambda_q2, lambda_k2=lambda_k2, g_diff_sub=g_diff_sub, w_out_c=w_out_c,
        w_ffn_up=w_ffn_up, w_ffn_conv=w_ffn_conv, b_ffn_conv=b_ffn_conv, w_ffn_down=w_ffn_down,
        g_final=g_final,
        g_norm_mix=g_norm_mix.reshape(depth, 1, d), g_norm_ffn=g_norm_ffn.reshape(depth, 1, d),
        dims_ab=(n_qlat, n_ckv, n_qb, n_kb, n_vb, mla_heads),
    )

    grp_p = _Group(bp, tp, bs, False, None)
    y_p, ab_p, c_p = _trunk(x_prompt.reshape(bp * tp, d), grp_p, mod, None, P)

    c128, s128 = _rope_tables(ts, HEAD_DIM)
    c64, s64 = _rope_tables(ts, MLA_ROPE)
    grp_s = _Group(bs, ts, 0, True, dict(c128=c128, s128=s128, c64=c64, s64=s64))
    n_odd = cache_diff_k.shape[1]
    kr_pad = jnp.pad(cache_mla_krope, ((0, 0), (0, 0), (0, 0), (0, LANES - MLA_ROPE))).astype(BF16)
    per_layer = lambda a, n: [a[:, j].reshape(bs * past, -1) for j in range(n)]
    caches = (per_layer(cache_mla_ckv, n_even), per_layer(kr_pad, n_even),
              per_layer(cache_gqa_k, n_even), per_layer(cache_gqa_v, n_even),
              per_layer(cache_diff_k.astype(BF16), n_odd),
              per_layer(cache_diff_v.astype(BF16), n_odd))
    y_s, _, _ = _trunk(x_sample.reshape(bs * ts, d), grp_s, mod, caches, P)

    ckv, krope, kb, vb = ab_p
    k_c, v_c = c_p
    kvh = n_kb // HEAD_DIM
    dh = k_c.shape[1] // (2 * HEAD_DIM)
    return (y_p.reshape(bp, tp, d), y_s.reshape(bs, ts, d),
            ckv.reshape(bp, 1, tp, n_ckv), krope.reshape(bp, 1, tp, MLA_ROPE),
            kb.reshape(bp, 1, tp, kvh, HEAD_DIM), vb.reshape(bp, 1, tp, kvh, HEAD_DIM),
            k_c.reshape(bp, 1, tp, dh, 2, HEAD_DIM), v_c.reshape(bp, 1, tp, dh, 2 * HEAD_DIM))
```

```python
import functools
import math

import jax
import jax.numpy as jnp
from jax import lax
from jax.experimental import pallas as pl
from jax.experimental.pallas import tpu as pltpu

F32 = jnp.float32
BF16 = jnp.bfloat16

HEAD_DIM = 128
GRID_W = 64
ROPE_BASE = 10000.0
EPS = 1e-6
MLA_NOPE = 128
MLA_ROPE = 64
MLA_V = 128
GQA_KV_HEADS = 2
LOG2E = 1.4426950408889634
LANES = 128
FFN_EDGE_ROWS = 16
NORM_ROWS = 64
VMEM_LIMIT = 56 * 1024 * 1024


def _params(*sem):
    return pltpu.CompilerParams(dimension_semantics=sem, vmem_limit_bytes=VMEM_LIMIT)


def _rms(x, g):
    ms = jnp.mean(x * x, axis=-1, keepdims=True)
    return x * lax.rsqrt(ms + EPS) * g


def _silu(x):
    return x / (1.0 + jnp.exp(-x))


def _rope_half(x, c, s, half):
    if 2 * half == LANES:
        sw = pltpu.roll(x, half, 1)
    else:
        lane = lax.broadcasted_iota(jnp.int32, x.shape, 1)
        first = (lane % (2 * half)) < half
        sw = jnp.where(first, pltpu.roll(x, LANES - half, 1), pltpu.roll(x, half, 1))
    return x * c + sw * s


def _mod_kernel(cond_ref, w_ref, b_ref, o_ref):
    s = _silu(cond_ref[...]).astype(BF16)
    o_ref[...] = jnp.dot(s, w_ref[...].astype(BF16), preferred_element_type=F32) + b_ref[...]


def _modulation(cond, w_ada, b_ada, tn=1024):
    depth, d, n = w_ada.shape
    rows = cond.shape[0]
    return pl.pallas_call(
        _mod_kernel,
        out_shape=jax.ShapeDtypeStruct((depth, rows, n), F32),
        grid=(depth, n // tn),
        in_specs=[
            pl.BlockSpec((rows, d), lambda l, j: (0, 0)),
            pl.BlockSpec((None, d, tn), lambda l, j: (l, 0, j)),
            pl.BlockSpec((None, 1, tn), lambda l, j: (l, 0, j)),
        ],
        out_specs=pl.BlockSpec((None, rows, tn), lambda l, j: (l, 0, j)),
        compiler_params=_params("arbitrary", "arbitrary"),
        name="adaln_mod",
    )(cond, w_ada, b_ada.reshape(depth, 1, n))


def _norm_mod_tile(x_ref, g_ref, sc_ref, sh_ref):
    y = _rms(x_ref[...], g_ref[...])
    return (y * (1.0 + sc_ref[...]) + sh_ref[...]).astype(BF16)


def _norm_mod_into(h_ref, x_ref, g_ref, sc_ref, sh_ref):
    g, sc1, sh = g_ref[...], 1.0 + sc_ref[...], sh_ref[...]
    for r0 in range(0, x_ref.shape[0], NORM_ROWS):
        rows = slice(r0, r0 + NORM_ROWS)
        h_ref[rows, :] = (_rms(x_ref[rows, :], g) * sc1 + sh).astype(BF16)


def _norm_kernel(x_ref, g_ref, o_ref):
    o_ref[...] = _rms(x_ref[...], g_ref[...]).astype(o_ref.dtype)


class _Group:
    def __init__(self, batch, seq, mod_row0, per_batch_mod, rope):
        self.batch = batch
        self.seq = seq
        self.m = batch * seq
        self.mod_row0 = mod_row0
        self.per_batch_mod = per_batch_mod
        self.rope = rope

    def mod_rows(self, tm):
        if self.per_batch_mod and tm > self.seq:
            assert tm % self.seq == 0 and self.mod_row0 % (tm // self.seq) == 0
            return tm // self.seq
        return 1

    def mod_spec(self, layer, col, width, tm):
        nb = self.mod_rows(tm)
        if self.per_batch_mod:
            assert tm % self.seq == 0 or self.seq % tm == 0
            row = lambda i: (self.mod_row0 + (i * tm) // self.seq) // nb
        else:
            row = lambda i: self.mod_row0
        col_fn = col if callable(col) else (lambda *idx: col)
        return pl.BlockSpec((None, nb, 1, width),
                            lambda *idx: (layer, row(idx[0]), 0, col_fn(*idx)))

    def norm_specs(self, layer, chunk_sc, chunk_sh, d, tm):
        assert self.mod_rows(tm) == 1
        vec = lambda chunk: pl.BlockSpec(
            (None, None, 1, d), self.mod_spec(layer, chunk, d, tm).index_map)
        return [pl.BlockSpec((None, 1, d), lambda *idx: (layer, 0, 0)), vec(chunk_sc), vec(chunk_sh)]


def _final_norm(x, g, tm=512):
    m, d = x.shape
    return pl.pallas_call(
        _norm_kernel,
        out_shape=jax.ShapeDtypeStruct((m, d), F32),
        grid=(m // tm,),
        in_specs=[pl.BlockSpec((tm, d), lambda i: (i, 0)),
                  pl.BlockSpec((1, d), lambda i: (0, 0))],
        out_specs=pl.BlockSpec((tm, d), lambda i: (i, 0)),
        compiler_params=_params("arbitrary"),
        name="final_norm",
    )(x, g.reshape(1, d))


def _inproj_ab_kernel(*refs, rope, n_qlat, n_ckv, n_qb, n_kb, n_vb, mla_heads):
    (x_ref, gn_ref, sc_ref, sh_ref, wa_ref, wb_ref, wk_ref, wq_ref, wkv_ref,
     gq_ref, gkv_ref, gqn_ref, gkn_ref) = refs[:13]
    (qmla_ref, kv_ref, ckv_ref, kr_ref, krp_ref, qb_ref, kb_ref, vb_ref) = refs[-8:]
    if rope:
        c128, s128, c64, s64 = (r[...] for r in refs[13:17])
    h = _norm_mod_tile(x_ref, gn_ref, sc_ref, sh_ref)

    def seg(w_ref, a, b):
        return jnp.dot(h, w_ref[:, a:b], preferred_element_type=F32)

    qn = _rms(seg(wa_ref, 0, n_qlat), gq_ref[...]).astype(BF16)
    qa = jnp.dot(qn, wq_ref[...], preferred_element_type=F32)
    hw = MLA_NOPE + LANES
    for i in range(mla_heads):
        qmla_ref[:, i * hw:i * hw + MLA_NOPE] = qa[:, i * hw:i * hw + MLA_NOPE].astype(BF16)
        rp = qa[:, i * hw + MLA_NOPE:(i + 1) * hw]
        if rope:
            rp = _rope_half(rp, c64, s64, MLA_ROPE // 2)
        qmla_ref[:, i * hw + MLA_NOPE:(i + 1) * hw] = rp.astype(BF16)
    cn = _rms(seg(wa_ref, n_qlat, n_qlat + n_ckv), gkv_ref[...])
    ckv_ref[...] = cn
    kv_ref[...] = jnp.dot(cn.astype(BF16), wkv_ref[...], preferred_element_type=F32).astype(BF16)
    r = seg(wb_ref, 0, n_qb)
    for i in range(n_qb // HEAD_DIM):
        y = _rms(r[:, i * HEAD_DIM:(i + 1) * HEAD_DIM], gqn_ref[...])
        if rope:
            y = _rope_half(y, c128, s128, HEAD_DIM // 2)
        qb_ref[:, i * HEAD_DIM:(i + 1) * HEAD_DIM] = y.astype(BF16)
    r = seg(wb_ref, n_qb, n_qb + n_kb)
    for i in range(n_kb // HEAD_DIM):
        y = _rms(r[:, i * HEAD_DIM:(i + 1) * HEAD_DIM], gkn_ref[...])
        if rope:
            y = _rope_half(y, c128, s128, HEAD_DIM // 2)
        kb_ref[:, i * HEAD_DIM:(i + 1) * HEAD_DIM] = y
    vb_ref[...] = seg(wb_ref, n_qb + n_kb, n_qb + n_kb + n_vb)
    r = seg(wk_ref, 0, LANES)
    if rope:
        r = _rope_half(r, c64, s64, MLA_ROPE // 2)
    kr_ref[...] = r[:, :MLA_ROPE]
    krp_ref[...] = r.astype(BF16)


def _inproj_ab(x, norm, weights, g_q, g_kv, g_qn, g_kn, grp, dims, tm=256):
    m, d = x.shape
    n_qlat, n_ckv, n_qb, n_kb, n_vb, mla_heads = dims
    g_norm, mod, layer, chunk_sc, chunk_sh = norm
    rope = grp.rope is not None
    one = pl.Buffered(1)
    nq_out = mla_heads * (MLA_NOPE + LANES)
    nkv_out = weights[4].shape[1]
    in_specs = [pl.BlockSpec((tm, d), lambda i: (i, 0))]
    in_specs += grp.norm_specs(layer, chunk_sc, chunk_sh, d, tm)
    in_specs += [pl.BlockSpec(w.shape, lambda i: (0, 0), pipeline_mode=one) for w in weights]
    in_specs += [
        pl.BlockSpec((1, n_qlat), lambda i: (0, 0)),
        pl.BlockSpec((1, n_ckv), lambda i: (0, 0)),
        pl.BlockSpec((1, HEAD_DIM), lambda i: (0, 0)),
        pl.BlockSpec((1, HEAD_DIM), lambda i: (0, 0)),
    ]
    args = [x, g_norm, mod, mod, *weights, g_q, g_kv, g_qn, g_kn]
    if rope:
        per = grp.seq // tm
        for name in ("c128", "s128", "c64", "s64"):
            in_specs.append(pl.BlockSpec((tm, LANES), lambda i: (i % per, 0)))
            args.append(grp.rope[name])
    widths = [(nq_out, BF16), (nkv_out, BF16), (n_ckv, F32), (MLA_ROPE, F32), (LANES, BF16),
              (n_qb, BF16), (n_kb, F32), (n_vb, F32)]
    return pl.pallas_call(
        functools.partial(_inproj_ab_kernel, rope=rope, n_qlat=n_qlat, n_ckv=n_ckv, n_qb=n_qb,
                          n_kb=n_kb, n_vb=n_vb, mla_heads=mla_heads),
        out_shape=[jax.ShapeDtypeStruct((m, w), dt) for w, dt in widths],
        grid=(m // tm,),
        in_specs=in_specs,
        out_specs=[pl.BlockSpec((tm, w), lambda i: (i, 0)) for w, _ in widths],
        compiler_params=_params("arbitrary"),
        name="inproj_ab",
    )(*args)


def _kvup_kernel(c_ref, w_ref, o_ref):
    o_ref[...] = jnp.dot(c_ref[...].astype(BF16), w_ref[...],
                         preferred_element_type=F32).astype(o_ref.dtype)


def _kvup(ckv, wkv, tm=512):
    m, k = ckv.shape
    n = wkv.shape[1]
    return pl.pallas_call(
        _kvup_kernel,
        out_shape=jax.ShapeDtypeStruct((m, n), BF16),
        grid=(m // tm,),
        in_specs=[pl.BlockSpec((tm, k), lambda i: (i, 0)),
                  pl.BlockSpec((k, n), lambda i: (0, 0))],
        out_specs=pl.BlockSpec((tm, n), lambda i: (i, 0)),
        compiler_params=_params("arbitrary"),
        name="mla_kv_up",
    )(ckv, wkv)


def _proj_kernel(*refs, rope, scale, norm):
    if norm:
        x_ref, gn_ref, sc_ref, sh_ref, *rest = refs
        h_ref = rest.pop()

        @pl.when(pl.program_id(1) == 0)
        def _():
            _norm_mod_into(h_ref, x_ref, gn_ref, sc_ref, sh_ref)
    else:
        h_ref, *rest = refs
    if rope:
        w_ref, c_ref, s_ref, o_ref = rest
    else:
        w_ref, o_ref = rest
    r = jnp.dot(h_ref[...], w_ref[...].astype(BF16), preferred_element_type=F32)
    if rope:
        c, s = c_ref[...], s_ref[...]
        for i in range(r.shape[1] // HEAD_DIM):
            sl = slice(i * HEAD_DIM, (i + 1) * HEAD_DIM)
            y = _rope_half(r[:, sl], c, s, HEAD_DIM // 2)
            o_ref[:, sl] = (y if scale is None else y * scale).astype(o_ref.dtype)
    else:
        o_ref[...] = (r if scale is None else r * scale).astype(o_ref.dtype)


def _proj(h, w3, col0, ncols, out_dtype, grp, use_rope, scale=None, norm=None, tm=2048, tn=512):
    m, d = h.shape
    rope = use_rope and grp.rope is not None
    tm = min(tm, m)
    off = col0 // tn
    in_specs = [pl.BlockSpec((tm, d), lambda i, j: (i, 0))]
    args = [h]
    out_shape = [jax.ShapeDtypeStruct((m, ncols), out_dtype)]
    out_specs = [pl.BlockSpec((tm, tn), lambda i, j: (i, j))]
    if norm is not None:
        g_norm, mod, layer, chunk_sc, chunk_sh = norm
        in_specs += grp.norm_specs(layer, chunk_sc, chunk_sh, d, tm)
        args += [g_norm, mod, mod]
        out_shape.append(jax.ShapeDtypeStruct((m, d), BF16))
        out_specs.append(pl.BlockSpec((tm, d), lambda i, j: (i, 0)))
    in_specs.append(pl.BlockSpec((None, d, tn), lambda i, j: (0, 0, off + j)))
    args.append(w3)
    if rope:
        assert tm % grp.seq == 0
        for name in ("c128", "s128"):
            in_specs.append(pl.BlockSpec((tm, LANES), lambda i, j: (0, 0)))
            args.append(jnp.tile(grp.rope[name], (tm // grp.seq, 1)))
    res = pl.pallas_call(
        functools.partial(_proj_kernel, rope=rope, scale=scale, norm=norm is not None),
        out_shape=out_shape,
        grid=(m // tm, ncols // tn),
        in_specs=in_specs,
        out_specs=out_specs,
        compiler_params=_params("arbitrary", "arbitrary"),
        name="proj_c",
    )(*args)
    return res if norm is not None else res[0]


def _softmax_parts(scores):
    mx = jnp.max(scores[0], axis=-1, keepdims=True)
    for s in scores[1:]:
        mx = jnp.maximum(mx, jnp.max(s, axis=-1, keepdims=True))
    ps = [jnp.exp2(s - mx) for s in scores]
    l = jnp.sum(ps[0], axis=-1, keepdims=True)
    for p in ps[1:]:
        l = l + jnp.sum(p, axis=-1, keepdims=True)
    return ps, l


def _attend(q, ks, vs):
    scores = [lax.dot_general(q, k, _NT, preferred_element_type=F32) for k in ks]
    ps, l = _softmax_parts(scores)
    o = jnp.dot(ps[0].astype(BF16), vs[0], preferred_element_type=F32)
    for p, v in zip(ps[1:], vs[1:]):
        o = o + jnp.dot(p.astype(BF16), v, preferred_element_type=F32)
    return o, l


_NT = (((1,), (1,)), ((), ()))


def _mla_kernel(*refs, n_seg, hg):
    q_ref = refs[0]
    o_ref = refs[-1]
    hw = MLA_NOPE + LANES
    for h in range(hg):
        ks, vs = [], []
        for i in range(n_seg):
            kv_ref, kr_ref = refs[1 + 2 * i], refs[2 + 2 * i]
            ks.append(jnp.concatenate([kv_ref[:, h * hw:h * hw + MLA_NOPE], kr_ref[...]], axis=1))
            vs.append(kv_ref[:, h * hw + MLA_NOPE:(h + 1) * hw])
        o, l = _attend(q_ref[:, h * hw:(h + 1) * hw], ks, vs)
        o_ref[:, h * MLA_V:(h + 1) * MLA_V] = (o / l).astype(o_ref.dtype)


def _mla_attention(q, segs, grp, heads, tq, hg):
    m = q.shape[0]
    nq = grp.seq // tq
    hw = MLA_NOPE + LANES
    in_specs = [pl.BlockSpec((tq, hg * hw), lambda b, h, i: (b * nq + i, h))]
    args = [q]
    for kv, kr, rows in segs:
        in_specs.append(pl.BlockSpec((rows, hg * (MLA_NOPE + MLA_V)), lambda b, h, i: (b, h)))
        in_specs.append(pl.BlockSpec((rows, LANES), lambda b, h, i: (b, 0)))
        args += [kv, kr]
    return pl.pallas_call(
        functools.partial(_mla_kernel, n_seg=len(segs), hg=hg),
        out_shape=jax.ShapeDtypeStruct((m, heads * MLA_V), BF16),
        grid=(grp.batch, heads // hg, nq),
        in_specs=in_specs,
        out_specs=pl.BlockSpec((tq, hg * MLA_V), lambda b, h, i: (b * nq + i, h)),
        compiler_params=_params("arbitrary", "arbitrary", "arbitrary"),
        name="attn_mla",
    )(*args)


def _gqa_kernel(*refs, n_seg, group, kvg):
    q_ref = refs[0]
    o_ref = refs[-1]
    for kh in range(kvg):
        ksl = slice(kh * HEAD_DIM, (kh + 1) * HEAD_DIM)
        ks = [refs[1 + 2 * i][:, ksl].astype(BF16) for i in range(n_seg)]
        vs = [refs[2 + 2 * i][:, ksl].astype(BF16) for i in range(n_seg)]
        for g in range(group):
            sl = slice((kh * group + g) * HEAD_DIM, (kh * group + g + 1) * HEAD_DIM)
            o, l = _attend(q_ref[:, sl], ks, vs)
            o_ref[:, sl] = (o / l).astype(o_ref.dtype)


def _gqa_attention(q, segs, grp, heads, kv_heads, tq, kvg):
    m = q.shape[0]
    nq = grp.seq // tq
    group = heads // kv_heads
    gw = kvg * group * HEAD_DIM
    in_specs = [pl.BlockSpec((tq, gw), lambda b, h, i: (b * nq + i, h))]
    args = [q]
    for k, v, rows in segs:
        in_specs.append(pl.BlockSpec((rows, kvg * HEAD_DIM), lambda b, h, i: (b, h)))
        in_specs.append(pl.BlockSpec((rows, kvg * HEAD_DIM), lambda b, h, i: (b, h)))
        args += [k, v]
    return pl.pallas_call(
        functools.partial(_gqa_kernel, n_seg=len(segs), group=group, kvg=kvg),
        out_shape=jax.ShapeDtypeStruct((m, heads * HEAD_DIM), BF16),
        grid=(grp.batch, kv_heads // kvg, nq),
        in_specs=in_specs,
        out_specs=pl.BlockSpec((tq, gw), lambda b, h, i: (b * nq + i, h)),
        compiler_params=_params("arbitrary", "arbitrary", "arbitrary"),
        name="attn_gqa",
    )(*args)


def _diff_kernel(*refs, n_seg, hg, lambda_init):
    q_ref, lq1_ref, lk1_ref, lq2_ref, lk2_ref, g_ref = refs[:6]
    o_ref = refs[-1]
    lam = (jnp.exp(jnp.sum(lq1_ref[...] * lk1_ref[...], axis=-1, keepdims=True))
           - jnp.exp(jnp.sum(lq2_ref[...] * lk2_ref[...], axis=-1, keepdims=True)) + lambda_init)
    hw = 2 * HEAD_DIM
    for h in range(hg):
        vs = [refs[7 + 2 * i][:, h * hw:(h + 1) * hw].astype(BF16) for i in range(n_seg)]
        halves = []
        for j in range(2):
            sl = slice(h * hw + j * HEAD_DIM, h * hw + (j + 1) * HEAD_DIM)
            ks = [refs[6 + 2 * i][:, sl].astype(BF16) for i in range(n_seg)]
            o, l = _attend(q_ref[:, sl], ks, vs)
            halves.append(o * (1.0 / l))
        o = halves[0] - lam * halves[1]
        o_ref[:, h * hw:(h + 1) * hw] = (_rms(o, g_ref[...]) * (1.0 - lambda_init)).astype(o_ref.dtype)


def _diff_attention(q, segs, lams, g_sub, grp, heads, tq, hg, lambda_init):
    m = q.shape[0]
    nq = grp.seq // tq
    hw = 2 * HEAD_DIM
    vec = lambda w: pl.BlockSpec((1, w), lambda b, h, i: (0, 0))
    in_specs = [pl.BlockSpec((tq, hg * hw), lambda b, h, i: (b * nq + i, h))]
    in_specs += [vec(HEAD_DIM)] * 4 + [vec(hw)]
    args = [q, *lams, g_sub]
    for k, v, rows in segs:
        in_specs.append(pl.BlockSpec((rows, hg * hw), lambda b, h, i: (b, h)))
        in_specs.append(pl.BlockSpec((rows, hg * hw), lambda b, h, i: (b, h)))
        args += [k, v]
    return pl.pallas_call(
        functools.partial(_diff_kernel, n_seg=len(segs), hg=hg, lambda_init=lambda_init),
        out_shape=jax.ShapeDtypeStruct((m, heads * hw), BF16),
        grid=(grp.batch, heads // hg, nq),
        in_specs=in_specs,
        out_specs=pl.BlockSpec((tq, hg * hw), lambda b, h, i: (b * nq + i, h)),
        compiler_params=_params("arbitrary", "arbitrary", "arbitrary"),
        name="attn_diff",
    )(*args)


def _mm_res_kernel(*refs, n_a, nk, nb):
    a_refs = refs[:n_a]
    w_refs = refs[n_a:2 * n_a]
    x_ref, gt_ref, o_ref = refs[2 * n_a:2 * n_a + 3]
    seg = o_ref.shape[0] // nb
    if nk == 1:
        ws = [w_ref[...].astype(BF16) for w_ref in w_refs]
        for r in range(nb):
            rows = slice(r * seg, (r + 1) * seg)
            part = None
            for a_ref, w in zip(a_refs, ws):
                p = jnp.dot(a_ref[rows, :], w, preferred_element_type=F32)
                part = p if part is None else part + p
            o_ref[rows, :] = x_ref[rows, :] + gt_ref[r] * part
        return
    k = pl.program_id(2)

    def step(first):
        w32 = w_refs[0][...]
        for r in range(nb):
            rows = slice(r * seg, (r + 1) * seg)
            w = (w32 * gt_ref[r]).astype(BF16)
            p = jnp.dot(a_refs[0][rows, :], w, preferred_element_type=F32)
            if first:
                o_ref[rows, :] = x_ref[rows, :] + p
            else:
                o_ref[rows, :] += p

    @pl.when(k == 0)
    def _():
        step(True)

    @pl.when(k > 0)
    def _():
        step(False)


def _mm_res(a_list, w3, w_layer, x, mod, layer, chunk, grp, tk, tm=2048, tn=512):
    m, n = x.shape
    n_a = len(a_list)
    ka = a_list[0].shape[1]
    if n_a > 1:
        tk = ka
    nk = ka // tk
    ncol = n // tn
    in_specs = [pl.BlockSpec((tm, tk), lambda i, j, k: (i, k)) for _ in a_list]
    for idx in range(n_a):
        in_specs.append(pl.BlockSpec((None, tk, tn),
                                     lambda i, j, k, idx=idx: (w_layer, idx * (ka // tk) + k, j)))
    in_specs.append(pl.BlockSpec((tm, tn), lambda i, j, k: (i, j)))
    in_specs.append(grp.mod_spec(layer, lambda i, j, k: chunk * ncol + j, tn, tm))
    return pl.pallas_call(
        functools.partial(_mm_res_kernel, n_a=n_a, nk=nk, nb=grp.mod_rows(tm)),
        out_shape=jax.ShapeDtypeStruct((m, n), F32),
        grid=(m // tm, ncol, nk),
        in_specs=in_specs,
        out_specs=pl.BlockSpec((tm, tn), lambda i, j, k: (i, j)),
        compiler_params=_params("arbitrary", "arbitrary", "arbitrary"),
        name="mm_res",
    )(*a_list, *([w3] * n_a), x, mod)


def _ffn_up_kernel(x_ref, gn_ref, sc_ref, sh_ref, wg_ref, wv_ref, cg_ref, cv_ref, bg_ref, bv_ref,
                   o_ref, h_ref, *, seq):
    @pl.when(pl.program_id(1) == 0)
    def _():
        _norm_mod_into(h_ref, x_ref, gn_ref, sc_ref, sh_ref)

    h = h_ref[...]
    tm = h.shape[0]

    def taps(w_ref):
        u = jnp.dot(h, w_ref[...].astype(BF16), preferred_element_type=F32)
        return pltpu.roll(u, 1, 0), u, pltpu.roll(u, tm - 1, 0)

    def conv(t, c_ref, b_ref):
        return t[0] * c_ref[0:1, :] + t[1] * c_ref[1:2, :] + t[2] * c_ref[2:3, :] + b_ref[...]

    tg, tv = taps(wg_ref), taps(wv_ref)
    o_ref[...] = (_silu(conv(tg, cg_ref, bg_ref)) * conv(tv, cv_ref, bv_ref)).astype(o_ref.dtype)
    sub = lax.broadcasted_iota(jnp.int32, (FFN_EDGE_ROWS, 1), 0)
    for r0 in sorted({r for b in range(0, tm, seq) for r in (b, b + seq - FFN_EDGE_ROWS)}):
        rows = slice(r0, r0 + FFN_EDGE_ROWS)

        def edge(t):
            prev, cur, nxt = (a[rows, :] for a in t)
            if r0 % seq == 0:
                prev = jnp.where(sub == 0, 0.0, prev)
            if (r0 + FFN_EDGE_ROWS) % seq == 0:
                nxt = jnp.where(sub == FFN_EDGE_ROWS - 1, 0.0, nxt)
            return prev, cur, nxt

        o_ref[rows, :] = (_silu(conv(edge(tg), cg_ref, bg_ref))
                          * conv(edge(tv), cv_ref, bv_ref)).astype(o_ref.dtype)


def _ffn_up(x, norm, w_up, w_conv, b_conv, layer, grp, tm=1024, tn=256):
    m, d = x.shape
    d_ff = w_up.shape[2] // 2
    nj = d_ff // tn
    assert tm % grp.seq == 0 and m % tm == 0
    g_norm, mod, chunk_sc, chunk_sh = norm
    b3 = b_conv.reshape(b_conv.shape[0], 1, b_conv.shape[1])
    return pl.pallas_call(
        functools.partial(_ffn_up_kernel, seq=grp.seq),
        out_shape=jax.ShapeDtypeStruct((m, d_ff), BF16),
        grid=(m // tm, nj),
        in_specs=[
            pl.BlockSpec((tm, d), lambda i, j: (i, 0)),
            *grp.norm_specs(layer, chunk_sc, chunk_sh, d, tm),
            pl.BlockSpec((None, d, tn), lambda i, j: (layer, 0, j)),
            pl.BlockSpec((None, d, tn), lambda i, j: (layer, 0, nj + j)),
            pl.BlockSpec((None, w_conv.shape[1], tn), lambda i, j: (layer, 0, j)),
            pl.BlockSpec((None, w_conv.shape[1], tn), lambda i, j: (layer, 0, nj + j)),
            pl.BlockSpec((None, 1, tn), lambda i, j: (layer, 0, j)),
            pl.BlockSpec((None, 1, tn), lambda i, j: (layer, 0, nj + j)),
        ],
        out_specs=pl.BlockSpec((tm, tn), lambda i, j: (i, j)),
        scratch_shapes=[pltpu.VMEM((tm, d), BF16)],
        compiler_params=_params("arbitrary", "arbitrary"),
        name="ffn_up",
    )(x, g_norm, mod, mod, w_up, w_up, w_conv, w_conv, b3, b3)


def _rope_tables(n_tokens, rot_dim):
    n_rows = n_tokens // GRID_W
    row = jnp.repeat(jnp.arange(n_rows), GRID_W).astype(F32)
    col = jnp.tile(jnp.arange(GRID_W), n_rows).astype(F32)
    n_freq = rot_dim // 4
    freqs = ROPE_BASE ** (-jnp.arange(n_freq, dtype=F32) / n_freq)
    ang = jnp.concatenate([row[:, None] * freqs, col[:, None] * freqs], axis=-1)
    cos, sin = jnp.cos(ang), jnp.sin(ang)
    reps = LANES // rot_dim
    c = jnp.tile(jnp.concatenate([cos, cos], axis=-1), (1, reps))
    s = jnp.tile(jnp.concatenate([-sin, sin], axis=-1), (1, reps))
    return c, s


def _trunk(x, grp, mod, caches, P):
    d = x.shape[1]
    ckv_p = krope_p = kb_p = vb_p = k_c = v_c = None
    depth = P["w_ffn_up"].shape[0]
    for l in range(depth):
        i = l // 2
        norm_mix = (P["g_norm_mix"], mod, l, 1, 0)
        if l % 2 == 0:
            (q_mla, kv, ckv, krope, krope_pad, qb, kb, vb) = _inproj_ab(
                x, norm_mix, [w[i] for w in P["w_ab"]],
                P["g_mla_q"][i:i + 1], P["g_mla_kv"][i:i + 1], P["g_gqa_q"][i:i + 1],
                P["g_gqa_k"][i:i + 1], grp, P["dims_ab"])
            ckv_p, krope_p, kb_p, vb_p = ckv, krope, kb, vb
            heads = P["dims_ab"][5]
            segs_a, segs_b = [], []
            if caches is not None:
                c_ckv, c_kr, c_k, c_v = (cc[i] for cc in caches[:4])
                past = c_ckv.shape[0] // grp.batch
                segs_a.append((_kvup(c_ckv, P["w_ab"][4][i]), c_kr, past))
                segs_b.append((c_k, c_v, past))
            segs_a.append((kv, krope_pad, grp.seq))
            segs_b.append((kb, vb, grp.seq))
            tq = min(grp.seq, 512)
            out_a = _mla_attention(q_mla, segs_a, grp, heads, tq, heads)
            out_b = _gqa_attention(qb, segs_b, grp, qb.shape[1] // HEAD_DIM, GQA_KV_HEADS, tq,
                                   GQA_KV_HEADS)
            x = _mm_res([out_a, out_b], P["w_out_ab"], i, x, mod, l, 2, grp, tk=None)
        else:
            w_in_c = P["w_in_c"][i:i + 1]
            nq = w_in_c.shape[2] // 3
            kv_dtype = F32 if caches is None else BF16
            q, h = _proj(x, w_in_c, 0, nq, BF16, grp, True, scale=HEAD_DIM ** -0.5 * LOG2E,
                         norm=norm_mix, tm=1024)
            k = _proj(h, w_in_c, nq, nq, kv_dtype, grp, True)
            v = _proj(h, w_in_c, 2 * nq, nq, kv_dtype, grp, False)
            k_c, v_c = k, v
            segs = []
            if caches is not None:
                c_k, c_v = (cc[i] for cc in caches[4:])
                segs.append((c_k, c_v, c_k.shape[0] // grp.batch))
            segs.append((k, v, grp.seq))
            lambda_init = 0.8 - 0.6 * math.exp(-0.3 * l)
            lams = [P[n][i:i + 1] for n in ("lambda_q1", "lambda_k1", "lambda_q2", "lambda_k2")]
            dheads = nq // (2 * HEAD_DIM)
            o = _diff_attention(q, segs, lams, P["g_diff_sub"][i:i + 1], grp, dheads,
                                min(grp.seq, 1024), dheads if grp.seq <= 256 else 2, lambda_init)
            x = _mm_res([o], P["w_out_c"], i, x, mod, l, 2, grp, tk=o.shape[1])
        g = _ffn_up(x, (P["g_norm_ffn"], mod, 4, 3), P["w_ffn_up"], P["w_ffn_conv"],
                    P["b_ffn_conv"], l, grp)
        x = _mm_res([g], P["w_ffn_down"], l, x, mod, l, 5, grp, tk=512, tn=1024)
    y = _final_norm(x, P["g_final"])
    return y, (ckv_p, krope_p, kb_p, vb_p), (k_c, v_c)


def kernel(x_prompt, x_sample, cache_mla_ckv, cache_mla_krope, cache_gqa_k, cache_gqa_v,
           cache_diff_k, cache_diff_v, c, c_ctx, w_ada, b_ada, g_norm_mix, g_norm_ffn,
           w_in_ab, g_mla_q, w_mla_q_up, g_mla_kv, w_mla_kv_up, g_gqa_q, g_gqa_k, w_out_ab,
           w_in_c, lambda_q1, lambda_k1, lambda_q2, lambda_k2, g_diff_sub, w_out_c,
           w_ffn_up, w_ffn_conv, b_ffn_conv, w_ffn_down, g_final):
    bp, tp, d = x_prompt.shape
    bs, ts, _ = x_sample.shape
    depth = w_ada.shape[0]
    n_even = w_in_ab.shape[0]
    assert depth == 2 and n_even == 1 and w_in_c.shape[0] == 1, "one MLA/GQA layer + one diff layer"
    past = cache_mla_ckv.shape[2]
    n_qlat = g_mla_q.shape[1]
    n_ckv = g_mla_kv.shape[1]
    mla_heads = w_mla_q_up.shape[2] // (MLA_NOPE + MLA_ROPE)
    n_kb = cache_gqa_k.shape[3] * HEAD_DIM
    n_vb = n_kb
    n_qb = w_in_ab.shape[2] - n_qlat - n_ckv - MLA_ROPE - n_kb - n_vb

    rows = 8
    cond = jnp.zeros((rows, d), F32).at[:bs].set(c).at[bs].set(c_ctx)
    mod = _modulation(cond, w_ada, b_ada)
    mod = mod.reshape(depth, rows, 1, mod.shape[2])

    o0 = n_qlat + n_ckv
    o1 = o0 + MLA_ROPE
    w_a = w_in_ab[:, :, :o0].astype(BF16)
    w_b = w_in_ab[:, :, o1:].astype(BF16)
    w_k = jnp.pad(w_in_ab[:, :, o0:o1], ((0, 0), (0, 0), (0, LANES - MLA_ROPE))).astype(BF16)
    wq = w_mla_q_up.reshape(n_even, n_qlat, mla_heads, MLA_NOPE + MLA_ROPE)
    wq = wq * ((MLA_NOPE + MLA_ROPE) ** -0.5 * LOG2E)
    w_q_up_p = jnp.pad(wq, ((0, 0), (0, 0), (0, 0), (0, LANES - MLA_ROPE))).reshape(
        n_even, n_qlat, mla_heads * (MLA_NOPE + LANES)).astype(BF16)
    P = dict(
        w_ab=(w_a, w_b, w_k, w_q_up_p, w_mla_kv_up.astype(BF16)),
        g_mla_q=g_mla_q, g_mla_kv=g_mla_kv, g_gqa_q=g_gqa_q * (HEAD_DIM ** -0.5 * LOG2E),
        g_gqa_k=g_gqa_k,
        w_out_ab=w_out_ab, w_in_c=w_in_c, lambda_q1=lambda_q1, lambda_k1=lambda_k1,
        lambda_q2=lambda_q2, lambda_k2=lambda_k2, g_diff_sub=g_diff_sub, w_out_c=w_out_c,
        w_ffn_up=w_ffn_up, w_ffn_conv=w_ffn_conv, b_ffn_conv=b_ffn_conv, w_ffn_down=w_ffn_down,
        g_final=g_final,
        g_norm_mix=g_norm_mix.reshape(depth, 1, d), g_norm_ffn=g_norm_ffn.reshape(depth, 1, d),
        dims_ab=(n_qlat, n_ckv, n_qb, n_kb, n_vb, mla_heads),
    )

    grp_p = _Group(bp, tp, bs, False, None)
    y_p, ab_p, c_p = _trunk(x_prompt.reshape(bp * tp, d), grp_p, mod, None, P)

    c128, s128 = _rope_tables(ts, HEAD_DIM)
    c64, s64 = _rope_tables(ts, MLA_ROPE)
    grp_s = _Group(bs, ts, 0, True, dict(c128=c128, s128=s128, c64=c64, s64=s64))
    n_odd = cache_diff_k.shape[1]
    kr_pad = jnp.pad(cache_mla_krope, ((0, 0), (0, 0), (0, 0), (0, LANES - MLA_ROPE))).astype(BF16)
    per_layer = lambda a, n: [a[:, j].reshape(bs * past, -1) for j in range(n)]
    caches = (per_layer(cache_mla_ckv, n_even), per_layer(kr_pad, n_even),
              per_layer(cache_gqa_k, n_even), per_layer(cache_gqa_v, n_even),
              per_layer(cache_diff_k.astype(BF16), n_odd),
              per_layer(cache_diff_v.astype(BF16), n_odd))
    y_s, _, _ = _trunk(x_sample.reshape(bs * ts, d), grp_s, mod, caches, P)

    ckv, krope, kb, vb = ab_p
    k_c, v_c = c_p
    kvh = n_kb // HEAD_DIM
    dh = k_c.shape[1] // (2 * HEAD_DIM)
    return (y_p.reshape(bp, tp, d), y_s.reshape(bs, ts, d),
            ckv.reshape(bp, 1, tp, n_ckv), krope.reshape(bp, 1, tp, MLA_ROPE),
            kb.reshape(bp, 1, tp, kvh, HEAD_DIM), vb.reshape(bp, 1, tp, kvh, HEAD_DIM),
            k_c.reshape(bp, 1, tp, dh, 2, HEAD_DIM), v_c.reshape(bp, 1, tp, dh, 2 * HEAD_DIM))
```

```python
import functools
import math

import jax
import jax.numpy as jnp
from jax import lax
from jax.experimental import pallas as pl
from jax.experimental.pallas import tpu as pltpu

F32 = jnp.float32
BF16 = jnp.bfloat16

HEAD_DIM = 128
GRID_W = 64
ROPE_BASE = 10000.0
EPS = 1e-6
MLA_NOPE = 128
MLA_ROPE = 64
MLA_V = 128
GQA_KV_HEADS = 2
LOG2E = 1.4426950408889634
LANES = 128
FFN_EDGE_ROWS = 16
NORM_ROWS = 64
FFN_SUB_COLS = 256
VMEM_LIMIT = 56 * 1024 * 1024


def _params(*sem):
    return pltpu.CompilerParams(dimension_semantics=sem, vmem_limit_bytes=VMEM_LIMIT)


def _rms(x, g):
    ms = jnp.mean(x * x, axis=-1, keepdims=True)
    return x * lax.rsqrt(ms + EPS) * g


def _silu(x):
    return x / (1.0 + jnp.exp(-x))


def _rope_half(x, c, s, half):
    if 2 * half == LANES:
        sw = pltpu.roll(x, half, 1)
    else:
        lane = lax.broadcasted_iota(jnp.int32, x.shape, 1)
        first = (lane % (2 * half)) < half
        sw = jnp.where(first, pltpu.roll(x, LANES - half, 1), pltpu.roll(x, half, 1))
    return x * c + sw * s


def _mod_kernel(cond_ref, w_ref, b_ref, o_ref):
    s = _silu(cond_ref[...]).astype(BF16)
    o_ref[...] = jnp.dot(s, w_ref[...].astype(BF16), preferred_element_type=F32) + b_ref[...]


def _modulation(cond, w_ada, b_ada, tn=1024):
    depth, d, n = w_ada.shape
    rows = cond.shape[0]
    return pl.pallas_call(
        _mod_kernel,
        out_shape=jax.ShapeDtypeStruct((depth, rows, n), F32),
        grid=(depth, n // tn),
        in_specs=[
            pl.BlockSpec((rows, d), lambda l, j: (0, 0)),
            pl.BlockSpec((None, d, tn), lambda l, j: (l, 0, j)),
            pl.BlockSpec((None, 1, tn), lambda l, j: (l, 0, j)),
        ],
        out_specs=pl.BlockSpec((None, rows, tn), lambda l, j: (l, 0, j)),
        compiler_params=_params("arbitrary", "arbitrary"),
        name="adaln_mod",
    )(cond, w_ada, b_ada.reshape(depth, 1, n))


def _norm_mod_tile(x_ref, g_ref, sc_ref, sh_ref):
    y = _rms(x_ref[...], g_ref[...])
    return (y * (1.0 + sc_ref[...]) + sh_ref[...]).astype(BF16)


def _norm_mod_into(h_ref, x_ref, g_ref, sc_ref, sh_ref):
    g, sc1, sh = g_ref[...], 1.0 + sc_ref[...], sh_ref[...]
    for r0 in range(0, x_ref.shape[0], NORM_ROWS):
        rows = slice(r0, r0 + NORM_ROWS)
        h_ref[rows, :] = (_rms(x_ref[rows, :], g) * sc1 + sh).astype(BF16)


class _Group:
    def __init__(self, batch, seq, mod_row0, per_batch_mod, rope):
        self.batch = batch
        self.seq = seq
        self.m = batch * seq
        self.mod_row0 = mod_row0
        self.per_batch_mod = per_batch_mod
        self.rope = rope

    def mod_rows(self, tm):
        if self.per_batch_mod and tm > self.seq:
            assert tm % self.seq == 0 and self.mod_row0 % (tm // self.seq) == 0
            return tm // self.seq
        return 1

    def mod_spec(self, layer, col, width, tm):
        nb = self.mod_rows(tm)
        if self.per_batch_mod:
            assert tm % self.seq == 0 or self.seq % tm == 0
            row = lambda i: (self.mod_row0 + (i * tm) // self.seq) // nb
        else:
            row = lambda i: self.mod_row0
        col_fn = col if callable(col) else (lambda *idx: col)
        return pl.BlockSpec((None, nb, 1, width),
                            lambda *idx: (layer, row(idx[0]), 0, col_fn(*idx)))

    def norm_specs(self, layer, chunk_sc, chunk_sh, d, tm):
        assert self.mod_rows(tm) == 1
        vec = lambda chunk: pl.BlockSpec(
            (None, None, 1, d), self.mod_spec(layer, chunk, d, tm).index_map)
        return [pl.BlockSpec((None, 1, d), lambda *idx: (layer, 0, 0)), vec(chunk_sc), vec(chunk_sh)]


def _inproj_ab_kernel(*refs, rope, n_qlat, n_ckv, n_qb, n_kb, n_vb, mla_heads):
    (x_ref, gn_ref, sc_ref, sh_ref, wa_ref, wb_ref, wk_ref, wq_ref, wkv_ref,
     gq_ref, gkv_ref, gqn_ref, gkn_ref) = refs[:13]
    (qmla_ref, kv_ref, ckv_ref, kr_ref, krp_ref, qb_ref, kb_ref, vb_ref) = refs[-8:]
    if rope:
        c128, s128, c64, s64 = (r[...] for r in refs[13:17])
    h = _norm_mod_tile(x_ref, gn_ref, sc_ref, sh_ref)

    def seg(w_ref, a, b):
        return jnp.dot(h, w_ref[:, a:b], preferred_element_type=F32)

    qn = _rms(seg(wa_ref, 0, n_qlat), gq_ref[...]).astype(BF16)
    qa = jnp.dot(qn, wq_ref[...], preferred_element_type=F32)
    hw = MLA_NOPE + LANES
    for i in range(mla_heads):
        qmla_ref[:, i * hw:i * hw + MLA_NOPE] = qa[:, i * hw:i * hw + MLA_NOPE].astype(BF16)
        rp = qa[:, i * hw + MLA_NOPE:(i + 1) * hw]
        if rope:
            rp = _rope_half(rp, c64, s64, MLA_ROPE // 2)
        qmla_ref[:, i * hw + MLA_NOPE:(i + 1) * hw] = rp.astype(BF16)
    cn = _rms(seg(wa_ref, n_qlat, n_qlat + n_ckv), gkv_ref[...])
    ckv_ref[...] = cn
    kv_ref[...] = jnp.dot(cn.astype(BF16), wkv_ref[...], preferred_element_type=F32).astype(BF16)
    r = seg(wb_ref, 0, n_qb)
    for i in range(n_qb // HEAD_DIM):
        y = _rms(r[:, i * HEAD_DIM:(i + 1) * HEAD_DIM], gqn_ref[...])
        if rope:
            y = _rope_half(y, c128, s128, HEAD_DIM // 2)
        qb_ref[:, i * HEAD_DIM:(i + 1) * HEAD_DIM] = y.astype(BF16)
    r = seg(wb_ref, n_qb, n_qb + n_kb)
    for i in range(n_kb // HEAD_DIM):
        y = _rms(r[:, i * HEAD_DIM:(i + 1) * HEAD_DIM], gkn_ref[...])
        if rope:
            y = _rope_half(y, c128, s128, HEAD_DIM // 2)
        kb_ref[:, i * HEAD_DIM:(i + 1) * HEAD_DIM] = y
    vb_ref[...] = seg(wb_ref, n_qb + n_kb, n_qb + n_kb + n_vb)
    r = seg(wk_ref, 0, LANES)
    if rope:
        r = _rope_half(r, c64, s64, MLA_ROPE // 2)
    kr_ref[...] = r[:, :MLA_ROPE]
    krp_ref[...] = r.astype(BF16)


def _inproj_ab(x, norm, weights, g_q, g_kv, g_qn, g_kn, grp, dims, tm=256):
    m, d = x.shape
    n_qlat, n_ckv, n_qb, n_kb, n_vb, mla_heads = dims
    g_norm, mod, layer, chunk_sc, chunk_sh = norm
    rope = grp.rope is not None
    one = pl.Buffered(1)
    nq_out = mla_heads * (MLA_NOPE + LANES)
    nkv_out = weights[4].shape[1]
    in_specs = [pl.BlockSpec((tm, d), lambda i: (i, 0))]
    in_specs += grp.norm_specs(layer, chunk_sc, chunk_sh, d, tm)
    in_specs += [pl.BlockSpec(w.shape, lambda i: (0, 0), pipeline_mode=one) for w in weights]
    in_specs += [
        pl.BlockSpec((1, n_qlat), lambda i: (0, 0)),
        pl.BlockSpec((1, n_ckv), lambda i: (0, 0)),
        pl.BlockSpec((1, HEAD_DIM), lambda i: (0, 0)),
        pl.BlockSpec((1, HEAD_DIM), lambda i: (0, 0)),
    ]
    args = [x, g_norm, mod, mod, *weights, g_q, g_kv, g_qn, g_kn]
    if rope:
        per = grp.seq // tm
        for name in ("c128", "s128", "c64", "s64"):
            in_specs.append(pl.BlockSpec((tm, LANES), lambda i: (i % per, 0)))
            args.append(grp.rope[name])
    widths = [(nq_out, BF16), (nkv_out, BF16), (n_ckv, F32), (MLA_ROPE, F32), (LANES, BF16),
              (n_qb, BF16), (n_kb, F32), (n_vb, F32)]
    return pl.pallas_call(
        functools.partial(_inproj_ab_kernel, rope=rope, n_qlat=n_qlat, n_ckv=n_ckv, n_qb=n_qb,
                          n_kb=n_kb, n_vb=n_vb, mla_heads=mla_heads),
        out_shape=[jax.ShapeDtypeStruct((m, w), dt) for w, dt in widths],
        grid=(m // tm,),
        in_specs=in_specs,
        out_specs=[pl.BlockSpec((tm, w), lambda i: (i, 0)) for w, _ in widths],
        compiler_params=_params("arbitrary"),
        name="inproj_ab",
    )(*args)


def _kvup_kernel(c_ref, w_ref, o_ref):
    o_ref[...] = jnp.dot(c_ref[...].astype(BF16), w_ref[...],
                         preferred_element_type=F32).astype(o_ref.dtype)


def _kvup(ckv, wkv, tm=512):
    m, k = ckv.shape
    n = wkv.shape[1]
    return pl.pallas_call(
        _kvup_kernel,
        out_shape=jax.ShapeDtypeStruct((m, n), BF16),
        grid=(m // tm,),
        in_specs=[pl.BlockSpec((tm, k), lambda i: (i, 0)),
                  pl.BlockSpec((k, n), lambda i: (0, 0))],
        out_specs=pl.BlockSpec((tm, n), lambda i: (i, 0)),
        compiler_params=_params("arbitrary"),
        name="mla_kv_up",
    )(ckv, wkv)


def _proj_kernel(*refs, rope, scale, norm):
    if norm:
        x_ref, gn_ref, sc_ref, sh_ref, *rest = refs
        h_ref = rest.pop()

        @pl.when(pl.program_id(1) == 0)
        def _():
            _norm_mod_into(h_ref, x_ref, gn_ref, sc_ref, sh_ref)
    else:
        h_ref, *rest = refs
    if rope:
        w_ref, c_ref, s_ref, o_ref = rest
    else:
        w_ref, o_ref = rest
    r = jnp.dot(h_ref[...], w_ref[...].astype(BF16), preferred_element_type=F32)
    if rope:
        c, s = c_ref[...], s_ref[...]
        for i in range(r.shape[1] // HEAD_DIM):
            sl = slice(i * HEAD_DIM, (i + 1) * HEAD_DIM)
            y = _rope_half(r[:, sl], c, s, HEAD_DIM // 2)
            o_ref[:, sl] = (y if scale is None else y * scale).astype(o_ref.dtype)
    else:
        o_ref[...] = (r if scale is None else r * scale).astype(o_ref.dtype)


def _proj(h, w3, col0, ncols, out_dtype, grp, use_rope, scale=None, norm=None, tm=2048, tn=512):
    m, d = h.shape
    rope = use_rope and grp.rope is not None
    tm = min(tm, m)
    off = col0 // tn
    in_specs = [pl.BlockSpec((tm, d), lambda i, j: (i, 0))]
    args = [h]
    out_shape = [jax.ShapeDtypeStruct((m, ncols), out_dtype)]
    out_specs = [pl.BlockSpec((tm, tn), lambda i, j: (i, j))]
    if norm is not None:
        g_norm, mod, layer, chunk_sc, chunk_sh = norm
        in_specs += grp.norm_specs(layer, chunk_sc, chunk_sh, d, tm)
        args += [g_norm, mod, mod]
        out_shape.append(jax.ShapeDtypeStruct((m, d), BF16))
        out_specs.append(pl.BlockSpec((tm, d), lambda i, j: (i, 0)))
    in_specs.append(pl.BlockSpec((None, d, tn), lambda i, j: (0, 0, off + j)))
    args.append(w3)
    if rope:
        assert tm % grp.seq == 0
        for name in ("c128", "s128"):
            in_specs.append(pl.BlockSpec((tm, LANES), lambda i, j: (0, 0)))
            args.append(jnp.tile(grp.rope[name], (tm // grp.seq, 1)))
    res = pl.pallas_call(
        functools.partial(_proj_kernel, rope=rope, scale=scale, norm=norm is not None),
        out_shape=out_shape,
        grid=(m // tm, ncols // tn),
        in_specs=in_specs,
        out_specs=out_specs,
        compiler_params=_params("arbitrary", "arbitrary"),
        name="proj_c",
    )(*args)
    return res if norm is not None else res[0]


def _softmax_parts(scores):
    mx = jnp.max(scores[0], axis=-1, keepdims=True)
    for s in scores[1:]:
        mx = jnp.maximum(mx, jnp.max(s, axis=-1, keepdims=True))
    ps = [jnp.exp2(s - mx) for s in scores]
    l = jnp.sum(ps[0], axis=-1, keepdims=True)
    for p in ps[1:]:
        l = l + jnp.sum(p, axis=-1, keepdims=True)
    return ps, l


def _attend(q, ks, vs):
    scores = [lax.dot_general(q, k, _NT, preferred_element_type=F32) for k in ks]
    ps, l = _softmax_parts(scores)
    o = jnp.dot(ps[0].astype(BF16), vs[0], preferred_element_type=F32)
    for p, v in zip(ps[1:], vs[1:]):
        o = o + jnp.dot(p.astype(BF16), v, preferred_element_type=F32)
    return o, l


_NT = (((1,), (1,)), ((), ()))


def _mla_kernel(*refs, n_seg, hg):
    q_ref = refs[0]
    o_ref = refs[-1]
    hw = MLA_NOPE + LANES
    for h in range(hg):
        ks, vs = [], []
        for i in range(n_seg):
            kv_ref, kr_ref = refs[1 + 2 * i], refs[2 + 2 * i]
            ks.append(jnp.concatenate([kv_ref[:, h * hw:h * hw + MLA_NOPE], kr_ref[...]], axis=1))
            vs.append(kv_ref[:, h * hw + MLA_NOPE:(h + 1) * hw])
        o, l = _attend(q_ref[:, h * hw:(h + 1) * hw], ks, vs)
        o_ref[:, h * MLA_V:(h + 1) * MLA_V] = (o / l).astype(o_ref.dtype)


def _mla_attention(q, segs, grp, heads, tq, hg):
    m = q.shape[0]
    nq = grp.seq // tq
    hw = MLA_NOPE + LANES
    in_specs = [pl.BlockSpec((tq, hg * hw), lambda b, h, i: (b * nq + i, h))]
    args = [q]
    for kv, kr, rows in segs:
        in_specs.append(pl.BlockSpec((rows, hg * (MLA_NOPE + MLA_V)), lambda b, h, i: (b, h)))
        in_specs.append(pl.BlockSpec((rows, LANES), lambda b, h, i: (b, 0)))
        args += [kv, kr]
    return pl.pallas_call(
        functools.partial(_mla_kernel, n_seg=len(segs), hg=hg),
        out_shape=jax.ShapeDtypeStruct((m, heads * MLA_V), BF16),
        grid=(grp.batch, heads // hg, nq),
        in_specs=in_specs,
        out_specs=pl.BlockSpec((tq, hg * MLA_V), lambda b, h, i: (b * nq + i, h)),
        compiler_params=_params("arbitrary", "arbitrary", "arbitrary"),
        name="attn_mla",
    )(*args)


def _gqa_kernel(*refs, n_seg, group, kvg):
    q_ref = refs[0]
    o_ref = refs[-1]
    for kh in range(kvg):
        ksl = slice(kh * HEAD_DIM, (kh + 1) * HEAD_DIM)
        ks = [refs[1 + 2 * i][:, ksl].astype(BF16) for i in range(n_seg)]
        vs = [refs[2 + 2 * i][:, ksl].astype(BF16) for i in range(n_seg)]
        for g in range(group):
            sl = slice((kh * group + g) * HEAD_DIM, (kh * group + g + 1) * HEAD_DIM)
            o, l = _attend(q_ref[:, sl], ks, vs)
            o_ref[:, sl] = (o / l).astype(o_ref.dtype)


def _gqa_attention(q, segs, grp, heads, kv_heads, tq, kvg):
    m = q.shape[0]
    nq = grp.seq // tq
    group = heads // kv_heads
    gw = kvg * group * HEAD_DIM
    in_specs = [pl.BlockSpec((tq, gw), lambda b, h, i: (b * nq + i, h))]
    args = [q]
    for k, v, rows in segs:
        in_specs.append(pl.BlockSpec((rows, kvg * HEAD_DIM), lambda b, h, i: (b, h)))
        in_specs.append(pl.BlockSpec((rows, kvg * HEAD_DIM), lambda b, h, i: (b, h)))
        args += [k, v]
    return pl.pallas_call(
        functools.partial(_gqa_kernel, n_seg=len(segs), group=group, kvg=kvg),
        out_shape=jax.ShapeDtypeStruct((m, heads * HEAD_DIM), BF16),
        grid=(grp.batch, kv_heads // kvg, nq),
        in_specs=in_specs,
        out_specs=pl.BlockSpec((tq, gw), lambda b, h, i: (b * nq + i, h)),
        compiler_params=_params("arbitrary", "arbitrary", "arbitrary"),
        name="attn_gqa",
    )(*args)


def _diff_kernel(*refs, n_seg, hg, lambda_init):
    q_ref, lq1_ref, lk1_ref, lq2_ref, lk2_ref, g_ref = refs[:6]
    o_ref = refs[-1]
    lam = (jnp.exp(jnp.sum(lq1_ref[...] * lk1_ref[...], axis=-1, keepdims=True))
           - jnp.exp(jnp.sum(lq2_ref[...] * lk2_ref[...], axis=-1, keepdims=True)) + lambda_init)
    hw = 2 * HEAD_DIM
    for h in range(hg):
        vs = [refs[7 + 2 * i][:, h * hw:(h + 1) * hw].astype(BF16) for i in range(n_seg)]
        halves = []
        for j in range(2):
            sl = slice(h * hw + j * HEAD_DIM, h * hw + (j + 1) * HEAD_DIM)
            ks = [refs[6 + 2 * i][:, sl].astype(BF16) for i in range(n_seg)]
            o, l = _attend(q_ref[:, sl], ks, vs)
            halves.append(o * (1.0 / l))
        o = halves[0] - lam * halves[1]
        o_ref[:, h * hw:(h + 1) * hw] = (_rms(o, g_ref[...]) * (1.0 - lambda_init)).astype(o_ref.dtype)


def _diff_attention(q, segs, lams, g_sub, grp, heads, tq, hg, lambda_init):
    m = q.shape[0]
    nq = grp.seq // tq
    hw = 2 * HEAD_DIM
    vec = lambda w: pl.BlockSpec((1, w), lambda b, h, i: (0, 0))
    in_specs = [pl.BlockSpec((tq, hg * hw), lambda b, h, i: (b * nq + i, h))]
    in_specs += [vec(HEAD_DIM)] * 4 + [vec(hw)]
    args = [q, *lams, g_sub]
    for k, v, rows in segs:
        in_specs.append(pl.BlockSpec((rows, hg * hw), lambda b, h, i: (b, h)))
        in_specs.append(pl.BlockSpec((rows, hg * hw), lambda b, h, i: (b, h)))
        args += [k, v]
    return pl.pallas_call(
        functools.partial(_diff_kernel, n_seg=len(segs), hg=hg, lambda_init=lambda_init),
        out_shape=jax.ShapeDtypeStruct((m, heads * hw), BF16),
        grid=(grp.batch, heads // hg, nq),
        in_specs=in_specs,
        out_specs=pl.BlockSpec((tq, hg * hw), lambda b, h, i: (b * nq + i, h)),
        compiler_params=_params("arbitrary", "arbitrary", "arbitrary"),
        name="attn_diff",
    )(*args)


def _mm_res_kernel(*refs, n_a, nk, nb, out_norm):
    a_refs = refs[:n_a]
    w_refs = refs[n_a:2 * n_a]
    x_ref, gt_ref = refs[2 * n_a:2 * n_a + 2]
    o_ref = refs[-1]
    seg = o_ref.shape[0] // nb
    if nk == 1:
        ws = [w_ref[...].astype(BF16) for w_ref in w_refs]
        for r in range(nb):
            rows = slice(r * seg, (r + 1) * seg)
            part = None
            for a_ref, w in zip(a_refs, ws):
                p = jnp.dot(a_ref[rows, :], w, preferred_element_type=F32)
                part = p if part is None else part + p
            o_ref[rows, :] = x_ref[rows, :] + gt_ref[r] * part
        return
    k = pl.program_id(2)

    def step(first):
        w32 = w_refs[0][...]
        for r in range(nb):
            rows = slice(r * seg, (r + 1) * seg)
            w = (w32 * gt_ref[r]).astype(BF16)
            p = jnp.dot(a_refs[0][rows, :], w, preferred_element_type=F32)
            if first:
                o_ref[rows, :] = x_ref[rows, :] + p
            else:
                o_ref[rows, :] += p

    @pl.when(k == 0)
    def _():
        step(True)

    @pl.when(k > 0)
    def _():
        step(False)

    if out_norm:
        gf_ref = refs[2 * n_a + 2]

        @pl.when(k == nk - 1)
        def _():
            for r0 in range(0, o_ref.shape[0], NORM_ROWS):
                rows = slice(r0, r0 + NORM_ROWS)
                o_ref[rows, :] = _rms(o_ref[rows, :], gf_ref[...])


def _mm_res(a_list, w3, w_layer, x, mod, layer, chunk, grp, tk, tm=2048, tn=512, out_gain=None):
    m, n = x.shape
    if out_gain is not None:
        tn = n
    n_a = len(a_list)
    ka = a_list[0].shape[1]
    if n_a > 1:
        tk = ka
    nk = ka // tk
    ncol = n // tn
    in_specs = [pl.BlockSpec((tm, tk), lambda i, j, k: (i, k)) for _ in a_list]
    for idx in range(n_a):
        in_specs.append(pl.BlockSpec((None, tk, tn),
                                     lambda i, j, k, idx=idx: (w_layer, idx * (ka // tk) + k, j)))
    in_specs.append(pl.BlockSpec((tm, tn), lambda i, j, k: (i, j)))
    in_specs.append(grp.mod_spec(layer, lambda i, j, k: chunk * ncol + j, tn, tm))
    args = [*a_list, *([w3] * n_a), x, mod]
    if out_gain is not None:
        assert nk > 1 and ncol == 1
        in_specs.append(pl.BlockSpec((1, n), lambda i, j, k: (0, 0)))
        args.append(out_gain.reshape(1, n))
    return pl.pallas_call(
        functools.partial(_mm_res_kernel, n_a=n_a, nk=nk, nb=grp.mod_rows(tm),
                          out_norm=out_gain is not None),
        out_shape=jax.ShapeDtypeStruct((m, n), F32),
        grid=(m // tm, ncol, nk),
        in_specs=in_specs,
        out_specs=pl.BlockSpec((tm, tn), lambda i, j, k: (i, j)),
        compiler_params=_params("arbitrary", "arbitrary", "arbitrary"),
        name="mm_res",
    )(*args)


def _ffn_up_kernel(x_ref, gn_ref, sc_ref, sh_ref, wg_ref, wv_ref, cg_ref, cv_ref, bg_ref, bv_ref,
                   o_ref, h_ref, *, seq):
    @pl.when(pl.program_id(1) == 0)
    def _():
        _norm_mod_into(h_ref, x_ref, gn_ref, sc_ref, sh_ref)

    h = h_ref[...]
    tm, tn = o_ref.shape
    sub = lax.broadcasted_iota(jnp.int32, (FFN_EDGE_ROWS, 1), 0)
    edges = sorted({r for b in range(0, tm, seq) for r in (b, b + seq - FFN_EDGE_ROWS)})
    for c0 in range(0, tn, FFN_SUB_COLS):
        cs = slice(c0, c0 + FFN_SUB_COLS)

        def taps(w_ref):
            u = jnp.dot(h, w_ref[:, cs].astype(BF16), preferred_element_type=F32)
            return pltpu.roll(u, 1, 0), u, pltpu.roll(u, tm - 1, 0)

        def conv(t, c_ref, b_ref):
            return t[0] * c_ref[0:1, cs] + t[1] * c_ref[1:2, cs] + t[2] * c_ref[2:3, cs] + b_ref[:, cs]

        tg, tv = taps(wg_ref), taps(wv_ref)
        o_ref[:, cs] = (_silu(conv(tg, cg_ref, bg_ref)) * conv(tv, cv_ref, bv_ref)).astype(o_ref.dtype)
        for r0 in edges:
            rows = slice(r0, r0 + FFN_EDGE_ROWS)

            def edge(t):
                prev, cur, nxt = (a[rows, :] for a in t)
                if r0 % seq == 0:
                    prev = jnp.where(sub == 0, 0.0, prev)
                if (r0 + FFN_EDGE_ROWS) % seq == 0:
                    nxt = jnp.where(sub == FFN_EDGE_ROWS - 1, 0.0, nxt)
                return prev, cur, nxt

            o_ref[rows, cs] = (_silu(conv(edge(tg), cg_ref, bg_ref))
                               * conv(edge(tv), cv_ref, bv_ref)).astype(o_ref.dtype)


def _ffn_up(x, norm, w_up, w_conv, b_conv, layer, grp, tm=1024, tn=512):
    m, d = x.shape
    d_ff = w_up.shape[2] // 2
    nj = d_ff // tn
    assert tm % grp.seq == 0 and m % tm == 0
    g_norm, mod, chunk_sc, chunk_sh = norm
    b3 = b_conv.reshape(b_conv.shape[0], 1, b_conv.shape[1])
    return pl.pallas_call(
        functools.partial(_ffn_up_kernel, seq=grp.seq),
        out_shape=jax.ShapeDtypeStruct((m, d_ff), BF16),
        grid=(m // tm, nj),
        in_specs=[
            pl.BlockSpec((tm, d), lambda i, j: (i, 0)),
            *grp.norm_specs(layer, chunk_sc, chunk_sh, d, tm),
            pl.BlockSpec((None, d, tn), lambda i, j: (layer, 0, j)),
            pl.BlockSpec((None, d, tn), lambda i, j: (layer, 0, nj + j)),
            pl.BlockSpec((None, w_conv.shape[1], tn), lambda i, j: (layer, 0, j)),
            pl.BlockSpec((None, w_conv.shape[1], tn), lambda i, j: (layer, 0, nj + j)),
            pl.BlockSpec((None, 1, tn), lambda i, j: (layer, 0, j)),
            pl.BlockSpec((None, 1, tn), lambda i, j: (layer, 0, nj + j)),
        ],
        out_specs=pl.BlockSpec((tm, tn), lambda i, j: (i, j)),
        scratch_shapes=[pltpu.VMEM((tm, d), BF16)],
        compiler_params=_params("arbitrary", "arbitrary"),
        name="ffn_up",
    )(x, g_norm, mod, mod, w_up, w_up, w_conv, w_conv, b3, b3)


def _rope_tables(n_tokens, rot_dim):
    n_rows = n_tokens // GRID_W
    row = jnp.repeat(jnp.arange(n_rows), GRID_W).astype(F32)
    col = jnp.tile(jnp.arange(GRID_W), n_rows).astype(F32)
    n_freq = rot_dim // 4
    freqs = ROPE_BASE ** (-jnp.arange(n_freq, dtype=F32) / n_freq)
    ang = jnp.concatenate([row[:, None] * freqs, col[:, None] * freqs], axis=-1)
    cos, sin = jnp.cos(ang), jnp.sin(ang)
    reps = LANES // rot_dim
    c = jnp.tile(jnp.concatenate([cos, cos], axis=-1), (1, reps))
    s = jnp.tile(jnp.concatenate([-sin, sin], axis=-1), (1, reps))
    return c, s


def _trunk(x, grp, mod, caches, P):
    d = x.shape[1]
    ckv_p = krope_p = kb_p = vb_p = k_c = v_c = None
    depth = P["w_ffn_up"].shape[0]
    for l in range(depth):
        i = l // 2
        norm_mix = (P["g_norm_mix"], mod, l, 1, 0)
        if l % 2 == 0:
            (q_mla, kv, ckv, krope, krope_pad, qb, kb, vb) = _inproj_ab(
                x, norm_mix, [w[i] for w in P["w_ab"]],
                P["g_mla_q"][i:i + 1], P["g_mla_kv"][i:i + 1], P["g_gqa_q"][i:i + 1],
                P["g_gqa_k"][i:i + 1], grp, P["dims_ab"])
            ckv_p, krope_p, kb_p, vb_p = ckv, krope, kb, vb
            heads = P["dims_ab"][5]
            segs_a, segs_b = [], []
            if caches is not None:
                c_ckv, c_kr, c_k, c_v = (cc[i] for cc in caches[:4])
                past = c_ckv.shape[0] // grp.batch
                segs_a.append((_kvup(c_ckv, P["w_ab"][4][i]), c_kr, past))
                segs_b.append((c_k, c_v, past))
            segs_a.append((kv, krope_pad, grp.seq))
            segs_b.append((kb, vb, grp.seq))
            tq = min(grp.seq, 512)
            out_a = _mla_attention(q_mla, segs_a, grp, heads, tq, heads)
            out_b = _gqa_attention(qb, segs_b, grp, qb.shape[1] // HEAD_DIM, GQA_KV_HEADS, tq,
                                   GQA_KV_HEADS)
            x = _mm_res([out_a, out_b], P["w_out_ab"], i, x, mod, l, 2, grp, tk=None)
        else:
            w_in_c = P["w_in_c"][i:i + 1]
            nq = w_in_c.shape[2] // 3
            kv_dtype = F32 if caches is None else BF16
            q, h = _proj(x, w_in_c, 0, nq, BF16, grp, True, scale=HEAD_DIM ** -0.5 * LOG2E,
                         norm=norm_mix, tm=1024)
            k = _proj(h, w_in_c, nq, nq, kv_dtype, grp, True)
            v = _proj(h, w_in_c, 2 * nq, nq, kv_dtype, grp, False)
            k_c, v_c = k, v
            segs = []
            if caches is not None:
                c_k, c_v = (cc[i] for cc in caches[4:])
                segs.append((c_k, c_v, c_k.shape[0] // grp.batch))
            segs.append((k, v, grp.seq))
            lambda_init = 0.8 - 0.6 * math.exp(-0.3 * l)
            lams = [P[n][i:i + 1] for n in ("lambda_q1", "lambda_k1", "lambda_q2", "lambda_k2")]
            dheads = nq // (2 * HEAD_DIM)
            o = _diff_attention(q, segs, lams, P["g_diff_sub"][i:i + 1], grp, dheads,
                                min(grp.seq, 1024), dheads if grp.seq <= 256 else 2, lambda_init)
            x = _mm_res([o], P["w_out_c"], i, x, mod, l, 2, grp, tk=o.shape[1])
        g = _ffn_up(x, (P["g_norm_ffn"], mod, 4, 3), P["w_ffn_up"], P["w_ffn_conv"],
                    P["b_ffn_conv"], l, grp)
        if l + 1 < depth:
            x = _mm_res([g], P["w_ffn_down"], l, x, mod, l, 5, grp, tk=512, tn=1024)
        else:
            x = _mm_res([g], P["w_ffn_down"], l, x, mod, l, 5, grp, tk=512, tm=1024,
                        out_gain=P["g_final"])
    return x, (ckv_p, krope_p, kb_p, vb_p), (k_c, v_c)


def kernel(x_prompt, x_sample, cache_mla_ckv, cache_mla_krope, cache_gqa_k, cache_gqa_v,
           cache_diff_k, cache_diff_v, c, c_ctx, w_ada, b_ada, g_norm_mix, g_norm_ffn,
           w_in_ab, g_mla_q, w_mla_q_up, g_mla_kv, w_mla_kv_up, g_gqa_q, g_gqa_k, w_out_ab,
           w_in_c, lambda_q1, lambda_k1, lambda_q2, lambda_k2, g_diff_sub, w_out_c,
           w_ffn_up, w_ffn_conv, b_ffn_conv, w_ffn_down, g_final):
    bp, tp, d = x_prompt.shape
    bs, ts, _ = x_sample.shape
    depth = w_ada.shape[0]
    n_even = w_in_ab.shape[0]
    assert depth == 2 and n_even == 1 and w_in_c.shape[0] == 1, "one MLA/GQA layer + one diff layer"
    past = cache_mla_ckv.shape[2]
    n_qlat = g_mla_q.shape[1]
    n_ckv = g_mla_kv.shape[1]
    mla_heads = w_mla_q_up.shape[2] // (MLA_NOPE + MLA_ROPE)
    n_kb = cache_gqa_k.shape[3] * HEAD_DIM
    n_vb = n_kb
    n_qb = w_in_ab.shape[2] - n_qlat - n_ckv - MLA_ROPE - n_kb - n_vb

    rows = 8
    cond = jnp.zeros((rows, d), F32).at[:bs].set(c).at[bs].set(c_ctx)
    mod = _modulation(cond, w_ada, b_ada)
    mod = mod.reshape(depth, rows, 1, mod.shape[2])

    o0 = n_qlat + n_ckv
    o1 = o0 + MLA_ROPE
    w_a = w_in_ab[:, :, :o0].astype(BF16)
    w_b = w_in_ab[:, :, o1:].astype(BF16)
    w_k = jnp.pad(w_in_ab[:, :, o0:o1], ((0, 0), (0, 0), (0, LANES - MLA_ROPE))).astype(BF16)
    wq = w_mla_q_up.reshape(n_even, n_qlat, mla_heads, MLA_NOPE + MLA_ROPE)
    wq = wq * ((MLA_NOPE + MLA_ROPE) ** -0.5 * LOG2E)
    w_q_up_p = jnp.pad(wq, ((0, 0), (0, 0), (0, 0), (0, LANES - MLA_ROPE))).reshape(
        n_even, n_qlat, mla_heads * (MLA_NOPE + LANES)).astype(BF16)
    P = dict(
        w_ab=(w_a, w_b, w_k, w_q_up_p, w_mla_kv_up.astype(BF16)),
        g_mla_q=g_mla_q, g_mla_kv=g_mla_kv, g_gqa_q=g_gqa_q * (HEAD_DIM ** -0.5 * LOG2E),
        g_gqa_k=g_gqa_k,
        w_out_ab=w_out_ab, w_in_c=w_in_c, lambda_q1=lambda_q1, lambda_k1=lambda_k1,
        lambda_q2=lambda_q2, lambda_k2=lambda_k2, g_diff_sub=g_diff_sub, w_out_c=w_out_c,
        w_ffn_up=w_ffn_up, w_ffn_conv=w_ffn_conv, b_ffn_conv=b_ffn_conv, w_ffn_down=w_ffn_down,
        g_final=g_final,
        g_norm_mix=g_norm_mix.reshape(depth, 1, d), g_norm_ffn=g_norm_ffn.reshape(depth, 1, d),
        dims_ab=(n_qlat, n_ckv, n_qb, n_kb, n_vb, mla_heads),
    )

    grp_p = _Group(bp, tp, bs, False, None)
    y_p, ab_p, c_p = _trunk(x_prompt.reshape(bp * tp, d), grp_p, mod, None, P)

    c128, s128 = _rope_tables(ts, HEAD_DIM)
    c64, s64 = _rope_tables(ts, MLA_ROPE)
    grp_s = _Group(bs, ts, 0, True, dict(c128=c128, s128=s128, c64=c64, s64=s64))
    n_odd = cache_diff_k.shape[1]
    kr_pad = jnp.pad(cache_mla_krope, ((0, 0), (0, 0), (0, 0), (0, LANES - MLA_ROPE))).astype(BF16)
    per_layer = lambda a, n: [a[:, j].reshape(bs * past, -1) for j in range(n)]
    caches = (per_layer(cache_mla_ckv, n_even), per_layer(kr_pad, n_even),
              per_layer(cache_gqa_k, n_even), per_layer(cache_gqa_v, n_even),
              per_layer(cache_diff_k.astype(BF16), n_odd),
              per_layer(cache_diff_v.astype(BF16), n_odd))
    y_s, _, _ = _trunk(x_sample.reshape(bs * ts, d), grp_s, mod, caches, P)

    ckv, krope, kb, vb = ab_p
    k_c, v_c = c_p
    kvh = n_kb // HEAD_DIM
    dh = k_c.shape[1] // (2 * HEAD_DIM)
    return (y_p.reshape(bp, tp, d), y_s.reshape(bs, ts, d),
            ckv.reshape(bp, 1, tp, n_ckv), krope.reshape(bp, 1, tp, MLA_ROPE),
            kb.reshape(bp, 1, tp, kvh, HEAD_DIM), vb.reshape(bp, 1, tp, kvh, HEAD_DIM),
            k_c.reshape(bp, 1, tp, dh, 2, HEAD_DIM), v_c.reshape(bp, 1, tp, dh, 2 * HEAD_DIM))
```

```python
import functools
import math

import jax
import jax.numpy as jnp
from jax import lax
from jax.experimental import pallas as pl
from jax.experimental.pallas import tpu as pltpu

F32 = jnp.float32
BF16 = jnp.bfloat16

HEAD_DIM = 128
GRID_W = 64
ROPE_BASE = 10000.0
EPS = 1e-6
MLA_NOPE = 128
MLA_ROPE = 64
MLA_V = 128
GQA_KV_HEADS = 2
LOG2E = 1.4426950408889634
LANES = 128
FFN_EDGE_ROWS = 16
NORM_ROWS = 64
FFN_SUB_COLS = 256
VMEM_LIMIT = 56 * 1024 * 1024


def _params(*sem):
    return pltpu.CompilerParams(dimension_semantics=sem, vmem_limit_bytes=VMEM_LIMIT)


def _rms(x, g):
    ms = jnp.mean(x * x, axis=-1, keepdims=True)
    return x * lax.rsqrt(ms + EPS) * g


def _silu(x):
    return x / (1.0 + jnp.exp(-x))


def _rope_half(x, c, s, half):
    if 2 * half == LANES:
        sw = pltpu.roll(x, half, 1)
    else:
        lane = lax.broadcasted_iota(jnp.int32, x.shape, 1)
        first = (lane % (2 * half)) < half
        sw = jnp.where(first, pltpu.roll(x, LANES - half, 1), pltpu.roll(x, half, 1))
    return x * c + sw * s


def _mod_kernel(cond_ref, w_ref, b_ref, o_ref):
    s = _silu(cond_ref[...]).astype(BF16)
    o_ref[...] = jnp.dot(s, w_ref[...].astype(BF16), preferred_element_type=F32) + b_ref[...]


def _modulation(cond, w_ada, b_ada, tn=1024):
    depth, d, n = w_ada.shape
    rows = cond.shape[0]
    return pl.pallas_call(
        _mod_kernel,
        out_shape=jax.ShapeDtypeStruct((depth, rows, n), F32),
        grid=(depth, n // tn),
        in_specs=[
            pl.BlockSpec((rows, d), lambda l, j: (0, 0)),
            pl.BlockSpec((None, d, tn), lambda l, j: (l, 0, j)),
            pl.BlockSpec((None, 1, tn), lambda l, j: (l, 0, j)),
        ],
        out_specs=pl.BlockSpec((None, rows, tn), lambda l, j: (l, 0, j)),
        compiler_params=_params("arbitrary", "arbitrary"),
        name="adaln_mod",
    )(cond, w_ada, b_ada.reshape(depth, 1, n))


def _norm_mod_tile(x_ref, g_ref, sc_ref, sh_ref):
    y = _rms(x_ref[...], g_ref[...])
    return (y * (1.0 + sc_ref[...]) + sh_ref[...]).astype(BF16)


def _norm_mod_into(h_ref, x_ref, g_ref, sc_ref, sh_ref):
    g, sc1, sh = g_ref[...], 1.0 + sc_ref[...], sh_ref[...]
    for r0 in range(0, x_ref.shape[0], NORM_ROWS):
        rows = slice(r0, r0 + NORM_ROWS)
        h_ref[rows, :] = (_rms(x_ref[rows, :], g) * sc1 + sh).astype(BF16)


class _Group:
    def __init__(self, batch, seq, mod_row0, per_batch_mod, rope):
        self.batch = batch
        self.seq = seq
        self.m = batch * seq
        self.mod_row0 = mod_row0
        self.per_batch_mod = per_batch_mod
        self.rope = rope

    def mod_rows(self, tm):
        if self.per_batch_mod and tm > self.seq:
            assert tm % self.seq == 0 and self.mod_row0 % (tm // self.seq) == 0
            return tm // self.seq
        return 1

    def mod_spec(self, layer, col, width, tm):
        nb = self.mod_rows(tm)
        if self.per_batch_mod:
            assert tm % self.seq == 0 or self.seq % tm == 0
            row = lambda i: (self.mod_row0 + (i * tm) // self.seq) // nb
        else:
            row = lambda i: self.mod_row0
        col_fn = col if callable(col) else (lambda *idx: col)
        return pl.BlockSpec((None, nb, 1, width),
                            lambda *idx: (layer, row(idx[0]), 0, col_fn(*idx)))

    def norm_specs(self, layer, chunk_sc, chunk_sh, d, tm):
        assert self.mod_rows(tm) == 1
        vec = lambda chunk: pl.BlockSpec(
            (None, None, 1, d), self.mod_spec(layer, chunk, d, tm).index_map)
        return [pl.BlockSpec((None, 1, d), lambda *idx: (layer, 0, 0)), vec(chunk_sc), vec(chunk_sh)]


def _inproj_ab_kernel(*refs, rope, n_qlat, n_ckv, n_qb, n_kb, n_vb, mla_heads):
    (x_ref, gn_ref, sc_ref, sh_ref, wa_ref, wb_ref, wk_ref, wq_ref, wkv_ref,
     gq_ref, gkv_ref, gqn_ref, gkn_ref) = refs[:13]
    (qmla_ref, kv_ref, ckv_ref, kr_ref, krp_ref, qb_ref, kb_ref, vb_ref) = refs[-8:]
    if rope:
        c128, s128, c64, s64 = (r[...] for r in refs[13:17])
    h = _norm_mod_tile(x_ref, gn_ref, sc_ref, sh_ref)

    def seg(w_ref, a, b):
        return jnp.dot(h, w_ref[:, a:b], preferred_element_type=F32)

    qn = _rms(seg(wa_ref, 0, n_qlat), gq_ref[...]).astype(BF16)
    qa = jnp.dot(qn, wq_ref[...], preferred_element_type=F32)
    hw = MLA_NOPE + LANES
    for i in range(mla_heads):
        qmla_ref[:, i * hw:i * hw + MLA_NOPE] = qa[:, i * hw:i * hw + MLA_NOPE].astype(BF16)
        rp = qa[:, i * hw + MLA_NOPE:(i + 1) * hw]
        if rope:
            rp = _rope_half(rp, c64, s64, MLA_ROPE // 2)
        qmla_ref[:, i * hw + MLA_NOPE:(i + 1) * hw] = rp.astype(BF16)
    cn = _rms(seg(wa_ref, n_qlat, n_qlat + n_ckv), gkv_ref[...])
    ckv_ref[...] = cn
    kv_ref[...] = jnp.dot(cn.astype(BF16), wkv_ref[...], preferred_element_type=F32).astype(BF16)
    r = seg(wb_ref, 0, n_qb)
    for i in range(n_qb // HEAD_DIM):
        y = _rms(r[:, i * HEAD_DIM:(i + 1) * HEAD_DIM], gqn_ref[...])
        if rope:
            y = _rope_half(y, c128, s128, HEAD_DIM // 2)
        qb_ref[:, i * HEAD_DIM:(i + 1) * HEAD_DIM] = y.astype(BF16)
    r = seg(wb_ref, n_qb, n_qb + n_kb)
    for i in range(n_kb // HEAD_DIM):
        y = _rms(r[:, i * HEAD_DIM:(i + 1) * HEAD_DIM], gkn_ref[...])
        if rope:
            y = _rope_half(y, c128, s128, HEAD_DIM // 2)
        kb_ref[:, i * HEAD_DIM:(i + 1) * HEAD_DIM] = y
    vb_ref[...] = seg(wb_ref, n_qb + n_kb, n_qb + n_kb + n_vb)
    r = seg(wk_ref, 0, LANES)
    if rope:
        r = _rope_half(r, c64, s64, MLA_ROPE // 2)
    kr_ref[...] = r[:, :MLA_ROPE]
    krp_ref[...] = r.astype(BF16)


def _inproj_ab(x, norm, weights, g_q, g_kv, g_qn, g_kn, grp, dims, tm=256):
    m, d = x.shape
    n_qlat, n_ckv, n_qb, n_kb, n_vb, mla_heads = dims
    g_norm, mod, layer, chunk_sc, chunk_sh = norm
    rope = grp.rope is not None
    one = pl.Buffered(1)
    nq_out = mla_heads * (MLA_NOPE + LANES)
    nkv_out = weights[4].shape[1]
    in_specs = [pl.BlockSpec((tm, d), lambda i: (i, 0))]
    in_specs += grp.norm_specs(layer, chunk_sc, chunk_sh, d, tm)
    in_specs += [pl.BlockSpec(w.shape, lambda i: (0, 0), pipeline_mode=one) for w in weights]
    in_specs += [
        pl.BlockSpec((1, n_qlat), lambda i: (0, 0)),
        pl.BlockSpec((1, n_ckv), lambda i: (0, 0)),
        pl.BlockSpec((1, HEAD_DIM), lambda i: (0, 0)),
        pl.BlockSpec((1, HEAD_DIM), lambda i: (0, 0)),
    ]
    args = [x, g_norm, mod, mod, *weights, g_q, g_kv, g_qn, g_kn]
    if rope:
        per = grp.seq // tm
        for name in ("c128", "s128", "c64", "s64"):
            in_specs.append(pl.BlockSpec((tm, LANES), lambda i: (i % per, 0)))
            args.append(grp.rope[name])
    widths = [(nq_out, BF16), (nkv_out, BF16), (n_ckv, F32), (MLA_ROPE, F32), (LANES, BF16),
              (n_qb, BF16), (n_kb, F32), (n_vb, F32)]
    return pl.pallas_call(
        functools.partial(_inproj_ab_kernel, rope=rope, n_qlat=n_qlat, n_ckv=n_ckv, n_qb=n_qb,
                          n_kb=n_kb, n_vb=n_vb, mla_heads=mla_heads),
        out_shape=[jax.ShapeDtypeStruct((m, w), dt) for w, dt in widths],
        grid=(m // tm,),
        in_specs=in_specs,
        out_specs=[pl.BlockSpec((tm, w), lambda i: (i, 0)) for w, _ in widths],
        compiler_params=_params("arbitrary"),
        name="inproj_ab",
    )(*args)


def _kvup_kernel(c_ref, w_ref, o_ref):
    o_ref[...] = jnp.dot(c_ref[...].astype(BF16), w_ref[...],
                         preferred_element_type=F32).astype(o_ref.dtype)


def _kvup(ckv, wkv, tm=512):
    m, k = ckv.shape
    n = wkv.shape[1]
    return pl.pallas_call(
        _kvup_kernel,
        out_shape=jax.ShapeDtypeStruct((m, n), BF16),
        grid=(m // tm,),
        in_specs=[pl.BlockSpec((tm, k), lambda i: (i, 0)),
                  pl.BlockSpec((k, n), lambda i: (0, 0))],
        out_specs=pl.BlockSpec((tm, n), lambda i: (i, 0)),
        compiler_params=_params("arbitrary"),
        name="mla_kv_up",
    )(ckv, wkv)


def _proj_kernel(*refs, rope, scale, norm):
    if norm:
        x_ref, gn_ref, sc_ref, sh_ref, *rest = refs
        h_ref = rest.pop()

        @pl.when(pl.program_id(1) == 0)
        def _():
            _norm_mod_into(h_ref, x_ref, gn_ref, sc_ref, sh_ref)
    else:
        h_ref, *rest = refs
    if rope:
        w_ref, c_ref, s_ref, o_ref = rest
    else:
        w_ref, o_ref = rest
    r = jnp.dot(h_ref[...], w_ref[...].astype(BF16), preferred_element_type=F32)
    if rope:
        c, s = c_ref[...], s_ref[...]
        for i in range(r.shape[1] // HEAD_DIM):
            sl = slice(i * HEAD_DIM, (i + 1) * HEAD_DIM)
            y = _rope_half(r[:, sl], c, s, HEAD_DIM // 2)
            o_ref[:, sl] = (y if scale is None else y * scale).astype(o_ref.dtype)
    else:
        o_ref[...] = (r if scale is None else r * scale).astype(o_ref.dtype)


def _proj(h, w3, col0, ncols, out_dtype, grp, use_rope, scale=None, norm=None, tm=2048, tn=512):
    m, d = h.shape
    rope = use_rope and grp.rope is not None
    tm = min(tm, m)
    off = col0 // tn
    in_specs = [pl.BlockSpec((tm, d), lambda i, j: (i, 0))]
    args = [h]
    out_shape = [jax.ShapeDtypeStruct((m, ncols), out_dtype)]
    out_specs = [pl.BlockSpec((tm, tn), lambda i, j: (i, j))]
    if norm is not None:
        g_norm, mod, layer, chunk_sc, chunk_sh = norm
        in_specs += grp.norm_specs(layer, chunk_sc, chunk_sh, d, tm)
        args += [g_norm, mod, mod]
        out_shape.append(jax.ShapeDtypeStruct((m, d), BF16))
        out_specs.append(pl.BlockSpec((tm, d), lambda i, j: (i, 0)))
    in_specs.append(pl.BlockSpec((None, d, tn), lambda i, j: (0, 0, off + j)))
    args.append(w3)
    if rope:
        assert tm % grp.seq == 0
        for name in ("c128", "s128"):
            in_specs.append(pl.BlockSpec((tm, LANES), lambda i, j: (0, 0)))
            args.append(jnp.tile(grp.rope[name], (tm // grp.seq, 1)))
    res = pl.pallas_call(
        functools.partial(_proj_kernel, rope=rope, scale=scale, norm=norm is not None),
        out_shape=out_shape,
        grid=(m // tm, ncols // tn),
        in_specs=in_specs,
        out_specs=out_specs,
        compiler_params=_params("arbitrary", "arbitrary"),
        name="proj_c",
    )(*args)
    return res if norm is not None else res[0]


def _softmax_parts(scores):
    mx = jnp.max(scores[0], axis=-1, keepdims=True)
    for s in scores[1:]:
        mx = jnp.maximum(mx, jnp.max(s, axis=-1, keepdims=True))
    ps = [jnp.exp2(s - mx) for s in scores]
    l = jnp.sum(ps[0], axis=-1, keepdims=True)
    for p in ps[1:]:
        l = l + jnp.sum(p, axis=-1, keepdims=True)
    return ps, l


def _attend(q, ks, vs):
    scores = [lax.dot_general(q, k, _NT, preferred_element_type=F32) for k in ks]
    ps, l = _softmax_parts(scores)
    o = jnp.dot(ps[0].astype(BF16), vs[0], preferred_element_type=F32)
    for p, v in zip(ps[1:], vs[1:]):
        o = o + jnp.dot(p.astype(BF16), v, preferred_element_type=F32)
    return o, l


_NT = (((1,), (1,)), ((), ()))


def _mla_kernel(*refs, n_seg, hg):
    q_ref = refs[0]
    o_ref = refs[-1]
    hw = MLA_NOPE + LANES
    for h in range(hg):
        ks, vs = [], []
        for i in range(n_seg):
            kv_ref, kr_ref = refs[1 + 2 * i], refs[2 + 2 * i]
            ks.append(jnp.concatenate([kv_ref[:, h * hw:h * hw + MLA_NOPE], kr_ref[...]], axis=1))
            vs.append(kv_ref[:, h * hw + MLA_NOPE:(h + 1) * hw])
        o, l = _attend(q_ref[:, h * hw:(h + 1) * hw], ks, vs)
        o_ref[:, h * MLA_V:(h + 1) * MLA_V] = (o / l).astype(o_ref.dtype)


def _mla_attention(q, segs, grp, heads, tq, hg):
    m = q.shape[0]
    nq = grp.seq // tq
    hw = MLA_NOPE + LANES
    in_specs = [pl.BlockSpec((tq, hg * hw), lambda b, h, i: (b * nq + i, h))]
    args = [q]
    for kv, kr, rows in segs:
        in_specs.append(pl.BlockSpec((rows, hg * (MLA_NOPE + MLA_V)), lambda b, h, i: (b, h)))
        in_specs.append(pl.BlockSpec((rows, LANES), lambda b, h, i: (b, 0)))
        args += [kv, kr]
    return pl.pallas_call(
        functools.partial(_mla_kernel, n_seg=len(segs), hg=hg),
        out_shape=jax.ShapeDtypeStruct((m, heads * MLA_V), BF16),
        grid=(grp.batch, heads // hg, nq),
        in_specs=in_specs,
        out_specs=pl.BlockSpec((tq, hg * MLA_V), lambda b, h, i: (b * nq + i, h)),
        compiler_params=_params("arbitrary", "arbitrary", "arbitrary"),
        name="attn_mla",
    )(*args)


def _gqa_kernel(*refs, n_seg, group, kvg):
    q_ref = refs[0]
    o_ref = refs[-1]
    for kh in range(kvg):
        ksl = slice(kh * HEAD_DIM, (kh + 1) * HEAD_DIM)
        ks = [refs[1 + 2 * i][:, ksl].astype(BF16) for i in range(n_seg)]
        vs = [refs[2 + 2 * i][:, ksl].astype(BF16) for i in range(n_seg)]
        for g in range(group):
            sl = slice((kh * group + g) * HEAD_DIM, (kh * group + g + 1) * HEAD_DIM)
            o, l = _attend(q_ref[:, sl], ks, vs)
            o_ref[:, sl] = (o / l).astype(o_ref.dtype)


def _gqa_attention(q, segs, grp, heads, kv_heads, tq, kvg):
    m = q.shape[0]
    nq = grp.seq // tq
    group = heads // kv_heads
    gw = kvg * group * HEAD_DIM
    in_specs = [pl.BlockSpec((tq, gw), lambda b, h, i: (b * nq + i, h))]
    args = [q]
    for k, v, rows in segs:
        in_specs.append(pl.BlockSpec((rows, kvg * HEAD_DIM), lambda b, h, i: (b, h)))
        in_specs.append(pl.BlockSpec((rows, kvg * HEAD_DIM), lambda b, h, i: (b, h)))
        args += [k, v]
    return pl.pallas_call(
        functools.partial(_gqa_kernel, n_seg=len(segs), group=group, kvg=kvg),
        out_shape=jax.ShapeDtypeStruct((m, heads * HEAD_DIM), BF16),
        grid=(grp.batch, kv_heads // kvg, nq),
        in_specs=in_specs,
        out_specs=pl.BlockSpec((tq, gw), lambda b, h, i: (b * nq + i, h)),
        compiler_params=_params("arbitrary", "arbitrary", "arbitrary"),
        name="attn_gqa",
    )(*args)


def _diff_kernel(*refs, n_seg, hg, lambda_init):
    q_ref, lq1_ref, lk1_ref, lq2_ref, lk2_ref, g_ref = refs[:6]
    o_ref = refs[-1]
    lam = (jnp.exp(jnp.sum(lq1_ref[...] * lk1_ref[...], axis=-1, keepdims=True))
           - jnp.exp(jnp.sum(lq2_ref[...] * lk2_ref[...], axis=-1, keepdims=True)) + lambda_init)
    hw = 2 * HEAD_DIM
    for h in range(hg):
        vs = [refs[7 + 2 * i][:, h * hw:(h + 1) * hw].astype(BF16) for i in range(n_seg)]
        halves = []
        for j in range(2):
            sl = slice(h * hw + j * HEAD_DIM, h * hw + (j + 1) * HEAD_DIM)
            ks = [refs[6 + 2 * i][:, sl].astype(BF16) for i in range(n_seg)]
            o, l = _attend(q_ref[:, sl], ks, vs)
            halves.append(o * (1.0 / l))
        o = halves[0] - lam * halves[1]
        o_ref[:, h * hw:(h + 1) * hw] = (_rms(o, g_ref[...]) * (1.0 - lambda_init)).astype(o_ref.dtype)


def _diff_attention(q, segs, lams, g_sub, grp, heads, tq, hg, lambda_init):
    m = q.shape[0]
    nq = grp.seq // tq
    hw = 2 * HEAD_DIM
    vec = lambda w: pl.BlockSpec((1, w), lambda b, h, i: (0, 0))
    in_specs = [pl.BlockSpec((tq, hg * hw), lambda b, h, i: (b * nq + i, h))]
    in_specs += [vec(HEAD_DIM)] * 4 + [vec(hw)]
    args = [q, *lams, g_sub]
    for k, v, rows in segs:
        in_specs.append(pl.BlockSpec((rows, hg * hw), lambda b, h, i: (b, h)))
        in_specs.append(pl.BlockSpec((rows, hg * hw), lambda b, h, i: (b, h)))
        args += [k, v]
    return pl.pallas_call(
        functools.partial(_diff_kernel, n_seg=len(segs), hg=hg, lambda_init=lambda_init),
        out_shape=jax.ShapeDtypeStruct((m, heads * hw), BF16),
        grid=(grp.batch, heads // hg, nq),
        in_specs=in_specs,
        out_specs=pl.BlockSpec((tq, hg * hw), lambda b, h, i: (b * nq + i, h)),
        compiler_params=_params("arbitrary", "arbitrary", "arbitrary"),
        name="attn_diff",
    )(*args)


def _mm_res_kernel(*refs, n_a, nk, nb, out_norm):
    a_refs = refs[:n_a]
    w_refs = refs[n_a:2 * n_a]
    x_ref, gt_ref = refs[2 * n_a:2 * n_a + 2]
    o_ref = refs[-1]
    seg = o_ref.shape[0] // nb
    if nk == 1:
        ws = [w_ref[...].astype(BF16) for w_ref in w_refs]
        for r in range(nb):
            rows = slice(r * seg, (r + 1) * seg)
            part = None
            for a_ref, w in zip(a_refs, ws):
                p = jnp.dot(a_ref[rows, :], w, preferred_element_type=F32)
                part = p if part is None else part + p
            o_ref[rows, :] = x_ref[rows, :] + gt_ref[r] * part
        return
    k = pl.program_id(2)

    def step(first):
        w32 = w_refs[0][...]
        for r in range(nb):
            rows = slice(r * seg, (r + 1) * seg)
            w = (w32 * gt_ref[r]).astype(BF16)
            p = jnp.dot(a_refs[0][rows, :], w, preferred_element_type=F32)
            if first:
                o_ref[rows, :] = x_ref[rows, :] + p
            else:
                o_ref[rows, :] += p

    @pl.when(k == 0)
    def _():
        step(True)

    @pl.when(k > 0)
    def _():
        step(False)

    if out_norm:
        gf_ref = refs[2 * n_a + 2]

        @pl.when(k == nk - 1)
        def _():
            for r0 in range(0, o_ref.shape[0], NORM_ROWS):
                rows = slice(r0, r0 + NORM_ROWS)
                o_ref[rows, :] = _rms(o_ref[rows, :], gf_ref[...])


def _mm_res(a_list, w3, w_layer, x, mod, layer, chunk, grp, tk, tm=2048, tn=512, out_gain=None):
    m, n = x.shape
    if out_gain is not None:
        tn = n
    n_a = len(a_list)
    if a_list[0].ndim == 3:
        assert n_a == 1 and a_list[0].shape[2] == tk
        ka = a_list[0].shape[0] * tk
        in_specs = [pl.BlockSpec((None, tm, tk), lambda i, j, k: (k, i, 0))]
    else:
        ka = a_list[0].shape[1]
        if n_a > 1:
            tk = ka
        in_specs = [pl.BlockSpec((tm, tk), lambda i, j, k: (i, k)) for _ in a_list]
    nk = ka // tk
    ncol = n // tn
    for idx in range(n_a):
        in_specs.append(pl.BlockSpec((None, tk, tn),
                                     lambda i, j, k, idx=idx: (w_layer, idx * (ka // tk) + k, j)))
    in_specs.append(pl.BlockSpec((tm, tn), lambda i, j, k: (i, j)))
    in_specs.append(grp.mod_spec(layer, lambda i, j, k: chunk * ncol + j, tn, tm))
    args = [*a_list, *([w3] * n_a), x, mod]
    if out_gain is not None:
        assert nk > 1 and ncol == 1
        in_specs.append(pl.BlockSpec((1, n), lambda i, j, k: (0, 0)))
        args.append(out_gain.reshape(1, n))
    return pl.pallas_call(
        functools.partial(_mm_res_kernel, n_a=n_a, nk=nk, nb=grp.mod_rows(tm),
                          out_norm=out_gain is not None),
        out_shape=jax.ShapeDtypeStruct((m, n), F32),
        grid=(m // tm, ncol, nk),
        in_specs=in_specs,
        out_specs=pl.BlockSpec((tm, tn), lambda i, j, k: (i, j)),
        compiler_params=_params("arbitrary", "arbitrary", "arbitrary"),
        name="mm_res",
    )(*args)


def _ffn_up_kernel(x_ref, gn_ref, sc_ref, sh_ref, wg_ref, wv_ref, cg_ref, cv_ref, bg_ref, bv_ref,
                   o_ref, h_ref, *, seq):
    @pl.when(pl.program_id(1) == 0)
    def _():
        _norm_mod_into(h_ref, x_ref, gn_ref, sc_ref, sh_ref)

    h = h_ref[...]
    tm, tn = o_ref.shape
    sub = lax.broadcasted_iota(jnp.int32, (FFN_EDGE_ROWS, 1), 0)
    edges = sorted({r for b in range(0, tm, seq) for r in (b, b + seq - FFN_EDGE_ROWS)})
    for c0 in range(0, tn, FFN_SUB_COLS):
        cs = slice(c0, c0 + FFN_SUB_COLS)

        def taps(w_ref):
            u = jnp.dot(h, w_ref[:, cs].astype(BF16), preferred_element_type=F32)
            return pltpu.roll(u, 1, 0), u, pltpu.roll(u, tm - 1, 0)

        def conv(t, c_ref, b_ref):
            return t[0] * c_ref[0:1, cs] + t[1] * c_ref[1:2, cs] + t[2] * c_ref[2:3, cs] + b_ref[:, cs]

        tg, tv = taps(wg_ref), taps(wv_ref)
        o_ref[:, cs] = (_silu(conv(tg, cg_ref, bg_ref)) * conv(tv, cv_ref, bv_ref)).astype(o_ref.dtype)
        for r0 in edges:
            rows = slice(r0, r0 + FFN_EDGE_ROWS)

            def edge(t):
                prev, cur, nxt = (a[rows, :] for a in t)
                if r0 % seq == 0:
                    prev = jnp.where(sub == 0, 0.0, prev)
                if (r0 + FFN_EDGE_ROWS) % seq == 0:
                    nxt = jnp.where(sub == FFN_EDGE_ROWS - 1, 0.0, nxt)
                return prev, cur, nxt

            o_ref[rows, cs] = (_silu(conv(edge(tg), cg_ref, bg_ref))
                               * conv(edge(tv), cv_ref, bv_ref)).astype(o_ref.dtype)


def _ffn_up(x, norm, w_up, w_conv, b_conv, layer, grp, tm=1024, tn=512):
    m, d = x.shape
    d_ff = w_up.shape[2] // 2
    nj = d_ff // tn
    assert tm % grp.seq == 0 and m % tm == 0
    g_norm, mod, chunk_sc, chunk_sh = norm
    b3 = b_conv.reshape(b_conv.shape[0], 1, b_conv.shape[1])
    return pl.pallas_call(
        functools.partial(_ffn_up_kernel, seq=grp.seq),
        out_shape=jax.ShapeDtypeStruct((nj, m, tn), BF16),
        grid=(m // tm, nj),
        in_specs=[
            pl.BlockSpec((tm, d), lambda i, j: (i, 0)),
            *grp.norm_specs(layer, chunk_sc, chunk_sh, d, tm),
            pl.BlockSpec((None, d, tn), lambda i, j: (layer, 0, j)),
            pl.BlockSpec((None, d, tn), lambda i, j: (layer, 0, nj + j)),
            pl.BlockSpec((None, w_conv.shape[1], tn), lambda i, j: (layer, 0, j)),
            pl.BlockSpec((None, w_conv.shape[1], tn), lambda i, j: (layer, 0, nj + j)),
            pl.BlockSpec((None, 1, tn), lambda i, j: (layer, 0, j)),
            pl.BlockSpec((None, 1, tn), lambda i, j: (layer, 0, nj + j)),
        ],
        out_specs=pl.BlockSpec((None, tm, tn), lambda i, j: (j, i, 0)),
        scratch_shapes=[pltpu.VMEM((tm, d), BF16)],
        compiler_params=_params("arbitrary", "arbitrary"),
        name="ffn_up",
    )(x, g_norm, mod, mod, w_up, w_up, w_conv, w_conv, b3, b3)


def _rope_tables(n_tokens, rot_dim):
    n_rows = n_tokens // GRID_W
    row = jnp.repeat(jnp.arange(n_rows), GRID_W).astype(F32)
    col = jnp.tile(jnp.arange(GRID_W), n_rows).astype(F32)
    n_freq = rot_dim // 4
    freqs = ROPE_BASE ** (-jnp.arange(n_freq, dtype=F32) / n_freq)
    ang = jnp.concatenate([row[:, None] * freqs, col[:, None] * freqs], axis=-1)
    cos, sin = jnp.cos(ang), jnp.sin(ang)
    reps = LANES // rot_dim
    c = jnp.tile(jnp.concatenate([cos, cos], axis=-1), (1, reps))
    s = jnp.tile(jnp.concatenate([-sin, sin], axis=-1), (1, reps))
    return c, s


def _trunk(x, grp, mod, caches, P):
    d = x.shape[1]
    ckv_p = krope_p = kb_p = vb_p = k_c = v_c = None
    depth = P["w_ffn_up"].shape[0]
    for l in range(depth):
        i = l // 2
        norm_mix = (P["g_norm_mix"], mod, l, 1, 0)
        if l % 2 == 0:
            (q_mla, kv, ckv, krope, krope_pad, qb, kb, vb) = _inproj_ab(
                x, norm_mix, [w[i] for w in P["w_ab"]],
                P["g_mla_q"][i:i + 1], P["g_mla_kv"][i:i + 1], P["g_gqa_q"][i:i + 1],
                P["g_gqa_k"][i:i + 1], grp, P["dims_ab"])
            ckv_p, krope_p, kb_p, vb_p = ckv, krope, kb, vb
            heads = P["dims_ab"][5]
            segs_a, segs_b = [], []
            if caches is not None:
                c_ckv, c_kr, c_k, c_v = (cc[i] for cc in caches[:4])
                past = c_ckv.shape[0] // grp.batch
                segs_a.append((_kvup(c_ckv, P["w_ab"][4][i]), c_kr, past))
                segs_b.append((c_k, c_v, past))
            segs_a.append((kv, krope_pad, grp.seq))
            segs_b.append((kb, vb, grp.seq))
            tq = min(grp.seq, 512)
            out_a = _mla_attention(q_mla, segs_a, grp, heads, tq, heads)
            out_b = _gqa_attention(qb, segs_b, grp, qb.shape[1] // HEAD_DIM, GQA_KV_HEADS, tq,
                                   GQA_KV_HEADS)
            x = _mm_res([out_a, out_b], P["w_out_ab"], i, x, mod, l, 2, grp, tk=None)
        else:
            w_in_c = P["w_in_c"][i:i + 1]
            nq = w_in_c.shape[2] // 3
            kv_dtype = F32 if caches is None else BF16
            q, h = _proj(x, w_in_c, 0, nq, BF16, grp, True, scale=HEAD_DIM ** -0.5 * LOG2E,
                         norm=norm_mix, tm=1024)
            k = _proj(h, w_in_c, nq, nq, kv_dtype, grp, True)
            v = _proj(h, w_in_c, 2 * nq, nq, kv_dtype, grp, False)
            k_c, v_c = k, v
            segs = []
            if caches is not None:
                c_k, c_v = (cc[i] for cc in caches[4:])
                segs.append((c_k, c_v, c_k.shape[0] // grp.batch))
            segs.append((k, v, grp.seq))
            lambda_init = 0.8 - 0.6 * math.exp(-0.3 * l)
            lams = [P[n][i:i + 1] for n in ("lambda_q1", "lambda_k1", "lambda_q2", "lambda_k2")]
            dheads = nq // (2 * HEAD_DIM)
            o = _diff_attention(q, segs, lams, P["g_diff_sub"][i:i + 1], grp, dheads,
                                min(grp.seq, 1024), dheads if grp.seq <= 256 else 2, lambda_init)
            x = _mm_res([o], P["w_out_c"], i, x, mod, l, 2, grp, tk=o.shape[1])
        g = _ffn_up(x, (P["g_norm_ffn"], mod, 4, 3), P["w_ffn_up"], P["w_ffn_conv"],
                    P["b_ffn_conv"], l, grp)
        if l + 1 < depth:
            x = _mm_res([g], P["w_ffn_down"], l, x, mod, l, 5, grp, tk=512, tn=1024)
        else:
            x = _mm_res([g], P["w_ffn_down"], l, x, mod, l, 5, grp, tk=512, tm=1024,
                        out_gain=P["g_final"])
    return x, (ckv_p, krope_p, kb_p, vb_p), (k_c, v_c)


def kernel(x_prompt, x_sample, cache_mla_ckv, cache_mla_krope, cache_gqa_k, cache_gqa_v,
           cache_diff_k, cache_diff_v, c, c_ctx, w_ada, b_ada, g_norm_mix, g_norm_ffn,
           w_in_ab, g_mla_q, w_mla_q_up, g_mla_kv, w_mla_kv_up, g_gqa_q, g_gqa_k, w_out_ab,
           w_in_c, lambda_q1, lambda_k1, lambda_q2, lambda_k2, g_diff_sub, w_out_c,
           w_ffn_up, w_ffn_conv, b_ffn_conv, w_ffn_down, g_final):
    bp, tp, d = x_prompt.shape
    bs, ts, _ = x_sample.shape
    depth = w_ada.shape[0]
    n_even = w_in_ab.shape[0]
    assert depth == 2 and n_even == 1 and w_in_c.shape[0] == 1, "one MLA/GQA layer + one diff layer"
    past = cache_mla_ckv.shape[2]
    n_qlat = g_mla_q.shape[1]
    n_ckv = g_mla_kv.shape[1]
    mla_heads = w_mla_q_up.shape[2] // (MLA_NOPE + MLA_ROPE)
    n_kb = cache_gqa_k.shape[3] * HEAD_DIM
    n_vb = n_kb
    n_qb = w_in_ab.shape[2] - n_qlat - n_ckv - MLA_ROPE - n_kb - n_vb

    rows = 8
    cond = jnp.zeros((rows, d), F32).at[:bs].set(c).at[bs].set(c_ctx)
    mod = _modulation(cond, w_ada, b_ada)
    mod = mod.reshape(depth, rows, 1, mod.shape[2])

    o0 = n_qlat + n_ckv
    o1 = o0 + MLA_ROPE
    w_a = w_in_ab[:, :, :o0].astype(BF16)
    w_b = w_in_ab[:, :, o1:].astype(BF16)
    w_k = jnp.pad(w_in_ab[:, :, o0:o1], ((0, 0), (0, 0), (0, LANES - MLA_ROPE))).astype(BF16)
    wq = w_mla_q_up.reshape(n_even, n_qlat, mla_heads, MLA_NOPE + MLA_ROPE)
    wq = wq * ((MLA_NOPE + MLA_ROPE) ** -0.5 * LOG2E)
    w_q_up_p = jnp.pad(wq, ((0, 0), (0, 0), (0, 0), (0, LANES - MLA_ROPE))).reshape(
        n_even, n_qlat, mla_heads * (MLA_NOPE + LANES)).astype(BF16)
    P = dict(
        w_ab=(w_a, w_b, w_k, w_q_up_p, w_mla_kv_up.astype(BF16)),
        g_mla_q=g_mla_q, g_mla_kv=g_mla_kv, g_gqa_q=g_gqa_q * (HEAD_DIM ** -0.5 * LOG2E),
        g_gqa_k=g_gqa_k,
        w_out_ab=w_out_ab, w_in_c=w_in_c, lambda_q1=lambda_q1, lambda_k1=lambda_k1,
        lambda_q2=lambda_q2, lambda_k2=lambda_k2, g_diff_sub=g_diff_sub, w_out_c=w_out_c,
        w_ffn_up=w_ffn_up, w_ffn_conv=w_ffn_conv, b_ffn_conv=b_ffn_conv, w_ffn_down=w_ffn_down,
        g_final=g_final,
        g_norm_mix=g_norm_mix.reshape(depth, 1, d), g_norm_ffn=g_norm_ffn.reshape(depth, 1, d),
        dims_ab=(n_qlat, n_ckv, n_qb, n_kb, n_vb, mla_heads),
    )

    grp_p = _Group(bp, tp, bs, False, None)
    y_p, ab_p, c_p = _trunk(x_prompt.reshape(bp * tp, d), grp_p, mod, None, P)

    c128, s128 = _rope_tables(ts, HEAD_DIM)
    c64, s64 = _rope_tables(ts, MLA_ROPE)
    grp_s = _Group(bs, ts, 0, True, dict(c128=c128, s128=s128, c64=c64, s64=s64))
    n_odd = cache_diff_k.shape[1]
    kr_pad = jnp.pad(cache_mla_krope, ((0, 0), (0, 0), (0, 0), (0, LANES - MLA_ROPE))).astype(BF16)
    per_layer = lambda a, n: [a[:, j].reshape(bs * past, -1) for j in range(n)]
    caches = (per_layer(cache_mla_ckv, n_even), per_layer(kr_pad, n_even),
              per_layer(cache_gqa_k, n_even), per_layer(cache_gqa_v, n_even),
              per_layer(cache_diff_k.astype(BF16), n_odd),
              per_layer(cache_diff_v.astype(BF16), n_odd))
    y_s, _, _ = _trunk(x_sample.reshape(bs * ts, d), grp_s, mod, caches, P)

    ckv, krope, kb, vb = ab_p
    k_c, v_c = c_p
    kvh = n_kb // HEAD_DIM
    dh = k_c.shape[1] // (2 * HEAD_DIM)
    return (y_p.reshape(bp, tp, d), y_s.reshape(bs, ts, d),
            ckv.reshape(bp, 1, tp, n_ckv), krope.reshape(bp, 1, tp, MLA_ROPE),
            kb.reshape(bp, 1, tp, kvh, HEAD_DIM), vb.reshape(bp, 1, tp, kvh, HEAD_DIM),
            k_c.reshape(bp, 1, tp, dh, 2, HEAD_DIM), v_c.reshape(bp, 1, tp, dh, 2 * HEAD_DIM))
```

```python
import functools
import math

import jax
import jax.numpy as jnp
from jax import lax
from jax.experimental import pallas as pl
from jax.experimental.pallas import tpu as pltpu

F32 = jnp.float32
BF16 = jnp.bfloat16

HEAD_DIM = 128
GRID_W = 64
ROPE_BASE = 10000.0
EPS = 1e-6
MLA_NOPE = 128
MLA_ROPE = 64
MLA_V = 128
GQA_KV_HEADS = 2
LOG2E = 1.4426950408889634
LANES = 128
FFN_EDGE_ROWS = 16
NORM_ROWS = 64
FFN_SUB_COLS = 256
W_RING = 3
VMEM_LIMIT = 56 * 1024 * 1024


def _params(*sem):
    return pltpu.CompilerParams(dimension_semantics=sem, vmem_limit_bytes=VMEM_LIMIT)


def _rms(x, g):
    ms = jnp.mean(x * x, axis=-1, keepdims=True)
    return x * lax.rsqrt(ms + EPS) * g


def _silu(x):
    return x / (1.0 + jnp.exp(-x))


def _rope_half(x, c, s, half):
    if 2 * half == LANES:
        sw = pltpu.roll(x, half, 1)
    else:
        lane = lax.broadcasted_iota(jnp.int32, x.shape, 1)
        first = (lane % (2 * half)) < half
        sw = jnp.where(first, pltpu.roll(x, LANES - half, 1), pltpu.roll(x, half, 1))
    return x * c + sw * s


def _mod_kernel(cond_ref, w_ref, b_ref, o_ref):
    s = _silu(cond_ref[...]).astype(BF16)
    o_ref[...] = jnp.dot(s, w_ref[...].astype(BF16), preferred_element_type=F32) + b_ref[...]


def _modulation(cond, w_ada, b_ada, tn=1024):
    depth, d, n = w_ada.shape
    rows = cond.shape[0]
    return pl.pallas_call(
        _mod_kernel,
        out_shape=jax.ShapeDtypeStruct((depth, rows, n), F32),
        grid=(depth, n // tn),
        in_specs=[
            pl.BlockSpec((rows, d), lambda l, j: (0, 0)),
            pl.BlockSpec((None, d, tn), lambda l, j: (l, 0, j)),
            pl.BlockSpec((None, 1, tn), lambda l, j: (l, 0, j)),
        ],
        out_specs=pl.BlockSpec((None, rows, tn), lambda l, j: (l, 0, j)),
        compiler_params=_params("arbitrary", "arbitrary"),
        name="adaln_mod",
    )(cond, w_ada, b_ada.reshape(depth, 1, n))


def _norm_mod_tile(x_ref, g_ref, sc_ref, sh_ref):
    y = _rms(x_ref[...], g_ref[...])
    return (y * (1.0 + sc_ref[...]) + sh_ref[...]).astype(BF16)


def _norm_mod_into(h_ref, x_ref, g_ref, sc_ref, sh_ref):
    g, sc1, sh = g_ref[...], 1.0 + sc_ref[...], sh_ref[...]
    for r0 in range(0, x_ref.shape[0], NORM_ROWS):
        rows = slice(r0, r0 + NORM_ROWS)
        h_ref[rows, :] = (_rms(x_ref[rows, :], g) * sc1 + sh).astype(BF16)


class _Group:
    def __init__(self, batch, seq, mod_row0, per_batch_mod, rope):
        self.batch = batch
        self.seq = seq
        self.m = batch * seq
        self.mod_row0 = mod_row0
        self.per_batch_mod = per_batch_mod
        self.rope = rope

    def mod_rows(self, tm):
        if self.per_batch_mod and tm > self.seq:
            assert tm % self.seq == 0 and self.mod_row0 % (tm // self.seq) == 0
            return tm // self.seq
        return 1

    def mod_spec(self, layer, col, width, tm):
        nb = self.mod_rows(tm)
        if self.per_batch_mod:
            assert tm % self.seq == 0 or self.seq % tm == 0
            row = lambda i: (self.mod_row0 + (i * tm) // self.seq) // nb
        else:
            row = lambda i: self.mod_row0
        col_fn = col if callable(col) else (lambda *idx: col)
        return pl.BlockSpec((None, nb, 1, width),
                            lambda *idx: (layer, row(idx[0]), 0, col_fn(*idx)))

    def norm_specs(self, layer, chunk_sc, chunk_sh, d, tm):
        assert self.mod_rows(tm) == 1
        vec = lambda chunk: pl.BlockSpec(
            (None, None, 1, d), self.mod_spec(layer, chunk, d, tm).index_map)
        return [pl.BlockSpec((None, 1, d), lambda *idx: (layer, 0, 0)), vec(chunk_sc), vec(chunk_sh)]


def _inproj_ab_kernel(*refs, rope, n_qlat, n_ckv, n_qb, n_kb, n_vb, mla_heads):
    (x_ref, gn_ref, sc_ref, sh_ref, wa_ref, wb_ref, wk_ref, wq_ref, wkv_ref,
     gq_ref, gkv_ref, gqn_ref, gkn_ref) = refs[:13]
    (qmla_ref, kv_ref, ckv_ref, kr_ref, krp_ref, qb_ref, kb_ref, vb_ref) = refs[-8:]
    if rope:
        c128, s128, c64, s64 = (r[...] for r in refs[13:17])
    h = _norm_mod_tile(x_ref, gn_ref, sc_ref, sh_ref)

    def seg(w_ref, a, b):
        return jnp.dot(h, w_ref[:, a:b], preferred_element_type=F32)

    qn = _rms(seg(wa_ref, 0, n_qlat), gq_ref[...]).astype(BF16)
    qa = jnp.dot(qn, wq_ref[...], preferred_element_type=F32)
    hw = MLA_NOPE + LANES
    for i in range(mla_heads):
        qmla_ref[:, i * hw:i * hw + MLA_NOPE] = qa[:, i * hw:i * hw + MLA_NOPE].astype(BF16)
        rp = qa[:, i * hw + MLA_NOPE:(i + 1) * hw]
        if rope:
            rp = _rope_half(rp, c64, s64, MLA_ROPE // 2)
        qmla_ref[:, i * hw + MLA_NOPE:(i + 1) * hw] = rp.astype(BF16)
    cn = _rms(seg(wa_ref, n_qlat, n_qlat + n_ckv), gkv_ref[...])
    ckv_ref[...] = cn
    kv_ref[...] = jnp.dot(cn.astype(BF16), wkv_ref[...], preferred_element_type=F32).astype(BF16)
    r = seg(wb_ref, 0, n_qb)
    for i in range(n_qb // HEAD_DIM):
        y = _rms(r[:, i * HEAD_DIM:(i + 1) * HEAD_DIM], gqn_ref[...])
        if rope:
            y = _rope_half(y, c128, s128, HEAD_DIM // 2)
        qb_ref[:, i * HEAD_DIM:(i + 1) * HEAD_DIM] = y.astype(BF16)
    r = seg(wb_ref, n_qb, n_qb + n_kb)
    for i in range(n_kb // HEAD_DIM):
        y = _rms(r[:, i * HEAD_DIM:(i + 1) * HEAD_DIM], gkn_ref[...])
        if rope:
            y = _rope_half(y, c128, s128, HEAD_DIM // 2)
        kb_ref[:, i * HEAD_DIM:(i + 1) * HEAD_DIM] = y
    vb_ref[...] = seg(wb_ref, n_qb + n_kb, n_qb + n_kb + n_vb)
    r = seg(wk_ref, 0, LANES)
    if rope:
        r = _rope_half(r, c64, s64, MLA_ROPE // 2)
    kr_ref[...] = r[:, :MLA_ROPE]
    krp_ref[...] = r.astype(BF16)


def _inproj_ab(x, norm, weights, g_q, g_kv, g_qn, g_kn, grp, dims, tm=256):
    m, d = x.shape
    n_qlat, n_ckv, n_qb, n_kb, n_vb, mla_heads = dims
    g_norm, mod, layer, chunk_sc, chunk_sh = norm
    rope = grp.rope is not None
    one = pl.Buffered(1)
    nq_out = mla_heads * (MLA_NOPE + LANES)
    nkv_out = weights[4].shape[1]
    in_specs = [pl.BlockSpec((tm, d), lambda i: (i, 0))]
    in_specs += grp.norm_specs(layer, chunk_sc, chunk_sh, d, tm)
    in_specs += [pl.BlockSpec(w.shape, lambda i: (0, 0), pipeline_mode=one) for w in weights]
    in_specs += [
        pl.BlockSpec((1, n_qlat), lambda i: (0, 0)),
        pl.BlockSpec((1, n_ckv), lambda i: (0, 0)),
        pl.BlockSpec((1, HEAD_DIM), lambda i: (0, 0)),
        pl.BlockSpec((1, HEAD_DIM), lambda i: (0, 0)),
    ]
    args = [x, g_norm, mod, mod, *weights, g_q, g_kv, g_qn, g_kn]
    if rope:
        per = grp.seq // tm
        for name in ("c128", "s128", "c64", "s64"):
            in_specs.append(pl.BlockSpec((tm, LANES), lambda i: (i % per, 0)))
            args.append(grp.rope[name])
    widths = [(nq_out, BF16), (nkv_out, BF16), (n_ckv, F32), (MLA_ROPE, F32), (LANES, BF16),
              (n_qb, BF16), (n_kb, F32), (n_vb, F32)]
    return pl.pallas_call(
        functools.partial(_inproj_ab_kernel, rope=rope, n_qlat=n_qlat, n_ckv=n_ckv, n_qb=n_qb,
                          n_kb=n_kb, n_vb=n_vb, mla_heads=mla_heads),
        out_shape=[jax.ShapeDtypeStruct((m, w), dt) for w, dt in widths],
        grid=(m // tm,),
        in_specs=in_specs,
        out_specs=[pl.BlockSpec((tm, w), lambda i: (i, 0)) for w, _ in widths],
        compiler_params=_params("arbitrary"),
        name="inproj_ab",
    )(*args)


def _kvup_kernel(c_ref, w_ref, o_ref):
    o_ref[...] = jnp.dot(c_ref[...].astype(BF16), w_ref[...],
                         preferred_element_type=F32).astype(o_ref.dtype)


def _kvup(ckv, wkv, tm=512):
    m, k = ckv.shape
    n = wkv.shape[1]
    return pl.pallas_call(
        _kvup_kernel,
        out_shape=jax.ShapeDtypeStruct((m, n), BF16),
        grid=(m // tm,),
        in_specs=[pl.BlockSpec((tm, k), lambda i: (i, 0)),
                  pl.BlockSpec((k, n), lambda i: (0, 0))],
        out_specs=pl.BlockSpec((tm, n), lambda i: (i, 0)),
        compiler_params=_params("arbitrary"),
        name="mla_kv_up",
    )(ckv, wkv)


def _proj_kernel(*refs, rope, scale, norm):
    if norm:
        x_ref, gn_ref, sc_ref, sh_ref, *rest = refs
        h_ref = rest.pop()

        @pl.when(pl.program_id(1) == 0)
        def _():
            _norm_mod_into(h_ref, x_ref, gn_ref, sc_ref, sh_ref)
    else:
        h_ref, *rest = refs
    if rope:
        w_ref, c_ref, s_ref, o_ref = rest
    else:
        w_ref, o_ref = rest
    r = jnp.dot(h_ref[...], w_ref[...].astype(BF16), preferred_element_type=F32)
    if rope:
        c, s = c_ref[...], s_ref[...]
        for i in range(r.shape[1] // HEAD_DIM):
            sl = slice(i * HEAD_DIM, (i + 1) * HEAD_DIM)
            y = _rope_half(r[:, sl], c, s, HEAD_DIM // 2)
            o_ref[:, sl] = (y if scale is None else y * scale).astype(o_ref.dtype)
    else:
        o_ref[...] = (r if scale is None else r * scale).astype(o_ref.dtype)


def _proj(h, w3, col0, ncols, out_dtype, grp, use_rope, scale=None, norm=None, tm=2048, tn=512):
    m, d = h.shape
    rope = use_rope and grp.rope is not None
    tm = min(tm, m)
    off = col0 // tn
    in_specs = [pl.BlockSpec((tm, d), lambda i, j: (i, 0))]
    args = [h]
    out_shape = [jax.ShapeDtypeStruct((m, ncols), out_dtype)]
    out_specs = [pl.BlockSpec((tm, tn), lambda i, j: (i, j))]
    if norm is not None:
        g_norm, mod, layer, chunk_sc, chunk_sh = norm
        in_specs += grp.norm_specs(layer, chunk_sc, chunk_sh, d, tm)
        args += [g_norm, mod, mod]
        out_shape.append(jax.ShapeDtypeStruct((m, d), BF16))
        out_specs.append(pl.BlockSpec((tm, d), lambda i, j: (i, 0)))
    in_specs.append(pl.BlockSpec((None, d, tn), lambda i, j: (0, 0, off + j)))
    args.append(w3)
    if rope:
        assert tm % grp.seq == 0
        for name in ("c128", "s128"):
            in_specs.append(pl.BlockSpec((tm, LANES), lambda i, j: (0, 0)))
            args.append(jnp.tile(grp.rope[name], (tm // grp.seq, 1)))
    res = pl.pallas_call(
        functools.partial(_proj_kernel, rope=rope, scale=scale, norm=norm is not None),
        out_shape=out_shape,
        grid=(m // tm, ncols // tn),
        in_specs=in_specs,
        out_specs=out_specs,
        compiler_params=_params("arbitrary", "arbitrary"),
        name="proj_c",
    )(*args)
    return res if norm is not None else res[0]


def _softmax_parts(scores):
    mx = jnp.max(scores[0], axis=-1, keepdims=True)
    for s in scores[1:]:
        mx = jnp.maximum(mx, jnp.max(s, axis=-1, keepdims=True))
    ps = [jnp.exp2(s - mx) for s in scores]
    l = jnp.sum(ps[0], axis=-1, keepdims=True)
    for p in ps[1:]:
        l = l + jnp.sum(p, axis=-1, keepdims=True)
    return ps, l


def _attend(q, ks, vs):
    scores = [lax.dot_general(q, k, _NT, preferred_element_type=F32) for k in ks]
    ps, l = _softmax_parts(scores)
    o = jnp.dot(ps[0].astype(BF16), vs[0], preferred_element_type=F32)
    for p, v in zip(ps[1:], vs[1:]):
        o = o + jnp.dot(p.astype(BF16), v, preferred_element_type=F32)
    return o, l


_NT = (((1,), (1,)), ((), ()))


def _mla_kernel(*refs, n_seg, hg):
    q_ref = refs[0]
    o_ref = refs[-1]
    hw = MLA_NOPE + LANES
    for h in range(hg):
        ks, vs = [], []
        for i in range(n_seg):
            kv_ref, kr_ref = refs[1 + 2 * i], refs[2 + 2 * i]
            ks.append(jnp.concatenate([kv_ref[:, h * hw:h * hw + MLA_NOPE], kr_ref[...]], axis=1))
            vs.append(kv_ref[:, h * hw + MLA_NOPE:(h + 1) * hw])
        o, l = _attend(q_ref[:, h * hw:(h + 1) * hw], ks, vs)
        o_ref[:, h * MLA_V:(h + 1) * MLA_V] = (o / l).astype(o_ref.dtype)


def _mla_attention(q, segs, grp, heads, tq, hg):
    m = q.shape[0]
    nq = grp.seq // tq
    hw = MLA_NOPE + LANES
    in_specs = [pl.BlockSpec((tq, hg * hw), lambda b, h, i: (b * nq + i, h))]
    args = [q]
    for kv, kr, rows in segs:
        in_specs.append(pl.BlockSpec((rows, hg * (MLA_NOPE + MLA_V)), lambda b, h, i: (b, h)))
        in_specs.append(pl.BlockSpec((rows, LANES), lambda b, h, i: (b, 0)))
        args += [kv, kr]
    return pl.pallas_call(
        functools.partial(_mla_kernel, n_seg=len(segs), hg=hg),
        out_shape=jax.ShapeDtypeStruct((m, heads * MLA_V), BF16),
        grid=(grp.batch, heads // hg, nq),
        in_specs=in_specs,
        out_specs=pl.BlockSpec((tq, hg * MLA_V), lambda b, h, i: (b * nq + i, h)),
        compiler_params=_params("arbitrary", "arbitrary", "arbitrary"),
        name="attn_mla",
    )(*args)


def _gqa_kernel(*refs, n_seg, group, kvg):
    q_ref = refs[0]
    o_ref = refs[-1]
    for kh in range(kvg):
        ksl = slice(kh * HEAD_DIM, (kh + 1) * HEAD_DIM)
        ks = [refs[1 + 2 * i][:, ksl].astype(BF16) for i in range(n_seg)]
        vs = [refs[2 + 2 * i][:, ksl].astype(BF16) for i in range(n_seg)]
        for g in range(group):
            sl = slice((kh * group + g) * HEAD_DIM, (kh * group + g + 1) * HEAD_DIM)
            o, l = _attend(q_ref[:, sl], ks, vs)
            o_ref[:, sl] = (o / l).astype(o_ref.dtype)


def _gqa_attention(q, segs, grp, heads, kv_heads, tq, kvg):
    m = q.shape[0]
    nq = grp.seq // tq
    group = heads // kv_heads
    gw = kvg * group * HEAD_DIM
    in_specs = [pl.BlockSpec((tq, gw), lambda b, h, i: (b * nq + i, h))]
    args = [q]
    for k, v, rows in segs:
        in_specs.append(pl.BlockSpec((rows, kvg * HEAD_DIM), lambda b, h, i: (b, h)))
        in_specs.append(pl.BlockSpec((rows, kvg * HEAD_DIM), lambda b, h, i: (b, h)))
        args += [k, v]
    return pl.pallas_call(
        functools.partial(_gqa_kernel, n_seg=len(segs), group=group, kvg=kvg),
        out_shape=jax.ShapeDtypeStruct((m, heads * HEAD_DIM), BF16),
        grid=(grp.batch, kv_heads // kvg, nq),
        in_specs=in_specs,
        out_specs=pl.BlockSpec((tq, gw), lambda b, h, i: (b * nq + i, h)),
        compiler_params=_params("arbitrary", "arbitrary", "arbitrary"),
        name="attn_gqa",
    )(*args)


def _diff_kernel(*refs, n_seg, hg, lambda_init):
    q_ref, lq1_ref, lk1_ref, lq2_ref, lk2_ref, g_ref = refs[:6]
    o_ref = refs[-1]
    lam = (jnp.exp(jnp.sum(lq1_ref[...] * lk1_ref[...], axis=-1, keepdims=True))
           - jnp.exp(jnp.sum(lq2_ref[...] * lk2_ref[...], axis=-1, keepdims=True)) + lambda_init)
    hw = 2 * HEAD_DIM
    for h in range(hg):
        vs = [refs[7 + 2 * i][:, h * hw:(h + 1) * hw].astype(BF16) for i in range(n_seg)]
        halves = []
        for j in range(2):
            sl = slice(h * hw + j * HEAD_DIM, h * hw + (j + 1) * HEAD_DIM)
            ks = [refs[6 + 2 * i][:, sl].astype(BF16) for i in range(n_seg)]
            o, l = _attend(q_ref[:, sl], ks, vs)
            halves.append(o * (1.0 / l))
        o = halves[0] - lam * halves[1]
        o_ref[:, h * hw:(h + 1) * hw] = (_rms(o, g_ref[...]) * (1.0 - lambda_init)).astype(o_ref.dtype)


def _diff_attention(q, segs, lams, g_sub, grp, heads, tq, hg, lambda_init):
    m = q.shape[0]
    nq = grp.seq // tq
    hw = 2 * HEAD_DIM
    vec = lambda w: pl.BlockSpec((1, w), lambda b, h, i: (0, 0))
    in_specs = [pl.BlockSpec((tq, hg * hw), lambda b, h, i: (b * nq + i, h))]
    in_specs += [vec(HEAD_DIM)] * 4 + [vec(hw)]
    args = [q, *lams, g_sub]
    for k, v, rows in segs:
        in_specs.append(pl.BlockSpec((rows, hg * hw), lambda b, h, i: (b, h)))
        in_specs.append(pl.BlockSpec((rows, hg * hw), lambda b, h, i: (b, h)))
        args += [k, v]
    return pl.pallas_call(
        functools.partial(_diff_kernel, n_seg=len(segs), hg=hg, lambda_init=lambda_init),
        out_shape=jax.ShapeDtypeStruct((m, heads * hw), BF16),
        grid=(grp.batch, heads // hg, nq),
        in_specs=in_specs,
        out_specs=pl.BlockSpec((tq, hg * hw), lambda b, h, i: (b * nq + i, h)),
        compiler_params=_params("arbitrary", "arbitrary", "arbitrary"),
        name="attn_diff",
    )(*args)


def _mm_res_kernel(*refs, n_a, nk, nb, out_norm, w_layer):
    a_refs = refs[:n_a]
    w_refs = refs[n_a:2 * n_a]
    x_ref, gt_ref = refs[2 * n_a:2 * n_a + 2]
    o_ref = refs[2 * n_a + 2 + (1 if out_norm else 0)]
    seg = o_ref.shape[0] // nb
    if nk == 1:
        ws = [w_ref[...].astype(BF16) for w_ref in w_refs]
        for r in range(nb):
            rows = slice(r * seg, (r + 1) * seg)
            part = None
            for a_ref, w in zip(a_refs, ws):
                p = jnp.dot(a_ref[rows, :], w, preferred_element_type=F32)
                part = p if part is None else part + p
            o_ref[rows, :] = x_ref[rows, :] + gt_ref[r] * part
        return
    k = pl.program_id(2)
    w_hbm = w_refs[0]
    wbuf, wsem = refs[-2], refs[-1]
    tk, tn = wbuf.shape[1:]
    ncol, steps = pl.num_programs(1), pl.num_programs(0) * pl.num_programs(1) * nk
    s = (pl.program_id(0) * ncol + pl.program_id(1)) * nk + k

    def w_copy(step):
        slot = step % W_RING
        src = w_hbm.at[w_layer, pl.ds((step % nk) * tk, tk), pl.ds(((step // nk) % ncol) * tn, tn)]
        return pltpu.make_async_copy(src, wbuf.at[slot], wsem.at[slot])

    @pl.when(s == 0)
    def _():
        for ahead in range(W_RING - 1):
            w_copy(ahead).start()

    @pl.when(s + W_RING - 1 < steps)
    def _():
        w_copy(s + W_RING - 1).start()

    w_copy(s).wait()

    def step(first):
        w32 = wbuf[s % W_RING]
        for r in range(nb):
            rows = slice(r * seg, (r + 1) * seg)
            w = (w32 * gt_ref[r]).astype(BF16)
            p = jnp.dot(a_refs[0][rows, :], w, preferred_element_type=F32)
            if first:
                o_ref[rows, :] = x_ref[rows, :] + p
            else:
                o_ref[rows, :] += p

    @pl.when(k == 0)
    def _():
        step(True)

    @pl.when(k > 0)
    def _():
        step(False)

    if out_norm:
        gf_ref = refs[2 * n_a + 2]

        @pl.when(k == nk - 1)
        def _():
            for r0 in range(0, o_ref.shape[0], NORM_ROWS):
                rows = slice(r0, r0 + NORM_ROWS)
                o_ref[rows, :] = _rms(o_ref[rows, :], gf_ref[...])


def _mm_res(a_list, w3, w_layer, x, mod, layer, chunk, grp, tk, tm=2048, tn=512, out_gain=None):
    m, n = x.shape
    if out_gain is not None:
        tn = n
    n_a = len(a_list)
    ka = a_list[0].shape[1]
    if n_a > 1:
        tk = ka
    nk = ka // tk
    ncol = n // tn
    in_specs = [pl.BlockSpec((tm, tk), lambda i, j, k: (i, k)) for _ in a_list]
    scratch = []
    if nk > 1:
        in_specs.append(pl.BlockSpec(memory_space=pl.ANY))
        scratch = [pltpu.VMEM((W_RING, tk, tn), F32), pltpu.SemaphoreType.DMA((W_RING,))]
    else:
        for idx in range(n_a):
            in_specs.append(pl.BlockSpec(
                (None, tk, tn), lambda i, j, k, idx=idx: (w_layer, idx * (ka // tk) + k, j)))
    in_specs.append(pl.BlockSpec((tm, tn), lambda i, j, k: (i, j)))
    in_specs.append(grp.mod_spec(layer, lambda i, j, k: chunk * ncol + j, tn, tm))
    args = [*a_list, *([w3] * n_a), x, mod]
    if out_gain is not None:
        assert nk > 1 and ncol == 1
        in_specs.append(pl.BlockSpec((1, n), lambda i, j, k: (0, 0)))
        args.append(out_gain.reshape(1, n))
    return pl.pallas_call(
        functools.partial(_mm_res_kernel, n_a=n_a, nk=nk, nb=grp.mod_rows(tm),
                          out_norm=out_gain is not None, w_layer=w_layer),
        out_shape=jax.ShapeDtypeStruct((m, n), F32),
        grid=(m // tm, ncol, nk),
        in_specs=in_specs,
        out_specs=pl.BlockSpec((tm, tn), lambda i, j, k: (i, j)),
        scratch_shapes=scratch,
        compiler_params=_params("arbitrary", "arbitrary", "arbitrary"),
        name="mm_res",
    )(*args)


def _ffn_up_kernel(x_ref, gn_ref, sc_ref, sh_ref, wg_ref, wv_ref, cg_ref, cv_ref, bg_ref, bv_ref,
                   o_ref, h_ref, *, seq):
    @pl.when(pl.program_id(1) == 0)
    def _():
        _norm_mod_into(h_ref, x_ref, gn_ref, sc_ref, sh_ref)

    h = h_ref[...]
    tm, tn = o_ref.shape
    sub = lax.broadcasted_iota(jnp.int32, (FFN_EDGE_ROWS, 1), 0)
    edges = sorted({r for b in range(0, tm, seq) for r in (b, b + seq - FFN_EDGE_ROWS)})
    for c0 in range(0, tn, FFN_SUB_COLS):
        cs = slice(c0, c0 + FFN_SUB_COLS)

        def taps(w_ref):
            u = jnp.dot(h, w_ref[:, cs].astype(BF16), preferred_element_type=F32)
            return pltpu.roll(u, 1, 0), u, pltpu.roll(u, tm - 1, 0)

        def conv(t, c_ref, b_ref):
            return t[0] * c_ref[0:1, cs] + t[1] * c_ref[1:2, cs] + t[2] * c_ref[2:3, cs] + b_ref[:, cs]

        tg, tv = taps(wg_ref), taps(wv_ref)
        o_ref[:, cs] = (_silu(conv(tg, cg_ref, bg_ref)) * conv(tv, cv_ref, bv_ref)).astype(o_ref.dtype)
        for r0 in edges:
            rows = slice(r0, r0 + FFN_EDGE_ROWS)

            def edge(t):
                prev, cur, nxt = (a[rows, :] for a in t)
                if r0 % seq == 0:
                    prev = jnp.where(sub == 0, 0.0, prev)
                if (r0 + FFN_EDGE_ROWS) % seq == 0:
                    nxt = jnp.where(sub == FFN_EDGE_ROWS - 1, 0.0, nxt)
                return prev, cur, nxt

            o_ref[rows, cs] = (_silu(conv(edge(tg), cg_ref, bg_ref))
                               * conv(edge(tv), cv_ref, bv_ref)).astype(o_ref.dtype)


def _ffn_up(x, norm, w_up, w_conv, b_conv, layer, grp, tm=1024, tn=512):
    m, d = x.shape
    d_ff = w_up.shape[2] // 2
    nj = d_ff // tn
    assert tm % grp.seq == 0 and m % tm == 0
    g_norm, mod, chunk_sc, chunk_sh = norm
    b3 = b_conv.reshape(b_conv.shape[0], 1, b_conv.shape[1])
    return pl.pallas_call(
        functools.partial(_ffn_up_kernel, seq=grp.seq),
        out_shape=jax.ShapeDtypeStruct((m, d_ff), BF16),
        grid=(m // tm, nj),
        in_specs=[
            pl.BlockSpec((tm, d), lambda i, j: (i, 0)),
            *grp.norm_specs(layer, chunk_sc, chunk_sh, d, tm),
            pl.BlockSpec((None, d, tn), lambda i, j: (layer, 0, j)),
            pl.BlockSpec((None, d, tn), lambda i, j: (layer, 0, nj + j)),
            pl.BlockSpec((None, w_conv.shape[1], tn), lambda i, j: (layer, 0, j)),
            pl.BlockSpec((None, w_conv.shape[1], tn), lambda i, j: (layer, 0, nj + j)),
            pl.BlockSpec((None, 1, tn), lambda i, j: (layer, 0, j)),
            pl.BlockSpec((None, 1, tn), lambda i, j: (layer, 0, nj + j)),
        ],
        out_specs=pl.BlockSpec((tm, tn), lambda i, j: (i, j)),
        scratch_shapes=[pltpu.VMEM((tm, d), BF16)],
        compiler_params=_params("arbitrary", "arbitrary"),
        name="ffn_up",
    )(x, g_norm, mod, mod, w_up, w_up, w_conv, w_conv, b3, b3)


def _rope_tables(n_tokens, rot_dim):
    n_rows = n_tokens // GRID_W
    row = jnp.repeat(jnp.arange(n_rows), GRID_W).astype(F32)
    col = jnp.tile(jnp.arange(GRID_W), n_rows).astype(F32)
    n_freq = rot_dim // 4
    freqs = ROPE_BASE ** (-jnp.arange(n_freq, dtype=F32) / n_freq)
    ang = jnp.concatenate([row[:, None] * freqs, col[:, None] * freqs], axis=-1)
    cos, sin = jnp.cos(ang), jnp.sin(ang)
    reps = LANES // rot_dim
    c = jnp.tile(jnp.concatenate([cos, cos], axis=-1), (1, reps))
    s = jnp.tile(jnp.concatenate([-sin, sin], axis=-1), (1, reps))
    return c, s


def _trunk(x, grp, mod, caches, P):
    d = x.shape[1]
    ckv_p = krope_p = kb_p = vb_p = k_c = v_c = None
    depth = P["w_ffn_up"].shape[0]
    for l in range(depth):
        i = l // 2
        norm_mix = (P["g_norm_mix"], mod, l, 1, 0)
        if l % 2 == 0:
            (q_mla, kv, ckv, krope, krope_pad, qb, kb, vb) = _inproj_ab(
                x, norm_mix, [w[i] for w in P["w_ab"]],
                P["g_mla_q"][i:i + 1], P["g_mla_kv"][i:i + 1], P["g_gqa_q"][i:i + 1],
                P["g_gqa_k"][i:i + 1], grp, P["dims_ab"])
            ckv_p, krope_p, kb_p, vb_p = ckv, krope, kb, vb
            heads = P["dims_ab"][5]
            segs_a, segs_b = [], []
            if caches is not None:
                c_ckv, c_kr, c_k, c_v = (cc[i] for cc in caches[:4])
                past = c_ckv.shape[0] // grp.batch
                segs_a.append((_kvup(c_ckv, P["w_ab"][4][i]), c_kr, past))
                segs_b.append((c_k, c_v, past))
            segs_a.append((kv, krope_pad, grp.seq))
            segs_b.append((kb, vb, grp.seq))
            tq = min(grp.seq, 512)
            out_a = _mla_attention(q_mla, segs_a, grp, heads, tq, heads)
            out_b = _gqa_attention(qb, segs_b, grp, qb.shape[1] // HEAD_DIM, GQA_KV_HEADS, tq,
                                   GQA_KV_HEADS)
            x = _mm_res([out_a, out_b], P["w_out_ab"], i, x, mod, l, 2, grp, tk=None)
        else:
            w_in_c = P["w_in_c"][i:i + 1]
            nq = w_in_c.shape[2] // 3
            kv_dtype = F32 if caches is None else BF16
            q, h = _proj(x, w_in_c, 0, nq, BF16, grp, True, scale=HEAD_DIM ** -0.5 * LOG2E,
                         norm=norm_mix, tm=1024)
            k = _proj(h, w_in_c, nq, nq, kv_dtype, grp, True)
            v = _proj(h, w_in_c, 2 * nq, nq, kv_dtype, grp, False)
            k_c, v_c = k, v
            segs = []
            if caches is not None:
                c_k, c_v = (cc[i] for cc in caches[4:])
                segs.append((c_k, c_v, c_k.shape[0] // grp.batch))
            segs.append((k, v, grp.seq))
            lambda_init = 0.8 - 0.6 * math.exp(-0.3 * l)
            lams = [P[n][i:i + 1] for n in ("lambda_q1", "lambda_k1", "lambda_q2", "lambda_k2")]
            dheads = nq // (2 * HEAD_DIM)
            o = _diff_attention(q, segs, lams, P["g_diff_sub"][i:i + 1], grp, dheads,
                                min(grp.seq, 1024), dheads if grp.seq <= 256 else 2, lambda_init)
            x = _mm_res([o], P["w_out_c"], i, x, mod, l, 2, grp, tk=o.shape[1])
        g = _ffn_up(x, (P["g_norm_ffn"], mod, 4, 3), P["w_ffn_up"], P["w_ffn_conv"],
                    P["b_ffn_conv"], l, grp)
        if l + 1 < depth:
            x = _mm_res([g], P["w_ffn_down"], l, x, mod, l, 5, grp, tk=512, tn=1024)
        else:
            x = _mm_res([g], P["w_ffn_down"], l, x, mod, l, 5, grp, tk=512, tm=1024,
                        out_gain=P["g_final"])
    return x, (ckv_p, krope_p, kb_p, vb_p), (k_c, v_c)


def kernel(x_prompt, x_sample, cache_mla_ckv, cache_mla_krope, cache_gqa_k, cache_gqa_v,
           cache_diff_k, cache_diff_v, c, c_ctx, w_ada, b_ada, g_norm_mix, g_norm_ffn,
           w_in_ab, g_mla_q, w_mla_q_up, g_mla_kv, w_mla_kv_up, g_gqa_q, g_gqa_k, w_out_ab,
           w_in_c, lambda_q1, lambda_k1, lambda_q2, lambda_k2, g_diff_sub, w_out_c,
           w_ffn_up, w_ffn_conv, b_ffn_conv, w_ffn_down, g_final):
    bp, tp, d = x_prompt.shape
    bs, ts, _ = x_sample.shape
    depth = w_ada.shape[0]
    n_even = w_in_ab.shape[0]
    assert depth == 2 and n_even == 1 and w_in_c.shape[0] == 1, "one MLA/GQA layer + one diff layer"
    past = cache_mla_ckv.shape[2]
    n_qlat = g_mla_q.shape[1]
    n_ckv = g_mla_kv.shape[1]
    mla_heads = w_mla_q_up.shape[2] // (MLA_NOPE + MLA_ROPE)
    n_kb = cache_gqa_k.shape[3] * HEAD_DIM
    n_vb = n_kb
    n_qb = w_in_ab.shape[2] - n_qlat - n_ckv - MLA_ROPE - n_kb - n_vb

    rows = 8
    cond = jnp.zeros((rows, d), F32).at[:bs].set(c).at[bs].set(c_ctx)
    mod = _modulation(cond, w_ada, b_ada)
    mod = mod.reshape(depth, rows, 1, mod.shape[2])

    o0 = n_qlat + n_ckv
    o1 = o0 + MLA_ROPE
    w_a = w_in_ab[:, :, :o0].astype(BF16)
    w_b = w_in_ab[:, :, o1:].astype(BF16)
    w_k = jnp.pad(w_in_ab[:, :, o0:o1], ((0, 0), (0, 0), (0, LANES - MLA_ROPE))).astype(BF16)
    wq = w_mla_q_up.reshape(n_even, n_qlat, mla_heads, MLA_NOPE + MLA_ROPE)
    wq = wq * ((MLA_NOPE + MLA_ROPE) ** -0.5 * LOG2E)
    w_q_up_p = jnp.pad(wq, ((0, 0), (0, 0), (0, 0), (0, LANES - MLA_ROPE))).reshape(
        n_even, n_qlat, mla_heads * (MLA_NOPE + LANES)).astype(BF16)
    P = dict(
        w_ab=(w_a, w_b, w_k, w_q_up_p, w_mla_kv_up.astype(BF16)),
        g_mla_q=g_mla_q, g_mla_kv=g_mla_kv, g_gqa_q=g_gqa_q * (HEAD_DIM ** -0.5 * LOG2E),
        g_gqa_k=g_gqa_k,
        w_out_ab=w_out_ab, w_in_c=w_in_c, lambda_q1=lambda_q1, lambda_k1=lambda_k1,
        lambda_q2=lambda_q2, lambda_k2=lambda_k2, g_diff_sub=g_diff_sub, w_out_c=w_out_c,
        w_ffn_up=w_ffn_up, w_ffn_conv=w_ffn_conv, b_ffn_conv=b_ffn_conv, w_ffn_down=w_ffn_down,
        g_final=g_final,
        g_norm_mix=g_norm_mix.reshape(depth, 1, d), g_norm_ffn=g_norm_ffn.reshape(depth, 1, d),
        dims_ab=(n_qlat, n_ckv, n_qb, n_kb, n_vb, mla_heads),
    )

    grp_p = _Group(bp, tp, bs, False, None)
    y_p, ab_p, c_p = _trunk(x_prompt.reshape(bp * tp, d), grp_p, mod, None, P)

    c128, s128 = _rope_tables(ts, HEAD_DIM)
    c64, s64 = _rope_tables(ts, MLA_ROPE)
    grp_s = _Group(bs, ts, 0, True, dict(c128=c128, s128=s128, c64=c64, s64=s64))
    n_odd = cache_diff_k.shape[1]
    kr_pad = jnp.pad(cache_mla_krope, ((0, 0), (0, 0), (0, 0), (0, LANES - MLA_ROPE))).astype(BF16)
    per_layer = lambda a, n: [a[:, j].reshape(bs * past, -1) for j in range(n)]
    caches = (per_layer(cache_mla_ckv, n_even), per_layer(kr_pad, n_even),
              per_layer(cache_gqa_k, n_even), per_layer(cache_gqa_v, n_even),
              per_layer(cache_diff_k.astype(BF16), n_odd),
              per_layer(cache_diff_v.astype(BF16), n_odd))
    y_s, _, _ = _trunk(x_sample.reshape(bs * ts, d), grp_s, mod, caches, P)

    ckv, krope, kb, vb = ab_p
    k_c, v_c = c_p
    kvh = n_kb // HEAD_DIM
    dh = k_c.shape[1] // (2 * HEAD_DIM)
    return (y_p.reshape(bp, tp, d), y_s.reshape(bs, ts, d),
            ckv.reshape(bp, 1, tp, n_ckv), krope.reshape(bp, 1, tp, MLA_ROPE),
            kb.reshape(bp, 1, tp, kvh, HEAD_DIM), vb.reshape(bp, 1, tp, kvh, HEAD_DIM),
            k_c.reshape(bp, 1, tp, dh, 2, HEAD_DIM), v_c.reshape(bp, 1, tp, dh, 2 * HEAD_DIM))
```

```python
import functools
import math

import jax
import jax.numpy as jnp
from jax import lax
from jax.experimental import pallas as pl
from jax.experimental.pallas import tpu as pltpu

F32 = jnp.float32
BF16 = jnp.bfloat16

HEAD_DIM = 128
GRID_W = 64
ROPE_BASE = 10000.0
EPS = 1e-6
MLA_NOPE = 128
MLA_ROPE = 64
MLA_V = 128
GQA_KV_HEADS = 2
LOG2E = 1.4426950408889634
LANES = 128
FFN_EDGE_ROWS = 16
NORM_ROWS = 64
FFN_SUB_COLS = 256
VMEM_LIMIT = 56 * 1024 * 1024


def _params(*sem):
    return pltpu.CompilerParams(dimension_semantics=sem, vmem_limit_bytes=VMEM_LIMIT)


def _rms(x, g):
    ms = jnp.mean(x * x, axis=-1, keepdims=True)
    return x * lax.rsqrt(ms + EPS) * g


def _silu(x):
    return x / (1.0 + jnp.exp(-x))


def _rope_half(x, c, s, half):
    if 2 * half == LANES:
        sw = pltpu.roll(x, half, 1)
    else:
        lane = lax.broadcasted_iota(jnp.int32, x.shape, 1)
        first = (lane % (2 * half)) < half
        sw = jnp.where(first, pltpu.roll(x, LANES - half, 1), pltpu.roll(x, half, 1))
    return x * c + sw * s


def _mod_kernel(cond_ref, w_ref, b_ref, o_ref):
    s = _silu(cond_ref[...]).astype(BF16)
    o_ref[...] = jnp.dot(s, w_ref[...].astype(BF16), preferred_element_type=F32) + b_ref[...]


def _modulation(cond, w_ada, b_ada, tn=1024):
    depth, d, n = w_ada.shape
    rows = cond.shape[0]
    return pl.pallas_call(
        _mod_kernel,
        out_shape=jax.ShapeDtypeStruct((depth, rows, n), F32),
        grid=(depth, n // tn),
        in_specs=[
            pl.BlockSpec((rows, d), lambda l, j: (0, 0)),
            pl.BlockSpec((None, d, tn), lambda l, j: (l, 0, j)),
            pl.BlockSpec((None, 1, tn), lambda l, j: (l, 0, j)),
        ],
        out_specs=pl.BlockSpec((None, rows, tn), lambda l, j: (l, 0, j)),
        compiler_params=_params("arbitrary", "arbitrary"),
        name="adaln_mod",
    )(cond, w_ada, b_ada.reshape(depth, 1, n))


def _norm_mod_tile(x_ref, g_ref, sc_ref, sh_ref):
    y = _rms(x_ref[...], g_ref[...])
    return (y * (1.0 + sc_ref[...]) + sh_ref[...]).astype(BF16)


def _norm_mod_into(h_ref, x_ref, g_ref, sc_ref, sh_ref):
    g, sc1, sh = g_ref[...], 1.0 + sc_ref[...], sh_ref[...]
    for r0 in range(0, x_ref.shape[0], NORM_ROWS):
        rows = slice(r0, r0 + NORM_ROWS)
        h_ref[rows, :] = (_rms(x_ref[rows, :], g) * sc1 + sh).astype(BF16)


class _Group:
    def __init__(self, batch, seq, mod_row0, per_batch_mod, rope):
        self.batch = batch
        self.seq = seq
        self.m = batch * seq
        self.mod_row0 = mod_row0
        self.per_batch_mod = per_batch_mod
        self.rope = rope

    def mod_rows(self, tm):
        if self.per_batch_mod and tm > self.seq:
            assert tm % self.seq == 0 and self.mod_row0 % (tm // self.seq) == 0
            return tm // self.seq
        return 1

    def mod_spec(self, layer, col, width, tm):
        nb = self.mod_rows(tm)
        if self.per_batch_mod:
            assert tm % self.seq == 0 or self.seq % tm == 0
            row = lambda i: (self.mod_row0 + (i * tm) // self.seq) // nb
        else:
            row = lambda i: self.mod_row0
        col_fn = col if callable(col) else (lambda *idx: col)
        return pl.BlockSpec((None, nb, 1, width),
                            lambda *idx: (layer, row(idx[0]), 0, col_fn(*idx)))

    def norm_specs(self, layer, chunk_sc, chunk_sh, d, tm):
        assert self.mod_rows(tm) == 1
        vec = lambda chunk: pl.BlockSpec(
            (None, None, 1, d), self.mod_spec(layer, chunk, d, tm).index_map)
        return [pl.BlockSpec((None, 1, d), lambda *idx: (layer, 0, 0)), vec(chunk_sc), vec(chunk_sh)]


def _inproj_ab_kernel(*refs, rope, n_qlat, n_ckv, n_qb, n_kb, n_vb, mla_heads):
    (x_ref, gn_ref, sc_ref, sh_ref, wa_ref, wb_ref, wk_ref, wq_ref, wkv_ref,
     gq_ref, gkv_ref, gqn_ref, gkn_ref) = refs[:13]
    (qmla_ref, kv_ref, ckv_ref, kr_ref, krp_ref, qb_ref, kb_ref, vb_ref) = refs[-8:]
    if rope:
        c128, s128, c64, s64 = (r[...] for r in refs[13:17])
    h = _norm_mod_tile(x_ref, gn_ref, sc_ref, sh_ref)

    def seg(w_ref, a, b):
        return jnp.dot(h, w_ref[:, a:b], preferred_element_type=F32)

    qn = _rms(seg(wa_ref, 0, n_qlat), gq_ref[...]).astype(BF16)
    qa = jnp.dot(qn, wq_ref[...], preferred_element_type=F32)
    hw = MLA_NOPE + LANES
    for i in range(mla_heads):
        qmla_ref[:, i * hw:i * hw + MLA_NOPE] = qa[:, i * hw:i * hw + MLA_NOPE].astype(BF16)
        rp = qa[:, i * hw + MLA_NOPE:(i + 1) * hw]
        if rope:
            rp = _rope_half(rp, c64, s64, MLA_ROPE // 2)
        qmla_ref[:, i * hw + MLA_NOPE:(i + 1) * hw] = rp.astype(BF16)
    cn = _rms(seg(wa_ref, n_qlat, n_qlat + n_ckv), gkv_ref[...])
    ckv_ref[...] = cn
    kv_ref[...] = jnp.dot(cn.astype(BF16), wkv_ref[...], preferred_element_type=F32).astype(BF16)
    r = seg(wb_ref, 0, n_qb)
    for i in range(n_qb // HEAD_DIM):
        y = _rms(r[:, i * HEAD_DIM:(i + 1) * HEAD_DIM], gqn_ref[...])
        if rope:
            y = _rope_half(y, c128, s128, HEAD_DIM // 2)
        qb_ref[:, i * HEAD_DIM:(i + 1) * HEAD_DIM] = y.astype(BF16)
    r = seg(wb_ref, n_qb, n_qb + n_kb)
    for i in range(n_kb // HEAD_DIM):
        y = _rms(r[:, i * HEAD_DIM:(i + 1) * HEAD_DIM], gkn_ref[...])
        if rope:
            y = _rope_half(y, c128, s128, HEAD_DIM // 2)
        kb_ref[:, i * HEAD_DIM:(i + 1) * HEAD_DIM] = y
    vb_ref[...] = seg(wb_ref, n_qb + n_kb, n_qb + n_kb + n_vb)
    r = seg(wk_ref, 0, LANES)
    if rope:
        r = _rope_half(r, c64, s64, MLA_ROPE // 2)
    kr_ref[...] = r[:, :MLA_ROPE]
    krp_ref[...] = r.astype(BF16)


def _inproj_ab(x, norm, weights, g_q, g_kv, g_qn, g_kn, grp, dims, tm=256):
    m, d = x.shape
    n_qlat, n_ckv, n_qb, n_kb, n_vb, mla_heads = dims
    g_norm, mod, layer, chunk_sc, chunk_sh = norm
    rope = grp.rope is not None
    one = pl.Buffered(1)
    nq_out = mla_heads * (MLA_NOPE + LANES)
    nkv_out = weights[4].shape[1]
    in_specs = [pl.BlockSpec((tm, d), lambda i: (i, 0))]
    in_specs += grp.norm_specs(layer, chunk_sc, chunk_sh, d, tm)
    in_specs += [pl.BlockSpec(w.shape, lambda i: (0, 0), pipeline_mode=one) for w in weights]
    in_specs += [
        pl.BlockSpec((1, n_qlat), lambda i: (0, 0)),
        pl.BlockSpec((1, n_ckv), lambda i: (0, 0)),
        pl.BlockSpec((1, HEAD_DIM), lambda i: (0, 0)),
        pl.BlockSpec((1, HEAD_DIM), lambda i: (0, 0)),
    ]
    args = [x, g_norm, mod, mod, *weights, g_q, g_kv, g_qn, g_kn]
    if rope:
        per = grp.seq // tm
        for name in ("c128", "s128", "c64", "s64"):
            in_specs.append(pl.BlockSpec((tm, LANES), lambda i: (i % per, 0)))
            args.append(grp.rope[name])
    widths = [(nq_out, BF16), (nkv_out, BF16), (n_ckv, F32), (MLA_ROPE, F32), (LANES, BF16),
              (n_qb, BF16), (n_kb, F32), (n_vb, F32)]
    return pl.pallas_call(
        functools.partial(_inproj_ab_kernel, rope=rope, n_qlat=n_qlat, n_ckv=n_ckv, n_qb=n_qb,
                          n_kb=n_kb, n_vb=n_vb, mla_heads=mla_heads),
        out_shape=[jax.ShapeDtypeStruct((m, w), dt) for w, dt in widths],
        grid=(m // tm,),
        in_specs=in_specs,
        out_specs=[pl.BlockSpec((tm, w), lambda i: (i, 0)) for w, _ in widths],
        compiler_params=_params("arbitrary"),
        name="inproj_ab",
    )(*args)


def _kvup_kernel(c_ref, w_ref, o_ref):
    o_ref[...] = jnp.dot(c_ref[...].astype(BF16), w_ref[...],
                         preferred_element_type=F32).astype(o_ref.dtype)


def _kvup(ckv, wkv, tm=512):
    m, k = ckv.shape
    n = wkv.shape[1]
    return pl.pallas_call(
        _kvup_kernel,
        out_shape=jax.ShapeDtypeStruct((m, n), BF16),
        grid=(m // tm,),
        in_specs=[pl.BlockSpec((tm, k), lambda i: (i, 0)),
                  pl.BlockSpec((k, n), lambda i: (0, 0))],
        out_specs=pl.BlockSpec((tm, n), lambda i: (i, 0)),
        compiler_params=_params("arbitrary"),
        name="mla_kv_up",
    )(ckv, wkv)


def _proj_kernel(*refs, rope, scale):
    if rope:
        h_ref, w_ref, c_ref, s_ref, o_ref = refs
    else:
        h_ref, w_ref, o_ref = refs
    r = jnp.dot(h_ref[...], w_ref[...].astype(BF16), preferred_element_type=F32)
    if rope:
        c, s = c_ref[...], s_ref[...]
        for i in range(r.shape[1] // HEAD_DIM):
            sl = slice(i * HEAD_DIM, (i + 1) * HEAD_DIM)
            y = _rope_half(r[:, sl], c, s, HEAD_DIM // 2)
            o_ref[:, sl] = (y if scale is None else y * scale).astype(o_ref.dtype)
    else:
        o_ref[...] = (r if scale is None else r * scale).astype(o_ref.dtype)


def _proj(h, w3, col0, ncols, out_dtype, grp, use_rope, scale=None, tm=2048, tn=512):
    m, d = h.shape
    rope = use_rope and grp.rope is not None
    tm = min(tm, m)
    off = col0 // tn
    in_specs = [pl.BlockSpec((tm, d), lambda i, j: (i, 0)),
                pl.BlockSpec((None, d, tn), lambda i, j: (0, 0, off + j))]
    args = [h, w3]
    if rope:
        assert tm % grp.seq == 0
        for name in ("c128", "s128"):
            in_specs.append(pl.BlockSpec((tm, LANES), lambda i, j: (0, 0)))
            args.append(jnp.tile(grp.rope[name], (tm // grp.seq, 1)))
    return pl.pallas_call(
        functools.partial(_proj_kernel, rope=rope, scale=scale),
        out_shape=jax.ShapeDtypeStruct((m, ncols), out_dtype),
        grid=(m // tm, ncols // tn),
        in_specs=in_specs,
        out_specs=pl.BlockSpec((tm, tn), lambda i, j: (i, j)),
        compiler_params=_params("arbitrary", "arbitrary"),
        name="proj_c",
    )(*args)


def _softmax_parts(scores):
    mx = jnp.max(scores[0], axis=-1, keepdims=True)
    for s in scores[1:]:
        mx = jnp.maximum(mx, jnp.max(s, axis=-1, keepdims=True))
    ps = [jnp.exp2(s - mx) for s in scores]
    l = jnp.sum(ps[0], axis=-1, keepdims=True)
    for p in ps[1:]:
        l = l + jnp.sum(p, axis=-1, keepdims=True)
    return ps, l


def _attend(q, ks, vs):
    scores = [lax.dot_general(q, k, _NT, preferred_element_type=F32) for k in ks]
    ps, l = _softmax_parts(scores)
    o = jnp.dot(ps[0].astype(BF16), vs[0], preferred_element_type=F32)
    for p, v in zip(ps[1:], vs[1:]):
        o = o + jnp.dot(p.astype(BF16), v, preferred_element_type=F32)
    return o, l


_NT = (((1,), (1,)), ((), ()))


def _mla_kernel(*refs, n_seg, hg):
    q_ref = refs[0]
    o_ref = refs[-1]
    hw = MLA_NOPE + LANES
    for h in range(hg):
        ks, vs = [], []
        for i in range(n_seg):
            kv_ref, kr_ref = refs[1 + 2 * i], refs[2 + 2 * i]
            ks.append(jnp.concatenate([kv_ref[:, h * hw:h * hw + MLA_NOPE], kr_ref[...]], axis=1))
            vs.append(kv_ref[:, h * hw + MLA_NOPE:(h + 1) * hw])
        o, l = _attend(q_ref[:, h * hw:(h + 1) * hw], ks, vs)
        o_ref[:, h * MLA_V:(h + 1) * MLA_V] = (o / l).astype(o_ref.dtype)


def _mla_attention(q, segs, grp, heads, tq, hg):
    m = q.shape[0]
    nq = grp.seq // tq
    hw = MLA_NOPE + LANES
    in_specs = [pl.BlockSpec((tq, hg * hw), lambda b, h, i: (b * nq + i, h))]
    args = [q]
    for kv, kr, rows in segs:
        in_specs.append(pl.BlockSpec((rows, hg * (MLA_NOPE + MLA_V)), lambda b, h, i: (b, h)))
        in_specs.append(pl.BlockSpec((rows, LANES), lambda b, h, i: (b, 0)))
        args += [kv, kr]
    return pl.pallas_call(
        functools.partial(_mla_kernel, n_seg=len(segs), hg=hg),
        out_shape=jax.ShapeDtypeStruct((m, heads * MLA_V), BF16),
        grid=(grp.batch, heads // hg, nq),
        in_specs=in_specs,
        out_specs=pl.BlockSpec((tq, hg * MLA_V), lambda b, h, i: (b * nq + i, h)),
        compiler_params=_params("arbitrary", "arbitrary", "arbitrary"),
        name="attn_mla",
    )(*args)


def _gqa_kernel(*refs, n_seg, group, kvg):
    q_ref = refs[0]
    o_ref = refs[-1]
    for kh in range(kvg):
        ksl = slice(kh * HEAD_DIM, (kh + 1) * HEAD_DIM)
        ks = [refs[1 + 2 * i][:, ksl].astype(BF16) for i in range(n_seg)]
        vs = [refs[2 + 2 * i][:, ksl].astype(BF16) for i in range(n_seg)]
        for g in range(group):
            sl = slice((kh * group + g) * HEAD_DIM, (kh * group + g + 1) * HEAD_DIM)
            o, l = _attend(q_ref[:, sl], ks, vs)
            o_ref[:, sl] = (o / l).astype(o_ref.dtype)


def _gqa_attention(q, segs, grp, heads, kv_heads, tq, kvg):
    m = q.shape[0]
    nq = grp.seq // tq
    group = heads // kv_heads
    gw = kvg * group * HEAD_DIM
    in_specs = [pl.BlockSpec((tq, gw), lambda b, h, i: (b * nq + i, h))]
    args = [q]
    for k, v, rows in segs:
        in_specs.append(pl.BlockSpec((rows, kvg * HEAD_DIM), lambda b, h, i: (b, h)))
        in_specs.append(pl.BlockSpec((rows, kvg * HEAD_DIM), lambda b, h, i: (b, h)))
        args += [k, v]
    return pl.pallas_call(
        functools.partial(_gqa_kernel, n_seg=len(segs), group=group, kvg=kvg),
        out_shape=jax.ShapeDtypeStruct((m, heads * HEAD_DIM), BF16),
        grid=(grp.batch, kv_heads // kvg, nq),
        in_specs=in_specs,
        out_specs=pl.BlockSpec((tq, gw), lambda b, h, i: (b * nq + i, h)),
        compiler_params=_params("arbitrary", "arbitrary", "arbitrary"),
        name="attn_gqa",
    )(*args)


def _diff_kernel(*refs, n_seg, hg, lambda_init):
    q_ref, lq1_ref, lk1_ref, lq2_ref, lk2_ref, g_ref = refs[:6]
    o_ref = refs[-1]
    lam = (jnp.exp(jnp.sum(lq1_ref[...] * lk1_ref[...], axis=-1, keepdims=True))
           - jnp.exp(jnp.sum(lq2_ref[...] * lk2_ref[...], axis=-1, keepdims=True)) + lambda_init)
    hw = 2 * HEAD_DIM
    for h in range(hg):
        vs = [refs[7 + 2 * i][:, h * hw:(h + 1) * hw].astype(BF16) for i in range(n_seg)]
        halves = []
        for j in range(2):
            sl = slice(h * hw + j * HEAD_DIM, h * hw + (j + 1) * HEAD_DIM)
            ks = [refs[6 + 2 * i][:, sl].astype(BF16) for i in range(n_seg)]
            o, l = _attend(q_ref[:, sl], ks, vs)
            halves.append(o * (1.0 / l))
        o = halves[0] - lam * halves[1]
        o_ref[:, h * hw:(h + 1) * hw] = (_rms(o, g_ref[...]) * (1.0 - lambda_init)).astype(o_ref.dtype)


def _diff_attention(q, segs, lams, g_sub, grp, heads, tq, hg, lambda_init):
    m = q.shape[0]
    nq = grp.seq // tq
    hw = 2 * HEAD_DIM
    vec = lambda w: pl.BlockSpec((1, w), lambda b, h, i: (0, 0))
    in_specs = [pl.BlockSpec((tq, hg * hw), lambda b, h, i: (b * nq + i, h))]
    in_specs += [vec(HEAD_DIM)] * 4 + [vec(hw)]
    args = [q, *lams, g_sub]
    for k, v, rows in segs:
        in_specs.append(pl.BlockSpec((rows, hg * hw), lambda b, h, i: (b, h)))
        in_specs.append(pl.BlockSpec((rows, hg * hw), lambda b, h, i: (b, h)))
        args += [k, v]
    return pl.pallas_call(
        functools.partial(_diff_kernel, n_seg=len(segs), hg=hg, lambda_init=lambda_init),
        out_shape=jax.ShapeDtypeStruct((m, heads * hw), BF16),
        grid=(grp.batch, heads // hg, nq),
        in_specs=in_specs,
        out_specs=pl.BlockSpec((tq, hg * hw), lambda b, h, i: (b * nq + i, h)),
        compiler_params=_params("arbitrary", "arbitrary", "arbitrary"),
        name="attn_diff",
    )(*args)


def _mm_res_kernel(*refs, n_a, nk, nb, out_norm, next_norm):
    a_refs = refs[:n_a]
    w_refs = refs[n_a:2 * n_a]
    x_ref, gt_ref = refs[2 * n_a:2 * n_a + 2]
    o_ref = refs[-2] if next_norm else refs[-1]
    seg = o_ref.shape[0] // nb
    if nk == 1:
        ws = [w_ref[...].astype(BF16) for w_ref in w_refs]
        for r in range(nb):
            rows = slice(r * seg, (r + 1) * seg)
            part = None
            for a_ref, w in zip(a_refs, ws):
                p = jnp.dot(a_ref[rows, :], w, preferred_element_type=F32)
                part = p if part is None else part + p
            o_ref[rows, :] = x_ref[rows, :] + gt_ref[r] * part
        return
    k = pl.program_id(2)

    def step(first):
        w32 = w_refs[0][...]
        for r in range(nb):
            rows = slice(r * seg, (r + 1) * seg)
            w = (w32 * gt_ref[r]).astype(BF16)
            p = jnp.dot(a_refs[0][rows, :], w, preferred_element_type=F32)
            if first:
                o_ref[rows, :] = x_ref[rows, :] + p
            else:
                o_ref[rows, :] += p

    @pl.when(k == 0)
    def _():
        step(True)

    @pl.when(k > 0)
    def _():
        step(False)

    if out_norm:
        gf_ref = refs[2 * n_a + 2]

        @pl.when(k == nk - 1)
        def _():
            for r0 in range(0, o_ref.shape[0], NORM_ROWS):
                rows = slice(r0, r0 + NORM_ROWS)
                o_ref[rows, :] = _rms(o_ref[rows, :], gf_ref[...])

    if next_norm:
        gn_ref, sc_ref, sh_ref = refs[2 * n_a + 2:2 * n_a + 5]

        @pl.when(k == nk - 1)
        def _():
            _norm_mod_into(refs[-1], o_ref, gn_ref, sc_ref, sh_ref)


def _mm_res(a_list, w3, w_layer, x, mod, layer, chunk, grp, tk, tm=2048, tn=512, out_gain=None,
            next_norm=None):
    m, n = x.shape
    if out_gain is not None or next_norm is not None:
        tn = n
    n_a = len(a_list)
    ka = a_list[0].shape[1]
    if n_a > 1:
        tk = ka
    nk = ka // tk
    ncol = n // tn
    in_specs = [pl.BlockSpec((tm, tk), lambda i, j, k: (i, k)) for _ in a_list]
    for idx in range(n_a):
        in_specs.append(pl.BlockSpec((None, tk, tn),
                                     lambda i, j, k, idx=idx: (w_layer, idx * (ka // tk) + k, j)))
    in_specs.append(pl.BlockSpec((tm, tn), lambda i, j, k: (i, j)))
    in_specs.append(grp.mod_spec(layer, lambda i, j, k: chunk * ncol + j, tn, tm))
    args = [*a_list, *([w3] * n_a), x, mod]
    if out_gain is not None:
        assert nk > 1 and ncol == 1
        in_specs.append(pl.BlockSpec((1, n), lambda i, j, k: (0, 0)))
        args.append(out_gain.reshape(1, n))
    out_shape = [jax.ShapeDtypeStruct((m, n), F32)]
    out_specs = [pl.BlockSpec((tm, tn), lambda i, j, k: (i, j))]
    if next_norm is not None:
        assert nk > 1 and ncol == 1 and out_gain is None
        g_next, layer_next, chunk_sc, chunk_sh = next_norm
        in_specs += grp.norm_specs(layer_next, chunk_sc, chunk_sh, n, tm)
        args += [g_next, mod, mod]
        out_shape.append(jax.ShapeDtypeStruct((m, n), BF16))
        out_specs.append(pl.BlockSpec((tm, n), lambda i, j, k: (i, 0)))
    res = pl.pallas_call(
        functools.partial(_mm_res_kernel, n_a=n_a, nk=nk, nb=grp.mod_rows(tm),
                          out_norm=out_gain is not None, next_norm=next_norm is not None),
        out_shape=out_shape,
        grid=(m // tm, ncol, nk),
        in_specs=in_specs,
        out_specs=out_specs,
        compiler_params=_params("arbitrary", "arbitrary", "arbitrary"),
        name="mm_res",
    )(*args)
    return res if next_norm is not None else res[0]


def _ffn_up_kernel(x_ref, gn_ref, sc_ref, sh_ref, wg_ref, wv_ref, cg_ref, cv_ref, bg_ref, bv_ref,
                   o_ref, h_ref, *, seq):
    @pl.when(pl.program_id(1) == 0)
    def _():
        _norm_mod_into(h_ref, x_ref, gn_ref, sc_ref, sh_ref)

    h = h_ref[...]
    tm, tn = o_ref.shape
    sub = lax.broadcasted_iota(jnp.int32, (FFN_EDGE_ROWS, 1), 0)
    edges = sorted({r for b in range(0, tm, seq) for r in (b, b + seq - FFN_EDGE_ROWS)})
    for c0 in range(0, tn, FFN_SUB_COLS):
        cs = slice(c0, c0 + FFN_SUB_COLS)

        def taps(w_ref):
            u = jnp.dot(h, w_ref[:, cs].astype(BF16), preferred_element_type=F32)
            return pltpu.roll(u, 1, 0), u, pltpu.roll(u, tm - 1, 0)

        def conv(t, c_ref, b_ref):
            return t[0] * c_ref[0:1, cs] + t[1] * c_ref[1:2, cs] + t[2] * c_ref[2:3, cs] + b_ref[:, cs]

        tg, tv = taps(wg_ref), taps(wv_ref)
        o_ref[:, cs] = (_silu(conv(tg, cg_ref, bg_ref)) * conv(tv, cv_ref, bv_ref)).astype(o_ref.dtype)
        for r0 in edges:
            rows = slice(r0, r0 + FFN_EDGE_ROWS)

            def edge(t):
                prev, cur, nxt = (a[rows, :] for a in t)
                if r0 % seq == 0:
                    prev = jnp.where(sub == 0, 0.0, prev)
                if (r0 + FFN_EDGE_ROWS) % seq == 0:
                    nxt = jnp.where(sub == FFN_EDGE_ROWS - 1, 0.0, nxt)
                return prev, cur, nxt

            o_ref[rows, cs] = (_silu(conv(edge(tg), cg_ref, bg_ref))
                               * conv(edge(tv), cv_ref, bv_ref)).astype(o_ref.dtype)


def _ffn_up(x, norm, w_up, w_conv, b_conv, layer, grp, tm=1024, tn=512):
    m, d = x.shape
    d_ff = w_up.shape[2] // 2
    nj = d_ff // tn
    assert tm % grp.seq == 0 and m % tm == 0
    g_norm, mod, chunk_sc, chunk_sh = norm
    b3 = b_conv.reshape(b_conv.shape[0], 1, b_conv.shape[1])
    return pl.pallas_call(
        functools.partial(_ffn_up_kernel, seq=grp.seq),
        out_shape=jax.ShapeDtypeStruct((m, d_ff), BF16),
        grid=(m // tm, nj),
        in_specs=[
            pl.BlockSpec((tm, d), lambda i, j: (i, 0)),
            *grp.norm_specs(layer, chunk_sc, chunk_sh, d, tm),
            pl.BlockSpec((None, d, tn), lambda i, j: (layer, 0, j)),
            pl.BlockSpec((None, d, tn), lambda i, j: (layer, 0, nj + j)),
            pl.BlockSpec((None, w_conv.shape[1], tn), lambda i, j: (layer, 0, j)),
            pl.BlockSpec((None, w_conv.shape[1], tn), lambda i, j: (layer, 0, nj + j)),
            pl.BlockSpec((None, 1, tn), lambda i, j: (layer, 0, j)),
            pl.BlockSpec((None, 1, tn), lambda i, j: (layer, 0, nj + j)),
        ],
        out_specs=pl.BlockSpec((tm, tn), lambda i, j: (i, j)),
        scratch_shapes=[pltpu.VMEM((tm, d), BF16)],
        compiler_params=_params("arbitrary", "arbitrary"),
        name="ffn_up",
    )(x, g_norm, mod, mod, w_up, w_up, w_conv, w_conv, b3, b3)


def _rope_tables(n_tokens, rot_dim):
    n_rows = n_tokens // GRID_W
    row = jnp.repeat(jnp.arange(n_rows), GRID_W).astype(F32)
    col = jnp.tile(jnp.arange(GRID_W), n_rows).astype(F32)
    n_freq = rot_dim // 4
    freqs = ROPE_BASE ** (-jnp.arange(n_freq, dtype=F32) / n_freq)
    ang = jnp.concatenate([row[:, None] * freqs, col[:, None] * freqs], axis=-1)
    cos, sin = jnp.cos(ang), jnp.sin(ang)
    reps = LANES // rot_dim
    c = jnp.tile(jnp.concatenate([cos, cos], axis=-1), (1, reps))
    s = jnp.tile(jnp.concatenate([-sin, sin], axis=-1), (1, reps))
    return c, s


def _trunk(x, grp, mod, caches, P):
    d = x.shape[1]
    ckv_p = krope_p = kb_p = vb_p = k_c = v_c = h_next = None
    depth = P["w_ffn_up"].shape[0]
    for l in range(depth):
        i = l // 2
        norm_mix = (P["g_norm_mix"], mod, l, 1, 0)
        if l % 2 == 0:
            (q_mla, kv, ckv, krope, krope_pad, qb, kb, vb) = _inproj_ab(
                x, norm_mix, [w[i] for w in P["w_ab"]],
                P["g_mla_q"][i:i + 1], P["g_mla_kv"][i:i + 1], P["g_gqa_q"][i:i + 1],
                P["g_gqa_k"][i:i + 1], grp, P["dims_ab"])
            ckv_p, krope_p, kb_p, vb_p = ckv, krope, kb, vb
            heads = P["dims_ab"][5]
            segs_a, segs_b = [], []
            if caches is not None:
                c_ckv, c_kr, c_k, c_v = (cc[i] for cc in caches[:4])
                past = c_ckv.shape[0] // grp.batch
                segs_a.append((_kvup(c_ckv, P["w_ab"][4][i]), c_kr, past))
                segs_b.append((c_k, c_v, past))
            segs_a.append((kv, krope_pad, grp.seq))
            segs_b.append((kb, vb, grp.seq))
            tq = min(grp.seq, 512)
            out_a = _mla_attention(q_mla, segs_a, grp, heads, tq, heads)
            out_b = _gqa_attention(qb, segs_b, grp, qb.shape[1] // HEAD_DIM, GQA_KV_HEADS, tq,
                                   GQA_KV_HEADS)
            x = _mm_res([out_a, out_b], P["w_out_ab"], i, x, mod, l, 2, grp, tk=None)
        else:
            w_in_c = P["w_in_c"][i:i + 1]
            nq = w_in_c.shape[2] // 3
            kv_dtype = F32 if caches is None else BF16
            h = h_next
            q = _proj(h, w_in_c, 0, nq, BF16, grp, True, scale=HEAD_DIM ** -0.5 * LOG2E)
            k = _proj(h, w_in_c, nq, nq, kv_dtype, grp, True)
            v = _proj(h, w_in_c, 2 * nq, nq, kv_dtype, grp, False)
            k_c, v_c = k, v
            segs = []
            if caches is not None:
                c_k, c_v = (cc[i] for cc in caches[4:])
                segs.append((c_k, c_v, c_k.shape[0] // grp.batch))
            segs.append((k, v, grp.seq))
            lambda_init = 0.8 - 0.6 * math.exp(-0.3 * l)
            lams = [P[n][i:i + 1] for n in ("lambda_q1", "lambda_k1", "lambda_q2", "lambda_k2")]
            dheads = nq // (2 * HEAD_DIM)
            o = _diff_attention(q, segs, lams, P["g_diff_sub"][i:i + 1], grp, dheads,
                                min(grp.seq, 1024), dheads if grp.seq <= 256 else 2, lambda_init)
            x = _mm_res([o], P["w_out_c"], i, x, mod, l, 2, grp, tk=o.shape[1])
        g = _ffn_up(x, (P["g_norm_ffn"], mod, 4, 3), P["w_ffn_up"], P["w_ffn_conv"],
                    P["b_ffn_conv"], l, grp)
        if l + 1 < depth:
            x, h_next = _mm_res([g], P["w_ffn_down"], l, x, mod, l, 5, grp, tk=512, tm=1024,
                                next_norm=(P["g_norm_mix"], l + 1, 1, 0))
        else:
            x = _mm_res([g], P["w_ffn_down"], l, x, mod, l, 5, grp, tk=512, tm=1024,
                        out_gain=P["g_final"])
    return x, (ckv_p, krope_p, kb_p, vb_p), (k_c, v_c)


def kernel(x_prompt, x_sample, cache_mla_ckv, cache_mla_krope, cache_gqa_k, cache_gqa_v,
           cache_diff_k, cache_diff_v, c, c_ctx, w_ada, b_ada, g_norm_mix, g_norm_ffn,
           w_in_ab, g_mla_q, w_mla_q_up, g_mla_kv, w_mla_kv_up, g_gqa_q, g_gqa_k, w_out_ab,
           w_in_c, lambda_q1, lambda_k1, lambda_q2, lambda_k2, g_diff_sub, w_out_c,
           w_ffn_up, w_ffn_conv, b_ffn_conv, w_ffn_down, g_final):
    bp, tp, d = x_prompt.shape
    bs, ts, _ = x_sample.shape
    depth = w_ada.shape[0]
    n_even = w_in_ab.shape[0]
    assert depth == 2 and n_even == 1 and w_in_c.shape[0] == 1, "one MLA/GQA layer + one diff layer"
    past = cache_mla_ckv.shape[2]
    n_qlat = g_mla_q.shape[1]
    n_ckv = g_mla_kv.shape[1]
    mla_heads = w_mla_q_up.shape[2] // (MLA_NOPE + MLA_ROPE)
    n_kb = cache_gqa_k.shape[3] * HEAD_DIM
    n_vb = n_kb
    n_qb = w_in_ab.shape[2] - n_qlat - n_ckv - MLA_ROPE - n_kb - n_vb

    rows = 8
    cond = jnp.zeros((rows, d), F32).at[:bs].set(c).at[bs].set(c_ctx)
    mod = _modulation(cond, w_ada, b_ada)
    mod = mod.reshape(depth, rows, 1, mod.shape[2])

    o0 = n_qlat + n_ckv
    o1 = o0 + MLA_ROPE
    w_a = w_in_ab[:, :, :o0].astype(BF16)
    w_b = w_in_ab[:, :, o1:].astype(BF16)
    w_k = jnp.pad(w_in_ab[:, :, o0:o1], ((0, 0), (0, 0), (0, LANES - MLA_ROPE))).astype(BF16)
    wq = w_mla_q_up.reshape(n_even, n_qlat, mla_heads, MLA_NOPE + MLA_ROPE)
    wq = wq * ((MLA_NOPE + MLA_ROPE) ** -0.5 * LOG2E)
    w_q_up_p = jnp.pad(wq, ((0, 0), (0, 0), (0, 0), (0, LANES - MLA_ROPE))).reshape(
        n_even, n_qlat, mla_heads * (MLA_NOPE + LANES)).astype(BF16)
    P = dict(
        w_ab=(w_a, w_b, w_k, w_q_up_p, w_mla_kv_up.astype(BF16)),
        g_mla_q=g_mla_q, g_mla_kv=g_mla_kv, g_gqa_q=g_gqa_q * (HEAD_DIM ** -0.5 * LOG2E),
        g_gqa_k=g_gqa_k,
        w_out_ab=w_out_ab, w_in_c=w_in_c, lambda_q1=lambda_q1, lambda_k1=lambda_k1,
        lambda_q2=lambda_q2, lambda_k2=lambda_k2, g_diff_sub=g_diff_sub, w_out_c=w_out_c,
        w_ffn_up=w_ffn_up, w_ffn_conv=w_ffn_conv, b_ffn_conv=b_ffn_conv, w_ffn_down=w_ffn_down,
        g_final=g_final,
        g_norm_mix=g_norm_mix.reshape(depth, 1, d), g_norm_ffn=g_norm_ffn.reshape(depth, 1, d),
        dims_ab=(n_qlat, n_ckv, n_qb, n_kb, n_vb, mla_heads),
    )

    grp_p = _Group(bp, tp, bs, False, None)
    y_p, ab_p, c_p = _trunk(x_prompt.reshape(bp * tp, d), grp_p, mod, None, P)

    c128, s128 = _rope_tables(ts, HEAD_DIM)
    c64, s64 = _rope_tables(ts, MLA_ROPE)
    grp_s = _Group(bs, ts, 0, True, dict(c128=c128, s128=s128, c64=c64, s64=s64))
    n_odd = cache_diff_k.shape[1]
    kr_pad = jnp.pad(cache_mla_krope, ((0, 0), (0, 0), (0, 0), (0, LANES - MLA_ROPE))).astype(BF16)
    per_layer = lambda a, n: [a[:, j].reshape(bs * past, -1) for j in range(n)]
    caches = (per_layer(cache_mla_ckv, n_even), per_layer(kr_pad, n_even),
              per_layer(cache_gqa_k, n_even), per_layer(cache_gqa_v, n_even),
              per_layer(cache_diff_k.astype(BF16), n_odd),
              per_layer(cache_diff_v.astype(BF16), n_odd))
    y_s, _, _ = _trunk(x_sample.reshape(bs * ts, d), grp_s, mod, caches, P)

    ckv, krope, kb, vb = ab_p
    k_c, v_c = c_p
    kvh = n_kb // HEAD_DIM
    dh = k_c.shape[1] // (2 * HEAD_DIM)
    return (y_p.reshape(bp, tp, d), y_s.reshape(bs, ts, d),
            ckv.reshape(bp, 1, tp, n_ckv), krope.reshape(bp, 1, tp, MLA_ROPE),
            kb.reshape(bp, 1, tp, kvh, HEAD_DIM), vb.reshape(bp, 1, tp, kvh, HEAD_DIM),
            k_c.reshape(bp, 1, tp, dh, 2, HEAD_DIM), v_c.reshape(bp, 1, tp, dh, 2 * HEAD_DIM))
```

```python
import functools
import math

import jax
import jax.numpy as jnp
from jax import lax
from jax.experimental import pallas as pl
from jax.experimental.pallas import tpu as pltpu

F32 = jnp.float32
BF16 = jnp.bfloat16

HEAD_DIM = 128
GRID_W = 64
ROPE_BASE = 10000.0
EPS = 1e-6
MLA_NOPE = 128
MLA_ROPE = 64
MLA_V = 128
GQA_KV_HEADS = 2
LOG2E = 1.4426950408889634
LANES = 128
FFN_EDGE_ROWS = 16
NORM_ROWS = 32
VMEM_LIMIT = 56 * 1024 * 1024


def _params(*sem):
    return pltpu.CompilerParams(dimension_semantics=sem, vmem_limit_bytes=VMEM_LIMIT)


def _rms(x, g):
    ms = jnp.mean(x * x, axis=-1, keepdims=True)
    return x * lax.rsqrt(ms + EPS) * g


def _silu(x):
    return x / (1.0 + jnp.exp(-x))


def _rope_half(x, c, s, half):
    if 2 * half == LANES:
        sw = pltpu.roll(x, half, 1)
    else:
        lane = lax.broadcasted_iota(jnp.int32, x.shape, 1)
        first = (lane % (2 * half)) < half
        sw = jnp.where(first, pltpu.roll(x, LANES - half, 1), pltpu.roll(x, half, 1))
    return x * c + sw * s


def _mod_kernel(cond_ref, w_ref, b_ref, o_ref):
    s = _silu(cond_ref[...]).astype(BF16)
    o_ref[...] = jnp.dot(s, w_ref[...].astype(BF16), preferred_element_type=F32) + b_ref[...]


def _modulation(cond, w_ada, b_ada, tn=1024):
    depth, d, n = w_ada.shape
    rows = cond.shape[0]
    return pl.pallas_call(
        _mod_kernel,
        out_shape=jax.ShapeDtypeStruct((depth, rows, n), F32),
        grid=(depth, n // tn),
        in_specs=[
            pl.BlockSpec((rows, d), lambda l, j: (0, 0)),
            pl.BlockSpec((None, d, tn), lambda l, j: (l, 0, j)),
            pl.BlockSpec((None, 1, tn), lambda l, j: (l, 0, j)),
        ],
        out_specs=pl.BlockSpec((None, rows, tn), lambda l, j: (l, 0, j)),
        compiler_params=_params("arbitrary", "arbitrary"),
        name="adaln_mod",
    )(cond, w_ada, b_ada.reshape(depth, 1, n))


def _norm_mod_tile(x_ref, g_ref, sc_ref, sh_ref):
    y = _rms(x_ref[...], g_ref[...])
    return (y * (1.0 + sc_ref[...]) + sh_ref[...]).astype(BF16)


def _norm_mod_into(h_ref, x_ref, g_ref, sc_ref, sh_ref):
    g, sc1, sh = g_ref[...], 1.0 + sc_ref[...], sh_ref[...]
    for r0 in range(0, x_ref.shape[0], NORM_ROWS):
        rows = slice(r0, r0 + NORM_ROWS)
        h_ref[rows, :] = (_rms(x_ref[rows, :], g) * sc1 + sh).astype(BF16)


class _Group:
    def __init__(self, batch, seq, mod_row0, per_batch_mod, rope):
        self.batch = batch
        self.seq = seq
        self.mod_row0 = mod_row0
        self.per_batch_mod = per_batch_mod
        self.rope = rope

    def mod_rows(self, tm):
        if self.per_batch_mod and tm > self.seq:
            assert tm % self.seq == 0 and self.mod_row0 % (tm // self.seq) == 0
            return tm // self.seq
        return 1

    def mod_spec(self, layer, col, width, tm):
        nb = self.mod_rows(tm)
        if self.per_batch_mod:
            assert tm % self.seq == 0 or self.seq % tm == 0
            row = lambda i: (self.mod_row0 + (i * tm) // self.seq) // nb
        else:
            row = lambda i: self.mod_row0
        col_fn = col if callable(col) else (lambda *idx: col)
        return pl.BlockSpec((None, nb, 1, width),
                            lambda *idx: (layer, row(idx[0]), 0, col_fn(*idx)))

    def norm_specs(self, layer, chunk_sc, chunk_sh, d, tm):
        assert self.mod_rows(tm) == 1
        vec = lambda chunk: pl.BlockSpec(
            (None, None, 1, d), self.mod_spec(layer, chunk, d, tm).index_map)
        return [pl.BlockSpec((None, 1, d), lambda *idx: (layer, 0, 0)), vec(chunk_sc), vec(chunk_sh)]


def _inproj_ab_kernel(*refs, rope, n_qlat, n_ckv, n_qb, n_kb, n_vb, mla_heads):
    (x_ref, gn_ref, sc_ref, sh_ref, wa_ref, wb_ref, wk_ref, wq_ref, wkv_ref,
     gq_ref, gkv_ref, gqn_ref, gkn_ref) = refs[:13]
    (qmla_ref, kv_ref, ckv_ref, kr_ref, krp_ref, qb_ref, kb_ref, vb_ref) = refs[-8:]
    if rope:
        c128, s128, c64, s64 = (r[...] for r in refs[13:17])
    h = _norm_mod_tile(x_ref, gn_ref, sc_ref, sh_ref)

    def seg(w_ref, a, b):
        return jnp.dot(h, w_ref[:, a:b], preferred_element_type=F32)

    qn = _rms(seg(wa_ref, 0, n_qlat), gq_ref[...]).astype(BF16)
    qa = jnp.dot(qn, wq_ref[...], preferred_element_type=F32)
    hw = MLA_NOPE + LANES
    for i in range(mla_heads):
        qmla_ref[:, i * hw:i * hw + MLA_NOPE] = qa[:, i * hw:i * hw + MLA_NOPE].astype(BF16)
        rp = qa[:, i * hw + MLA_NOPE:(i + 1) * hw]
        if rope:
            rp = _rope_half(rp, c64, s64, MLA_ROPE // 2)
        qmla_ref[:, i * hw + MLA_NOPE:(i + 1) * hw] = rp.astype(BF16)
    cn = _rms(seg(wa_ref, n_qlat, n_qlat + n_ckv), gkv_ref[...])
    ckv_ref[...] = cn
    kv_ref[...] = jnp.dot(cn.astype(BF16), wkv_ref[...], preferred_element_type=F32).astype(BF16)
    r = seg(wb_ref, 0, n_qb)
    for i in range(n_qb // HEAD_DIM):
        y = _rms(r[:, i * HEAD_DIM:(i + 1) * HEAD_DIM], gqn_ref[...])
        if rope:
            y = _rope_half(y, c128, s128, HEAD_DIM // 2)
        qb_ref[:, i * HEAD_DIM:(i + 1) * HEAD_DIM] = y.astype(BF16)
    r = seg(wb_ref, n_qb, n_qb + n_kb)
    for i in range(n_kb // HEAD_DIM):
        y = _rms(r[:, i * HEAD_DIM:(i + 1) * HEAD_DIM], gkn_ref[...])
        if rope:
            y = _rope_half(y, c128, s128, HEAD_DIM // 2)
        kb_ref[:, i * HEAD_DIM:(i + 1) * HEAD_DIM] = y
    vb_ref[...] = seg(wb_ref, n_qb + n_kb, n_qb + n_kb + n_vb)
    r = seg(wk_ref, 0, LANES)
    if rope:
        r = _rope_half(r, c64, s64, MLA_ROPE // 2)
    kr_ref[...] = r[:, :MLA_ROPE]
    krp_ref[...] = r.astype(BF16)


def _inproj_ab(x, norm, weights, g_q, g_kv, g_qn, g_kn, grp, dims, tm=256):
    m, d = x.shape
    n_qlat, n_ckv, n_qb, n_kb, n_vb, mla_heads = dims
    g_norm, mod, layer, chunk_sc, chunk_sh = norm
    rope = grp.rope is not None
    one = pl.Buffered(1)
    nq_out = mla_heads * (MLA_NOPE + LANES)
    nkv_out = weights[4].shape[1]
    in_specs = [pl.BlockSpec((tm, d), lambda i: (i, 0))]
    in_specs += grp.norm_specs(layer, chunk_sc, chunk_sh, d, tm)
    in_specs += [pl.BlockSpec(w.shape, lambda i: (0, 0), pipeline_mode=one) for w in weights]
    in_specs += [
        pl.BlockSpec((1, n_qlat), lambda i: (0, 0)),
        pl.BlockSpec((1, n_ckv), lambda i: (0, 0)),
        pl.BlockSpec((1, HEAD_DIM), lambda i: (0, 0)),
        pl.BlockSpec((1, HEAD_DIM), lambda i: (0, 0)),
    ]
    args = [x, g_norm, mod, mod, *weights, g_q, g_kv, g_qn, g_kn]
    if rope:
        per = grp.seq // tm
        for name in ("c128", "s128", "c64", "s64"):
            in_specs.append(pl.BlockSpec((tm, LANES), lambda i: (i % per, 0)))
            args.append(grp.rope[name])
    widths = [(nq_out, BF16), (nkv_out, BF16), (n_ckv, F32), (MLA_ROPE, F32), (LANES, BF16),
              (n_qb, BF16), (n_kb, F32), (n_vb, F32)]
    return pl.pallas_call(
        functools.partial(_inproj_ab_kernel, rope=rope, n_qlat=n_qlat, n_ckv=n_ckv, n_qb=n_qb,
                          n_kb=n_kb, n_vb=n_vb, mla_heads=mla_heads),
        out_shape=[jax.ShapeDtypeStruct((m, w), dt) for w, dt in widths],
        grid=(m // tm,),
        in_specs=in_specs,
        out_specs=[pl.BlockSpec((tm, w), lambda i: (i, 0)) for w, _ in widths],
        compiler_params=_params("arbitrary"),
        name="inproj_ab",
    )(*args)


def _kvup_kernel(c_ref, w_ref, o_ref):
    o_ref[...] = jnp.dot(c_ref[...].astype(BF16), w_ref[...],
                         preferred_element_type=F32).astype(o_ref.dtype)


def _kvup(ckv, wkv, tm=512):
    m, k = ckv.shape
    n = wkv.shape[1]
    return pl.pallas_call(
        _kvup_kernel,
        out_shape=jax.ShapeDtypeStruct((m, n), BF16),
        grid=(m // tm,),
        in_specs=[pl.BlockSpec((tm, k), lambda i: (i, 0)),
                  pl.BlockSpec((k, n), lambda i: (0, 0))],
        out_specs=pl.BlockSpec((tm, n), lambda i: (i, 0)),
        compiler_params=_params("arbitrary"),
        name="mla_kv_up",
    )(ckv, wkv)


def _proj_kernel(*refs, rope, scale):
    if rope:
        h_ref, w_ref, c_ref, s_ref, o_ref = refs
    else:
        h_ref, w_ref, o_ref = refs
    r = jnp.dot(h_ref[...], w_ref[...].astype(BF16), preferred_element_type=F32)
    if rope:
        c, s = c_ref[...], s_ref[...]
        for i in range(r.shape[1] // HEAD_DIM):
            sl = slice(i * HEAD_DIM, (i + 1) * HEAD_DIM)
            y = _rope_half(r[:, sl], c, s, HEAD_DIM // 2)
            o_ref[:, sl] = (y if scale is None else y * scale).astype(o_ref.dtype)
    else:
        o_ref[...] = (r if scale is None else r * scale).astype(o_ref.dtype)


def _proj(h, w3, col0, ncols, out_dtype, grp, use_rope, scale=None, tm=2048, tn=512):
    m, d = h.shape
    rope = use_rope and grp.rope is not None
    tm = min(tm, m)
    off = col0 // tn
    in_specs = [pl.BlockSpec((tm, d), lambda i, j: (i, 0)),
                pl.BlockSpec((None, d, tn), lambda i, j: (0, 0, off + j))]
    args = [h, w3]
    if rope:
        assert tm % grp.seq == 0
        for name in ("c128", "s128"):
            in_specs.append(pl.BlockSpec((tm, LANES), lambda i, j: (0, 0)))
            args.append(jnp.tile(grp.rope[name], (tm // grp.seq, 1)))
    return pl.pallas_call(
        functools.partial(_proj_kernel, rope=rope, scale=scale),
        out_shape=jax.ShapeDtypeStruct((m, ncols), out_dtype),
        grid=(m // tm, ncols // tn),
        in_specs=in_specs,
        out_specs=pl.BlockSpec((tm, tn), lambda i, j: (i, j)),
        compiler_params=_params("arbitrary", "arbitrary"),
        name="proj_c",
    )(*args)


def _softmax_parts(scores):
    mx = jnp.max(scores[0], axis=-1, keepdims=True)
    for s in scores[1:]:
        mx = jnp.maximum(mx, jnp.max(s, axis=-1, keepdims=True))
    ps = [jnp.exp2(s - mx) for s in scores]
    l = jnp.sum(ps[0], axis=-1, keepdims=True)
    for p in ps[1:]:
        l = l + jnp.sum(p, axis=-1, keepdims=True)
    return ps, l


def _attend(q, ks, vs):
    scores = [lax.dot_general(q, k, _NT, preferred_element_type=F32) for k in ks]
    ps, l = _softmax_parts(scores)
    o = jnp.dot(ps[0].astype(BF16), vs[0], preferred_element_type=F32)
    for p, v in zip(ps[1:], vs[1:]):
        o = o + jnp.dot(p.astype(BF16), v, preferred_element_type=F32)
    return o, l


_NT = (((1,), (1,)), ((), ()))


def _mla_kernel(*refs, n_seg, hg):
    q_ref = refs[0]
    o_ref = refs[-1]
    hw = MLA_NOPE + LANES
    for h in range(hg):
        ks, vs = [], []
        for i in range(n_seg):
            kv_ref, kr_ref = refs[1 + 2 * i], refs[2 + 2 * i]
            ks.append(jnp.concatenate([kv_ref[:, h * hw:h * hw + MLA_NOPE], kr_ref[...]], axis=1))
            vs.append(kv_ref[:, h * hw + MLA_NOPE:(h + 1) * hw])
        o, l = _attend(q_ref[:, h * hw:(h + 1) * hw], ks, vs)
        o_ref[:, h * MLA_V:(h + 1) * MLA_V] = (o / l).astype(o_ref.dtype)


def _mla_attention(q, segs, grp, heads, tq, hg):
    m = q.shape[0]
    nq = grp.seq // tq
    hw = MLA_NOPE + LANES
    in_specs = [pl.BlockSpec((tq, hg * hw), lambda b, h, i: (b * nq + i, h))]
    args = [q]
    for kv, kr, rows in segs:
        in_specs.append(pl.BlockSpec((rows, hg * (MLA_NOPE + MLA_V)), lambda b, h, i: (b, h)))
        in_specs.append(pl.BlockSpec((rows, LANES), lambda b, h, i: (b, 0)))
        args += [kv, kr]
    return pl.pallas_call(
        functools.partial(_mla_kernel, n_seg=len(segs), hg=hg),
        out_shape=jax.ShapeDtypeStruct((m, heads * MLA_V), BF16),
        grid=(grp.batch, heads // hg, nq),
        in_specs=in_specs,
        out_specs=pl.BlockSpec((tq, hg * MLA_V), lambda b, h, i: (b * nq + i, h)),
        compiler_params=_params("arbitrary", "arbitrary", "arbitrary"),
        name="attn_mla",
    )(*args)


def _gqa_kernel(*refs, n_seg, group, kvg):
    q_ref = refs[0]
    o_ref = refs[-1]
    for kh in range(kvg):
        ksl = slice(kh * HEAD_DIM, (kh + 1) * HEAD_DIM)
        ks = [refs[1 + 2 * i][:, ksl].astype(BF16) for i in range(n_seg)]
        vs = [refs[2 + 2 * i][:, ksl].astype(BF16) for i in range(n_seg)]
        for g in range(group):
            sl = slice((kh * group + g) * HEAD_DIM, (kh * group + g + 1) * HEAD_DIM)
            o, l = _attend(q_ref[:, sl], ks, vs)
            o_ref[:, sl] = (o / l).astype(o_ref.dtype)


def _gqa_attention(q, segs, grp, heads, kv_heads, tq, kvg):
    m = q.shape[0]
    nq = grp.seq // tq
    group = heads // kv_heads
    gw = kvg * group * HEAD_DIM
    in_specs = [pl.BlockSpec((tq, gw), lambda b, h, i: (b * nq + i, h))]
    args = [q]
    for k, v, rows in segs:
        in_specs.append(pl.BlockSpec((rows, kvg * HEAD_DIM), lambda b, h, i: (b, h)))
        in_specs.append(pl.BlockSpec((rows, kvg * HEAD_DIM), lambda b, h, i: (b, h)))
        args += [k, v]
    return pl.pallas_call(
        functools.partial(_gqa_kernel, n_seg=len(segs), group=group, kvg=kvg),
        out_shape=jax.ShapeDtypeStruct((m, heads * HEAD_DIM), BF16),
        grid=(grp.batch, kv_heads // kvg, nq),
        in_specs=in_specs,
        out_specs=pl.BlockSpec((tq, gw), lambda b, h, i: (b * nq + i, h)),
        compiler_params=_params("arbitrary", "arbitrary", "arbitrary"),
        name="attn_gqa",
    )(*args)


def _diff_kernel(*refs, n_seg, hg, lambda_init):
    q_ref, lq1_ref, lk1_ref, lq2_ref, lk2_ref, g_ref = refs[:6]
    o_ref = refs[-1]
    lam = (jnp.exp(jnp.sum(lq1_ref[...] * lk1_ref[...], axis=-1, keepdims=True))
           - jnp.exp(jnp.sum(lq2_ref[...] * lk2_ref[...], axis=-1, keepdims=True)) + lambda_init)
    hw = 2 * HEAD_DIM
    for h in range(hg):
        vs = [refs[7 + 2 * i][:, h * hw:(h + 1) * hw].astype(BF16) for i in range(n_seg)]
        halves = []
        for j in range(2):
            sl = slice(h * hw + j * HEAD_DIM, h * hw + (j + 1) * HEAD_DIM)
            ks = [refs[6 + 2 * i][:, sl].astype(BF16) for i in range(n_seg)]
            o, l = _attend(q_ref[:, sl], ks, vs)
            halves.append(o * (1.0 / l))
        o = halves[0] - lam * halves[1]
        o_ref[:, h * hw:(h + 1) * hw] = (_rms(o, g_ref[...]) * (1.0 - lambda_init)).astype(o_ref.dtype)


def _diff_attention(q, segs, lams, g_sub, grp, heads, tq, hg, lambda_init):
    m = q.shape[0]
    nq = grp.seq // tq
    hw = 2 * HEAD_DIM
    vec = lambda w: pl.BlockSpec((1, w), lambda b, h, i: (0, 0))
    in_specs = [pl.BlockSpec((tq, hg * hw), lambda b, h, i: (b * nq + i, h))]
    in_specs += [vec(HEAD_DIM)] * 4 + [vec(hw)]
    args = [q, *lams, g_sub]
    for k, v, rows in segs:
        in_specs.append(pl.BlockSpec((rows, hg * hw), lambda b, h, i: (b, h)))
        in_specs.append(pl.BlockSpec((rows, hg * hw), lambda b, h, i: (b, h)))
        args += [k, v]
    return pl.pallas_call(
        functools.partial(_diff_kernel, n_seg=len(segs), hg=hg, lambda_init=lambda_init),
        out_shape=jax.ShapeDtypeStruct((m, heads * hw), BF16),
        grid=(grp.batch, heads // hg, nq),
        in_specs=in_specs,
        out_specs=pl.BlockSpec((tq, hg * hw), lambda b, h, i: (b * nq + i, h)),
        compiler_params=_params("arbitrary", "arbitrary", "arbitrary"),
        name="attn_diff",
    )(*args)


def _mm_res_kernel(*refs, n_a, nk, nb, out_norm, next_norm):
    a_refs = refs[:n_a]
    w_refs = refs[n_a:2 * n_a]
    x_ref, gt_ref = refs[2 * n_a:2 * n_a + 2]
    o_ref = refs[-2] if next_norm else refs[-1]
    seg = o_ref.shape[0] // nb
    if nk == 1:
        ws = [w_ref[...].astype(BF16) for w_ref in w_refs]
        for r in range(nb):
            rows = slice(r * seg, (r + 1) * seg)
            part = None
            for a_ref, w in zip(a_refs, ws):
                p = jnp.dot(a_ref[rows, :], w, preferred_element_type=F32)
                part = p if part is None else part + p
            o_ref[rows, :] = x_ref[rows, :] + gt_ref[r] * part
        return
    k = pl.program_id(2)

    def step(first):
        w32 = w_refs[0][...]
        for r in range(nb):
            rows = slice(r * seg, (r + 1) * seg)
            w = (w32 * gt_ref[r]).astype(BF16)
            p = jnp.dot(a_refs[0][rows, :], w, preferred_element_type=F32)
            if first:
                o_ref[rows, :] = x_ref[rows, :] + p
            else:
                o_ref[rows, :] += p

    @pl.when(k == 0)
    def _():
        step(True)

    @pl.when(k > 0)
    def _():
        step(False)

    if out_norm:
        gf_ref = refs[2 * n_a + 2]

        @pl.when(k == nk - 1)
        def _():
            for r0 in range(0, o_ref.shape[0], NORM_ROWS):
                rows = slice(r0, r0 + NORM_ROWS)
                o_ref[rows, :] = _rms(o_ref[rows, :], gf_ref[...])

    if next_norm:
        gn_ref, sc_ref, sh_ref = refs[2 * n_a + 2:2 * n_a + 5]

        @pl.when(k == nk - 1)
        def _():
            _norm_mod_into(refs[-1], o_ref, gn_ref, sc_ref, sh_ref)


def _mm_res(a_list, w3, w_layer, x, mod, layer, chunk, grp, tk, tm=2048, tn=512, out_gain=None,
            next_norm=None):
    m, n = x.shape
    if out_gain is not None or next_norm is not None:
        tn = n
    n_a = len(a_list)
    ka = a_list[0].shape[1]
    if n_a > 1:
        tk = ka
    nk = ka // tk
    ncol = n // tn
    in_specs = [pl.BlockSpec((tm, tk), lambda i, j, k: (i, k)) for _ in a_list]
    for idx in range(n_a):
        in_specs.append(pl.BlockSpec((None, tk, tn),
                                     lambda i, j, k, idx=idx: (w_layer, idx * (ka // tk) + k, j)))
    in_specs.append(pl.BlockSpec((tm, tn), lambda i, j, k: (i, j)))
    in_specs.append(grp.mod_spec(layer, lambda i, j, k: chunk * ncol + j, tn, tm))
    args = [*a_list, *([w3] * n_a), x, mod]
    if out_gain is not None:
        assert nk > 1 and ncol == 1
        in_specs.append(pl.BlockSpec((1, n), lambda i, j, k: (0, 0)))
        args.append(out_gain.reshape(1, n))
    out_shape = [jax.ShapeDtypeStruct((m, n), F32)]
    out_specs = [pl.BlockSpec((tm, tn), lambda i, j, k: (i, j))]
    if next_norm is not None:
        assert nk > 1 and ncol == 1 and out_gain is None
        g_next, layer_next, chunk_sc, chunk_sh = next_norm
        in_specs += grp.norm_specs(layer_next, chunk_sc, chunk_sh, n, tm)
        args += [g_next, mod, mod]
        out_shape.append(jax.ShapeDtypeStruct((m, n), BF16))
        out_specs.append(pl.BlockSpec((tm, n), lambda i, j, k: (i, 0)))
    res = pl.pallas_call(
        functools.partial(_mm_res_kernel, n_a=n_a, nk=nk, nb=grp.mod_rows(tm),
                          out_norm=out_gain is not None, next_norm=next_norm is not None),
        out_shape=out_shape,
        grid=(m // tm, ncol, nk),
        in_specs=in_specs,
        out_specs=out_specs,
        compiler_params=_params("arbitrary", "arbitrary", "arbitrary"),
        name="mm_res",
    )(*args)
    return res if next_norm is not None else res[0]


def _ffn_up_kernel(x_ref, gn_ref, sc_ref, sh_ref, wg_ref, wv_ref, cg_ref, cv_ref, bg_ref, bv_ref,
                   o_ref, h_ref, *, seq):
    @pl.when(pl.program_id(1) == 0)
    def _():
        _norm_mod_into(h_ref, x_ref, gn_ref, sc_ref, sh_ref)

    h = h_ref[...]
    tm = h.shape[0]

    def taps(w_ref):
        u = jnp.dot(h, w_ref[...].astype(BF16), preferred_element_type=F32)
        return pltpu.roll(u, 1, 0), u, pltpu.roll(u, tm - 1, 0)

    def conv(t, c_ref, b_ref):
        return t[0] * c_ref[0:1, :] + t[1] * c_ref[1:2, :] + t[2] * c_ref[2:3, :] + b_ref[...]

    tg, tv = taps(wg_ref), taps(wv_ref)
    o_ref[...] = (_silu(conv(tg, cg_ref, bg_ref)) * conv(tv, cv_ref, bv_ref)).astype(o_ref.dtype)
    sub = lax.broadcasted_iota(jnp.int32, (FFN_EDGE_ROWS, 1), 0)
    for r0 in sorted({r for b in range(0, tm, seq) for r in (b, b + seq - FFN_EDGE_ROWS)}):
        rows = slice(r0, r0 + FFN_EDGE_ROWS)

        def edge(t):
            prev, cur, nxt = (a[rows, :] for a in t)
            if r0 % seq == 0:
                prev = jnp.where(sub == 0, 0.0, prev)
            if (r0 + FFN_EDGE_ROWS) % seq == 0:
                nxt = jnp.where(sub == FFN_EDGE_ROWS - 1, 0.0, nxt)
            return prev, cur, nxt

        o_ref[rows, :] = (_silu(conv(edge(tg), cg_ref, bg_ref))
                          * conv(edge(tv), cv_ref, bv_ref)).astype(o_ref.dtype)


def _ffn_up(x, norm, w_up, w_conv, b_conv, layer, grp, tm=1024, tn=512):
    m, d = x.shape
    d_ff = w_up.shape[2] // 2
    nj = d_ff // tn
    assert tm % grp.seq == 0 and m % tm == 0
    g_norm, mod, chunk_sc, chunk_sh = norm
    b3 = b_conv.reshape(b_conv.shape[0], 1, b_conv.shape[1])
    return pl.pallas_call(
        functools.partial(_ffn_up_kernel, seq=grp.seq),
        out_shape=jax.ShapeDtypeStruct((m, d_ff), BF16),
        grid=(m // tm, nj),
        in_specs=[
            pl.BlockSpec((tm, d), lambda i, j: (i, 0)),
            *grp.norm_specs(layer, chunk_sc, chunk_sh, d, tm),
            pl.BlockSpec((None, d, tn), lambda i, j: (layer, 0, j)),
            pl.BlockSpec((None, d, tn), lambda i, j: (layer, 0, nj + j)),
            pl.BlockSpec((None, w_conv.shape[1], tn), lambda i, j: (layer, 0, j)),
            pl.BlockSpec((None, w_conv.shape[1], tn), lambda i, j: (layer, 0, nj + j)),
            pl.BlockSpec((None, 1, tn), lambda i, j: (layer, 0, j)),
            pl.BlockSpec((None, 1, tn), lambda i, j: (layer, 0, nj + j)),
        ],
        out_specs=pl.BlockSpec((tm, tn), lambda i, j: (i, j)),
        scratch_shapes=[pltpu.VMEM((tm, d), BF16)],
        compiler_params=_params("arbitrary", "arbitrary"),
        name="ffn_up",
    )(x, g_norm, mod, mod, w_up, w_up, w_conv, w_conv, b3, b3)


def _rope_tables(n_tokens, rot_dim):
    n_rows = n_tokens // GRID_W
    row = jnp.repeat(jnp.arange(n_rows), GRID_W).astype(F32)
    col = jnp.tile(jnp.arange(GRID_W), n_rows).astype(F32)
    n_freq = rot_dim // 4
    freqs = ROPE_BASE ** (-jnp.arange(n_freq, dtype=F32) / n_freq)
    ang = jnp.concatenate([row[:, None] * freqs, col[:, None] * freqs], axis=-1)
    cos, sin = jnp.cos(ang), jnp.sin(ang)
    reps = LANES // rot_dim
    c = jnp.tile(jnp.concatenate([cos, cos], axis=-1), (1, reps))
    s = jnp.tile(jnp.concatenate([-sin, sin], axis=-1), (1, reps))
    return c, s


def _trunk(x, grp, mod, caches, P):
    ckv_p = krope_p = kb_p = vb_p = k_c = v_c = h_next = None
    depth = P["w_ffn_up"].shape[0]
    for l in range(depth):
        i = l // 2
        norm_mix = (P["g_norm_mix"], mod, l, 1, 0)
        if l % 2 == 0:
            (q_mla, kv, ckv, krope, krope_pad, qb, kb, vb) = _inproj_ab(
                x, norm_mix, [w[i] for w in P["w_ab"]],
                P["g_mla_q"][i:i + 1], P["g_mla_kv"][i:i + 1], P["g_gqa_q"][i:i + 1],
                P["g_gqa_k"][i:i + 1], grp, P["dims_ab"])
            ckv_p, krope_p, kb_p, vb_p = ckv, krope, kb, vb
            heads = P["dims_ab"][5]
            segs_a, segs_b = [], []
            if caches is not None:
                c_ckv, c_kr, c_k, c_v = (cc[i] for cc in caches[:4])
                past = c_ckv.shape[0] // grp.batch
                segs_a.append((_kvup(c_ckv, P["w_ab"][4][i]), c_kr, past))
                segs_b.append((c_k, c_v, past))
            segs_a.append((kv, krope_pad, grp.seq))
            segs_b.append((kb, vb, grp.seq))
            tq = min(grp.seq, 512)
            out_a = _mla_attention(q_mla, segs_a, grp, heads, tq, heads)
            out_b = _gqa_attention(qb, segs_b, grp, qb.shape[1] // HEAD_DIM, GQA_KV_HEADS, tq,
                                   GQA_KV_HEADS)
            x = _mm_res([out_a, out_b], P["w_out_ab"], i, x, mod, l, 2, grp, tk=None)
        else:
            w_in_c = P["w_in_c"][i:i + 1]
            nq = w_in_c.shape[2] // 3
            kv_dtype = F32 if caches is None else BF16
            h = h_next
            q = _proj(h, w_in_c, 0, nq, BF16, grp, True, scale=HEAD_DIM ** -0.5 * LOG2E)
            k = _proj(h, w_in_c, nq, nq, kv_dtype, grp, True)
            v = _proj(h, w_in_c, 2 * nq, nq, kv_dtype, grp, False)
            k_c, v_c = k, v
            segs = []
            if caches is not None:
                c_k, c_v = (cc[i] for cc in caches[4:])
                segs.append((c_k, c_v, c_k.shape[0] // grp.batch))
            segs.append((k, v, grp.seq))
            lambda_init = 0.8 - 0.6 * math.exp(-0.3 * l)
            lams = [P[n][i:i + 1] for n in ("lambda_q1", "lambda_k1", "lambda_q2", "lambda_k2")]
            dheads = nq // (2 * HEAD_DIM)
            o = _diff_attention(q, segs, lams, P["g_diff_sub"][i:i + 1], grp, dheads,
                                min(grp.seq, 1024), dheads if grp.seq <= 256 else 2, lambda_init)
            x = _mm_res([o], P["w_out_c"], i, x, mod, l, 2, grp, tk=o.shape[1])
        g = _ffn_up(x, (P["g_norm_ffn"], mod, 4, 3), P["w_ffn_up"], P["w_ffn_conv"],
                    P["b_ffn_conv"], l, grp)
        if l + 1 < depth:
            x, h_next = _mm_res([g], P["w_ffn_down"], l, x, mod, l, 5, grp, tk=512, tm=1024,
                                next_norm=(P["g_norm_mix"], l + 1, 1, 0))
        else:
            x = _mm_res([g], P["w_ffn_down"], l, x, mod, l, 5, grp, tk=512, tm=1024,
                        out_gain=P["g_final"])
    return x, (ckv_p, krope_p, kb_p, vb_p), (k_c, v_c)


def kernel(x_prompt, x_sample, cache_mla_ckv, cache_mla_krope, cache_gqa_k, cache_gqa_v,
           cache_diff_k, cache_diff_v, c, c_ctx, w_ada, b_ada, g_norm_mix, g_norm_ffn,
           w_in_ab, g_mla_q, w_mla_q_up, g_mla_kv, w_mla_kv_up, g_gqa_q, g_gqa_k, w_out_ab,
           w_in_c, lambda_q1, lambda_k1, lambda_q2, lambda_k2, g_diff_sub, w_out_c,
           w_ffn_up, w_ffn_conv, b_ffn_conv, w_ffn_down, g_final):
    bp, tp, d = x_prompt.shape
    bs, ts, _ = x_sample.shape
    depth = w_ada.shape[0]
    n_even = w_in_ab.shape[0]
    assert depth == 2 and n_even == 1 and w_in_c.shape[0] == 1, "one MLA/GQA layer + one diff layer"
    past = cache_mla_ckv.shape[2]
    n_qlat = g_mla_q.shape[1]
    n_ckv = g_mla_kv.shape[1]
    mla_heads = w_mla_q_up.shape[2] // (MLA_NOPE + MLA_ROPE)
    n_kb = cache_gqa_k.shape[3] * HEAD_DIM
    n_vb = n_kb
    n_qb = w_in_ab.shape[2] - n_qlat - n_ckv - MLA_ROPE - n_kb - n_vb

    rows = 8
    cond = jnp.zeros((rows, d), F32).at[:bs].set(c).at[bs].set(c_ctx)
    mod = _modulation(cond, w_ada, b_ada)
    mod = mod.reshape(depth, rows, 1, mod.shape[2])

    o0 = n_qlat + n_ckv
    o1 = o0 + MLA_ROPE
    w_a = w_in_ab[:, :, :o0].astype(BF16)
    w_b = w_in_ab[:, :, o1:].astype(BF16)
    w_k = jnp.pad(w_in_ab[:, :, o0:o1], ((0, 0), (0, 0), (0, LANES - MLA_ROPE))).astype(BF16)
    wq = w_mla_q_up.reshape(n_even, n_qlat, mla_heads, MLA_NOPE + MLA_ROPE)
    wq = wq * ((MLA_NOPE + MLA_ROPE) ** -0.5 * LOG2E)
    w_q_up_p = jnp.pad(wq, ((0, 0), (0, 0), (0, 0), (0, LANES - MLA_ROPE))).reshape(
        n_even, n_qlat, mla_heads * (MLA_NOPE + LANES)).astype(BF16)
    P = dict(
        w_ab=(w_a, w_b, w_k, w_q_up_p, w_mla_kv_up.astype(BF16)),
        g_mla_q=g_mla_q, g_mla_kv=g_mla_kv, g_gqa_q=g_gqa_q * (HEAD_DIM ** -0.5 * LOG2E),
        g_gqa_k=g_gqa_k,
        w_out_ab=w_out_ab, w_in_c=w_in_c, lambda_q1=lambda_q1, lambda_k1=lambda_k1,
        lambda_q2=lambda_q2, lambda_k2=lambda_k2, g_diff_sub=g_diff_sub, w_out_c=w_out_c,
        w_ffn_up=w_ffn_up, w_ffn_conv=w_ffn_conv, b_ffn_conv=b_ffn_conv, w_ffn_down=w_ffn_down,
        g_final=g_final,
        g_norm_mix=g_norm_mix.reshape(depth, 1, d), g_norm_ffn=g_norm_ffn.reshape(depth, 1, d),
        dims_ab=(n_qlat, n_ckv, n_qb, n_kb, n_vb, mla_heads),
    )

    grp_p = _Group(bp, tp, bs, False, None)
    y_p, ab_p, c_p = _trunk(x_prompt.reshape(bp * tp, d), grp_p, mod, None, P)

    c128, s128 = _rope_tables(ts, HEAD_DIM)
    c64, s64 = _rope_tables(ts, MLA_ROPE)
    grp_s = _Group(bs, ts, 0, True, dict(c128=c128, s128=s128, c64=c64, s64=s64))
    n_odd = cache_diff_k.shape[1]
    kr_pad = jnp.pad(cache_mla_krope, ((0, 0), (0, 0), (0, 0), (0, LANES - MLA_ROPE))).astype(BF16)
    per_layer = lambda a, n: [a[:, j].reshape(bs * past, -1) for j in range(n)]
    caches = (per_layer(cache_mla_ckv, n_even), per_layer(kr_pad, n_even),
              per_layer(cache_gqa_k, n_even), per_layer(cache_gqa_v, n_even),
              per_layer(cache_diff_k.astype(BF16), n_odd),
              per_layer(cache_diff_v.astype(BF16), n_odd))
    y_s, _, _ = _trunk(x_sample.reshape(bs * ts, d), grp_s, mod, caches, P)

    ckv, krope, kb, vb = ab_p
    k_c, v_c = c_p
    kvh = n_kb // HEAD_DIM
    dh = k_c.shape[1] // (2 * HEAD_DIM)
    return (y_p.reshape(bp, tp, d), y_s.reshape(bs, ts, d),
            ckv.reshape(bp, 1, tp, n_ckv), krope.reshape(bp, 1, tp, MLA_ROPE),
            kb.reshape(bp, 1, tp, kvh, HEAD_DIM), vb.reshape(bp, 1, tp, kvh, HEAD_DIM),
            k_c.reshape(bp, 1, tp, dh, 2, HEAD_DIM), v_c.reshape(bp, 1, tp, dh, 2 * HEAD_DIM))
```

```python
import functools
import math

import jax
import jax.numpy as jnp
from jax import lax
from jax.experimental import pallas as pl
from jax.experimental.pallas import tpu as pltpu

F32 = jnp.float32
BF16 = jnp.bfloat16

HEAD_DIM = 128
GRID_W = 64
ROPE_BASE = 10000.0
EPS = 1e-6
MLA_NOPE = 128
MLA_ROPE = 64
MLA_V = 128
GQA_KV_HEADS = 2
LOG2E = 1.4426950408889634
LANES = 128
FFN_EDGE_ROWS = 16
NORM_ROWS = 32
VMEM_LIMIT = 56 * 1024 * 1024


def _params(*sem):
    return pltpu.CompilerParams(dimension_semantics=sem, vmem_limit_bytes=VMEM_LIMIT)


def _rms(x, g):
    ms = jnp.mean(x * x, axis=-1, keepdims=True)
    return x * lax.rsqrt(ms + EPS) * g


def _silu(x):
    return x / (1.0 + jnp.exp(-x))


def _rope_half(x, c, s, half):
    if 2 * half == LANES:
        sw = pltpu.roll(x, half, 1)
    else:
        lane = lax.broadcasted_iota(jnp.int32, x.shape, 1)
        first = (lane % (2 * half)) < half
        sw = jnp.where(first, pltpu.roll(x, LANES - half, 1), pltpu.roll(x, half, 1))
    return x * c + sw * s


def _mod_kernel(cond_ref, w_ref, b_ref, o_ref):
    s = _silu(cond_ref[...]).astype(BF16)
    o_ref[...] = jnp.dot(s, w_ref[...].astype(BF16), preferred_element_type=F32) + b_ref[...]


def _modulation(cond, w_ada, b_ada, tn=1024):
    depth, d, n = w_ada.shape
    rows = cond.shape[0]
    return pl.pallas_call(
        _mod_kernel,
        out_shape=jax.ShapeDtypeStruct((depth, rows, n), F32),
        grid=(depth, n // tn),
        in_specs=[
            pl.BlockSpec((rows, d), lambda l, j: (0, 0)),
            pl.BlockSpec((None, d, tn), lambda l, j: (l, 0, j)),
            pl.BlockSpec((None, 1, tn), lambda l, j: (l, 0, j)),
        ],
        out_specs=pl.BlockSpec((None, rows, tn), lambda l, j: (l, 0, j)),
        compiler_params=_params("arbitrary", "arbitrary"),
        name="adaln_mod",
    )(cond, w_ada, b_ada.reshape(depth, 1, n))


def _norm_mod_tile(x_ref, g_ref, sc_ref, sh_ref):
    y = _rms(x_ref[...], g_ref[...])
    return (y * (1.0 + sc_ref[...]) + sh_ref[...]).astype(BF16)


def _norm_mod_into(h_ref, x_ref, g_ref, sc_ref, sh_ref):
    g, sc1, sh = g_ref[...], 1.0 + sc_ref[...], sh_ref[...]
    for r0 in range(0, x_ref.shape[0], NORM_ROWS):
        rows = slice(r0, r0 + NORM_ROWS)
        h_ref[rows, :] = (_rms(x_ref[rows, :], g) * sc1 + sh).astype(BF16)


class _Group:
    def __init__(self, batch, seq, mod_row0, per_batch_mod, rope):
        self.batch = batch
        self.seq = seq
        self.mod_row0 = mod_row0
        self.per_batch_mod = per_batch_mod
        self.rope = rope

    def mod_rows(self, tm):
        if self.per_batch_mod and tm > self.seq:
            assert tm % self.seq == 0 and self.mod_row0 % (tm // self.seq) == 0
            return tm // self.seq
        return 1

    def mod_spec(self, layer, col, width, tm):
        nb = self.mod_rows(tm)
        if self.per_batch_mod:
            assert tm % self.seq == 0 or self.seq % tm == 0
            row = lambda i: (self.mod_row0 + (i * tm) // self.seq) // nb
        else:
            row = lambda i: self.mod_row0
        col_fn = col if callable(col) else (lambda *idx: col)
        return pl.BlockSpec((None, nb, 1, width),
                            lambda *idx: (layer, row(idx[0]), 0, col_fn(*idx)))

    def norm_specs(self, layer, chunk_sc, chunk_sh, d, tm):
        assert self.mod_rows(tm) == 1
        vec = lambda chunk: pl.BlockSpec(
            (None, None, 1, d), self.mod_spec(layer, chunk, d, tm).index_map)
        return [pl.BlockSpec((None, 1, d), lambda *idx: (layer, 0, 0)), vec(chunk_sc), vec(chunk_sh)]


def _inproj_ab_kernel(*refs, rope, n_qlat, n_ckv, n_qb, n_kb, n_vb, mla_heads):
    (x_ref, gn_ref, sc_ref, sh_ref, wa_ref, wb_ref, wk_ref, wq_ref, wkv_ref,
     gq_ref, gkv_ref, gqn_ref, gkn_ref) = refs[:13]
    (qmla_ref, kv_ref, ckv_ref, kr_ref, krp_ref, qb_ref, kb_ref, vb_ref) = refs[-8:]
    if rope:
        c128, s128, c64, s64 = (r[...] for r in refs[13:17])
    h = _norm_mod_tile(x_ref, gn_ref, sc_ref, sh_ref)

    ra = jnp.dot(h, wa_ref[...], preferred_element_type=F32)
    rb = jnp.dot(h, wb_ref[...], preferred_element_type=F32)
    qn = _rms(ra[:, :n_qlat], gq_ref[...]).astype(BF16)
    qa = jnp.dot(qn, wq_ref[...], preferred_element_type=F32)
    hw = MLA_NOPE + LANES
    for i in range(mla_heads):
        qmla_ref[:, i * hw:i * hw + MLA_NOPE] = qa[:, i * hw:i * hw + MLA_NOPE].astype(BF16)
        rp = qa[:, i * hw + MLA_NOPE:(i + 1) * hw]
        if rope:
            rp = _rope_half(rp, c64, s64, MLA_ROPE // 2)
        qmla_ref[:, i * hw + MLA_NOPE:(i + 1) * hw] = rp.astype(BF16)
    cn = _rms(ra[:, n_qlat:n_qlat + n_ckv], gkv_ref[...])
    ckv_ref[...] = cn
    kv_ref[...] = jnp.dot(cn.astype(BF16), wkv_ref[...], preferred_element_type=F32).astype(BF16)
    r = rb[:, :n_qb]
    for i in range(n_qb // HEAD_DIM):
        y = _rms(r[:, i * HEAD_DIM:(i + 1) * HEAD_DIM], gqn_ref[...])
        if rope:
            y = _rope_half(y, c128, s128, HEAD_DIM // 2)
        qb_ref[:, i * HEAD_DIM:(i + 1) * HEAD_DIM] = y.astype(BF16)
    r = rb[:, n_qb:n_qb + n_kb]
    for i in range(n_kb // HEAD_DIM):
        y = _rms(r[:, i * HEAD_DIM:(i + 1) * HEAD_DIM], gkn_ref[...])
        if rope:
            y = _rope_half(y, c128, s128, HEAD_DIM // 2)
        kb_ref[:, i * HEAD_DIM:(i + 1) * HEAD_DIM] = y
    vb_ref[...] = rb[:, n_qb + n_kb:n_qb + n_kb + n_vb]
    r = jnp.dot(h, wk_ref[...], preferred_element_type=F32)
    if rope:
        r = _rope_half(r, c64, s64, MLA_ROPE // 2)
    kr_ref[...] = r[:, :MLA_ROPE]
    krp_ref[...] = r.astype(BF16)


def _inproj_ab(x, norm, weights, g_q, g_kv, g_qn, g_kn, grp, dims, tm=256):
    m, d = x.shape
    n_qlat, n_ckv, n_qb, n_kb, n_vb, mla_heads = dims
    g_norm, mod, layer, chunk_sc, chunk_sh = norm
    rope = grp.rope is not None
    one = pl.Buffered(1)
    nq_out = mla_heads * (MLA_NOPE + LANES)
    nkv_out = weights[4].shape[1]
    in_specs = [pl.BlockSpec((tm, d), lambda i: (i, 0))]
    in_specs += grp.norm_specs(layer, chunk_sc, chunk_sh, d, tm)
    in_specs += [pl.BlockSpec(w.shape, lambda i: (0, 0), pipeline_mode=one) for w in weights]
    in_specs += [
        pl.BlockSpec((1, n_qlat), lambda i: (0, 0)),
        pl.BlockSpec((1, n_ckv), lambda i: (0, 0)),
        pl.BlockSpec((1, HEAD_DIM), lambda i: (0, 0)),
        pl.BlockSpec((1, HEAD_DIM), lambda i: (0, 0)),
    ]
    args = [x, g_norm, mod, mod, *weights, g_q, g_kv, g_qn, g_kn]
    if rope:
        per = grp.seq // tm
        for name in ("c128", "s128", "c64", "s64"):
            in_specs.append(pl.BlockSpec((tm, LANES), lambda i: (i % per, 0)))
            args.append(grp.rope[name])
    widths = [(nq_out, BF16), (nkv_out, BF16), (n_ckv, F32), (MLA_ROPE, F32), (LANES, BF16),
              (n_qb, BF16), (n_kb, F32), (n_vb, F32)]
    return pl.pallas_call(
        functools.partial(_inproj_ab_kernel, rope=rope, n_qlat=n_qlat, n_ckv=n_ckv, n_qb=n_qb,
                          n_kb=n_kb, n_vb=n_vb, mla_heads=mla_heads),
        out_shape=[jax.ShapeDtypeStruct((m, w), dt) for w, dt in widths],
        grid=(m // tm,),
        in_specs=in_specs,
        out_specs=[pl.BlockSpec((tm, w), lambda i: (i, 0)) for w, _ in widths],
        compiler_params=_params("arbitrary"),
        name="inproj_ab",
    )(*args)


def _kvup_kernel(c_ref, w_ref, o_ref):
    o_ref[...] = jnp.dot(c_ref[...].astype(BF16), w_ref[...],
                         preferred_element_type=F32).astype(o_ref.dtype)


def _kvup(ckv, wkv, tm=512):
    m, k = ckv.shape
    n = wkv.shape[1]
    return pl.pallas_call(
        _kvup_kernel,
        out_shape=jax.ShapeDtypeStruct((m, n), BF16),
        grid=(m // tm,),
        in_specs=[pl.BlockSpec((tm, k), lambda i: (i, 0)),
                  pl.BlockSpec((k, n), lambda i: (0, 0))],
        out_specs=pl.BlockSpec((tm, n), lambda i: (i, 0)),
        compiler_params=_params("arbitrary"),
        name="mla_kv_up",
    )(ckv, wkv)


def _proj_kernel(*refs, rope, scale):
    if rope:
        h_ref, w_ref, c_ref, s_ref, o_ref = refs
    else:
        h_ref, w_ref, o_ref = refs
    r = jnp.dot(h_ref[...], w_ref[...].astype(BF16), preferred_element_type=F32)
    if rope:
        c, s = c_ref[...], s_ref[...]
        for i in range(r.shape[1] // HEAD_DIM):
            sl = slice(i * HEAD_DIM, (i + 1) * HEAD_DIM)
            y = _rope_half(r[:, sl], c, s, HEAD_DIM // 2)
            o_ref[:, sl] = (y if scale is None else y * scale).astype(o_ref.dtype)
    else:
        o_ref[...] = (r if scale is None else r * scale).astype(o_ref.dtype)


def _proj(h, w3, col0, ncols, out_dtype, grp, use_rope, scale=None, tm=2048, tn=512):
    m, d = h.shape
    rope = use_rope and grp.rope is not None
    tm = min(tm, m)
    off = col0 // tn
    in_specs = [pl.BlockSpec((tm, d), lambda i, j: (i, 0)),
                pl.BlockSpec((None, d, tn), lambda i, j: (0, 0, off + j))]
    args = [h, w3]
    if rope:
        assert tm % grp.seq == 0
        for name in ("c128", "s128"):
            in_specs.append(pl.BlockSpec((tm, LANES), lambda i, j: (0, 0)))
            args.append(jnp.tile(grp.rope[name], (tm // grp.seq, 1)))
    return pl.pallas_call(
        functools.partial(_proj_kernel, rope=rope, scale=scale),
        out_shape=jax.ShapeDtypeStruct((m, ncols), out_dtype),
        grid=(m // tm, ncols // tn),
        in_specs=in_specs,
        out_specs=pl.BlockSpec((tm, tn), lambda i, j: (i, j)),
        compiler_params=_params("arbitrary", "arbitrary"),
        name="proj_c",
    )(*args)


def _softmax_parts(scores):
    mx = jnp.max(scores[0], axis=-1, keepdims=True)
    for s in scores[1:]:
        mx = jnp.maximum(mx, jnp.max(s, axis=-1, keepdims=True))
    ps = [jnp.exp2(s - mx) for s in scores]
    l = jnp.sum(ps[0], axis=-1, keepdims=True)
    for p in ps[1:]:
        l = l + jnp.sum(p, axis=-1, keepdims=True)
    return ps, l


def _attend(q, ks, vs):
    scores = [lax.dot_general(q, k, _NT, preferred_element_type=F32) for k in ks]
    ps, l = _softmax_parts(scores)
    o = jnp.dot(ps[0].astype(BF16), vs[0], preferred_element_type=F32)
    for p, v in zip(ps[1:], vs[1:]):
        o = o + jnp.dot(p.astype(BF16), v, preferred_element_type=F32)
    return o, l


_NT = (((1,), (1,)), ((), ()))


def _mla_kernel(*refs, n_seg, hg):
    q_ref = refs[0]
    o_ref = refs[-1]
    hw = MLA_NOPE + LANES
    for h in range(hg):
        ks, vs = [], []
        for i in range(n_seg):
            kv_ref, kr_ref = refs[1 + 2 * i], refs[2 + 2 * i]
            ks.append(jnp.concatenate([kv_ref[:, h * hw:h * hw + MLA_NOPE], kr_ref[...]], axis=1))
            vs.append(kv_ref[:, h * hw + MLA_NOPE:(h + 1) * hw])
        o, l = _attend(q_ref[:, h * hw:(h + 1) * hw], ks, vs)
        o_ref[:, h * MLA_V:(h + 1) * MLA_V] = (o / l).astype(o_ref.dtype)


def _mla_attention(q, segs, grp, heads, tq, hg):
    m = q.shape[0]
    nq = grp.seq // tq
    hw = MLA_NOPE + LANES
    in_specs = [pl.BlockSpec((tq, hg * hw), lambda b, h, i: (b * nq + i, h))]
    args = [q]
    for kv, kr, rows in segs:
        in_specs.append(pl.BlockSpec((rows, hg * (MLA_NOPE + MLA_V)), lambda b, h, i: (b, h)))
        in_specs.append(pl.BlockSpec((rows, LANES), lambda b, h, i: (b, 0)))
        args += [kv, kr]
    return pl.pallas_call(
        functools.partial(_mla_kernel, n_seg=len(segs), hg=hg),
        out_shape=jax.ShapeDtypeStruct((m, heads * MLA_V), BF16),
        grid=(grp.batch, heads // hg, nq),
        in_specs=in_specs,
        out_specs=pl.BlockSpec((tq, hg * MLA_V), lambda b, h, i: (b * nq + i, h)),
        compiler_params=_params("arbitrary", "arbitrary", "arbitrary"),
        name="attn_mla",
    )(*args)


def _gqa_kernel(*refs, n_seg, group, kvg):
    q_ref = refs[0]
    o_ref = refs[-1]
    for kh in range(kvg):
        ksl = slice(kh * HEAD_DIM, (kh + 1) * HEAD_DIM)
        ks = [refs[1 + 2 * i][:, ksl].astype(BF16) for i in range(n_seg)]
        vs = [refs[2 + 2 * i][:, ksl].astype(BF16) for i in range(n_seg)]
        for g in range(group):
            sl = slice((kh * group + g) * HEAD_DIM, (kh * group + g + 1) * HEAD_DIM)
            o, l = _attend(q_ref[:, sl], ks, vs)
            o_ref[:, sl] = (o / l).astype(o_ref.dtype)


def _gqa_attention(q, segs, grp, heads, kv_heads, tq, kvg):
    m = q.shape[0]
    nq = grp.seq // tq
    group = heads // kv_heads
    gw = kvg * group * HEAD_DIM
    in_specs = [pl.BlockSpec((tq, gw), lambda b, h, i: (b * nq + i, h))]
    args = [q]
    for k, v, rows in segs:
        in_specs.append(pl.BlockSpec((rows, kvg * HEAD_DIM), lambda b, h, i: (b, h)))
        in_specs.append(pl.BlockSpec((rows, kvg * HEAD_DIM), lambda b, h, i: (b, h)))
        args += [k, v]
    return pl.pallas_call(
        functools.partial(_gqa_kernel, n_seg=len(segs), group=group, kvg=kvg),
        out_shape=jax.ShapeDtypeStruct((m, heads * HEAD_DIM), BF16),
        grid=(grp.batch, kv_heads // kvg, nq),
        in_specs=in_specs,
        out_specs=pl.BlockSpec((tq, gw), lambda b, h, i: (b * nq + i, h)),
        compiler_params=_params("arbitrary", "arbitrary", "arbitrary"),
        name="attn_gqa",
    )(*args)


def _diff_kernel(*refs, n_seg, hg, lambda_init):
    q_ref, lq1_ref, lk1_ref, lq2_ref, lk2_ref, g_ref = refs[:6]
    o_ref = refs[-1]
    lam = (jnp.exp(jnp.sum(lq1_ref[...] * lk1_ref[...], axis=-1, keepdims=True))
           - jnp.exp(jnp.sum(lq2_ref[...] * lk2_ref[...], axis=-1, keepdims=True)) + lambda_init)
    hw = 2 * HEAD_DIM
    for h in range(hg):
        vs = [refs[7 + 2 * i][:, h * hw:(h + 1) * hw].astype(BF16) for i in range(n_seg)]
        halves = []
        for j in range(2):
            sl = slice(h * hw + j * HEAD_DIM, h * hw + (j + 1) * HEAD_DIM)
            ks = [refs[6 + 2 * i][:, sl].astype(BF16) for i in range(n_seg)]
            o, l = _attend(q_ref[:, sl], ks, vs)
            halves.append(o * (1.0 / l))
        o = halves[0] - lam * halves[1]
        o_ref[:, h * hw:(h + 1) * hw] = (_rms(o, g_ref[...]) * (1.0 - lambda_init)).astype(o_ref.dtype)


def _diff_attention(q, segs, lams, g_sub, grp, heads, tq, hg, lambda_init):
    m = q.shape[0]
    nq = grp.seq // tq
    hw = 2 * HEAD_DIM
    vec = lambda w: pl.BlockSpec((1, w), lambda b, h, i: (0, 0))
    in_specs = [pl.BlockSpec((tq, hg * hw), lambda b, h, i: (b * nq + i, h))]
    in_specs += [vec(HEAD_DIM)] * 4 + [vec(hw)]
    args = [q, *lams, g_sub]
    for k, v, rows in segs:
        in_specs.append(pl.BlockSpec((rows, hg * hw), lambda b, h, i: (b, h)))
        in_specs.append(pl.BlockSpec((rows, hg * hw), lambda b, h, i: (b, h)))
        args += [k, v]
    return pl.pallas_call(
        functools.partial(_diff_kernel, n_seg=len(segs), hg=hg, lambda_init=lambda_init),
        out_shape=jax.ShapeDtypeStruct((m, heads * hw), BF16),
        grid=(grp.batch, heads // hg, nq),
        in_specs=in_specs,
        out_specs=pl.BlockSpec((tq, hg * hw), lambda b, h, i: (b * nq + i, h)),
        compiler_params=_params("arbitrary", "arbitrary", "arbitrary"),
        name="attn_diff",
    )(*args)


def _mm_res_kernel(*refs, n_a, nk, nb, out_norm, next_norm):
    a_refs = refs[:n_a]
    w_refs = refs[n_a:2 * n_a]
    x_ref, gt_ref = refs[2 * n_a:2 * n_a + 2]
    o_ref = refs[-2] if next_norm else refs[-1]
    seg = o_ref.shape[0] // nb
    if nk == 1:
        ws = [w_ref[...].astype(BF16) for w_ref in w_refs]
        for r in range(nb):
            rows = slice(r * seg, (r + 1) * seg)
            part = None
            for a_ref, w in zip(a_refs, ws):
                p = jnp.dot(a_ref[rows, :], w, preferred_element_type=F32)
                part = p if part is None else part + p
            o_ref[rows, :] = x_ref[rows, :] + gt_ref[r] * part
        return
    k = pl.program_id(2)

    def step(first):
        w32 = w_refs[0][...]
        for r in range(nb):
            rows = slice(r * seg, (r + 1) * seg)
            w = (w32 * gt_ref[r]).astype(BF16)
            p = jnp.dot(a_refs[0][rows, :], w, preferred_element_type=F32)
            if first:
                o_ref[rows, :] = x_ref[rows, :] + p
            else:
                o_ref[rows, :] += p

    @pl.when(k == 0)
    def _():
        step(True)

    @pl.when(k > 0)
    def _():
        step(False)

    if out_norm:
        gf_ref = refs[2 * n_a + 2]

        @pl.when(k == nk - 1)
        def _():
            for r0 in range(0, o_ref.shape[0], NORM_ROWS):
                rows = slice(r0, r0 + NORM_ROWS)
                o_ref[rows, :] = _rms(o_ref[rows, :], gf_ref[...])

    if next_norm:
        gn_ref, sc_ref, sh_ref = refs[2 * n_a + 2:2 * n_a + 5]

        @pl.when(k == nk - 1)
        def _():
            _norm_mod_into(refs[-1], o_ref, gn_ref, sc_ref, sh_ref)


def _mm_res(a_list, w3, w_layer, x, mod, layer, chunk, grp, tk, tm=2048, tn=512, out_gain=None,
            next_norm=None):
    m, n = x.shape
    if out_gain is not None or next_norm is not None:
        tn = n
    n_a = len(a_list)
    ka = a_list[0].shape[1]
    if n_a > 1:
        tk = ka
    nk = ka // tk
    ncol = n // tn
    in_specs = [pl.BlockSpec((tm, tk), lambda i, j, k: (i, k)) for _ in a_list]
    for idx in range(n_a):
        in_specs.append(pl.BlockSpec((None, tk, tn),
                                     lambda i, j, k, idx=idx: (w_layer, idx * (ka // tk) + k, j)))
    in_specs.append(pl.BlockSpec((tm, tn), lambda i, j, k: (i, j)))
    in_specs.append(grp.mod_spec(layer, lambda i, j, k: chunk * ncol + j, tn, tm))
    args = [*a_list, *([w3] * n_a), x, mod]
    if out_gain is not None:
        assert nk > 1 and ncol == 1
        in_specs.append(pl.BlockSpec((1, n), lambda i, j, k: (0, 0)))
        args.append(out_gain.reshape(1, n))
    out_shape = [jax.ShapeDtypeStruct((m, n), F32)]
    out_specs = [pl.BlockSpec((tm, tn), lambda i, j, k: (i, j))]
    if next_norm is not None:
        assert nk > 1 and ncol == 1 and out_gain is None
        g_next, layer_next, chunk_sc, chunk_sh = next_norm
        in_specs += grp.norm_specs(layer_next, chunk_sc, chunk_sh, n, tm)
        args += [g_next, mod, mod]
        out_shape.append(jax.ShapeDtypeStruct((m, n), BF16))
        out_specs.append(pl.BlockSpec((tm, n), lambda i, j, k: (i, 0)))
    res = pl.pallas_call(
        functools.partial(_mm_res_kernel, n_a=n_a, nk=nk, nb=grp.mod_rows(tm),
                          out_norm=out_gain is not None, next_norm=next_norm is not None),
        out_shape=out_shape,
        grid=(m // tm, ncol, nk),
        in_specs=in_specs,
        out_specs=out_specs,
        compiler_params=_params("arbitrary", "arbitrary", "arbitrary"),
        name="mm_res",
    )(*args)
    return res if next_norm is not None else res[0]


def _ffn_up_kernel(x_ref, gn_ref, sc_ref, sh_ref, wg_ref, wv_ref, cg_ref, cv_ref, bg_ref, bv_ref,
                   o_ref, h_ref, *, seq):
    @pl.when(pl.program_id(1) == 0)
    def _():
        _norm_mod_into(h_ref, x_ref, gn_ref, sc_ref, sh_ref)

    h = h_ref[...]
    tm = h.shape[0]

    def taps(w_ref):
        u = jnp.dot(h, w_ref[...].astype(BF16), preferred_element_type=F32)
        return pltpu.roll(u, 1, 0), u, pltpu.roll(u, tm - 1, 0)

    def conv(t, c_ref, b_ref):
        return t[0] * c_ref[0:1, :] + t[1] * c_ref[1:2, :] + t[2] * c_ref[2:3, :] + b_ref[...]

    tg, tv = taps(wg_ref), taps(wv_ref)
    o_ref[...] = (_silu(conv(tg, cg_ref, bg_ref)) * conv(tv, cv_ref, bv_ref)).astype(o_ref.dtype)
    sub = lax.broadcasted_iota(jnp.int32, (FFN_EDGE_ROWS, 1), 0)
    for r0 in sorted({r for b in range(0, tm, seq) for r in (b, b + seq - FFN_EDGE_ROWS)}):
        rows = slice(r0, r0 + FFN_EDGE_ROWS)

        def edge(t):
            prev, cur, nxt = (a[rows, :] for a in t)
            if r0 % seq == 0:
                prev = jnp.where(sub == 0, 0.0, prev)
            if (r0 + FFN_EDGE_ROWS) % seq == 0:
                nxt = jnp.where(sub == FFN_EDGE_ROWS - 1, 0.0, nxt)
            return prev, cur, nxt

        o_ref[rows, :] = (_silu(conv(edge(tg), cg_ref, bg_ref))
                          * conv(edge(tv), cv_ref, bv_ref)).astype(o_ref.dtype)


def _ffn_up(x, norm, w_up, w_conv, b_conv, layer, grp, tm=1024, tn=512):
    m, d = x.shape
    d_ff = w_up.shape[2] // 2
    nj = d_ff // tn
    assert tm % grp.seq == 0 and m % tm == 0
    g_norm, mod, chunk_sc, chunk_sh = norm
    b3 = b_conv.reshape(b_conv.shape[0], 1, b_conv.shape[1])
    return pl.pallas_call(
        functools.partial(_ffn_up_kernel, seq=grp.seq),
        out_shape=jax.ShapeDtypeStruct((m, d_ff), BF16),
        grid=(m // tm, nj),
        in_specs=[
            pl.BlockSpec((tm, d), lambda i, j: (i, 0)),
            *grp.norm_specs(layer, chunk_sc, chunk_sh, d, tm),
            pl.BlockSpec((None, d, tn), lambda i, j: (layer, 0, j)),
            pl.BlockSpec((None, d, tn), lambda i, j: (layer, 0, nj + j)),
            pl.BlockSpec((None, w_conv.shape[1], tn), lambda i, j: (layer, 0, j)),
            pl.BlockSpec((None, w_conv.shape[1], tn), lambda i, j: (layer, 0, nj + j)),
            pl.BlockSpec((None, 1, tn), lambda i, j: (layer, 0, j)),
            pl.BlockSpec((None, 1, tn), lambda i, j: (layer, 0, nj + j)),
        ],
        out_specs=pl.BlockSpec((tm, tn), lambda i, j: (i, j)),
        scratch_shapes=[pltpu.VMEM((tm, d), BF16)],
        compiler_params=_params("arbitrary", "arbitrary"),
        name="ffn_up",
    )(x, g_norm, mod, mod, w_up, w_up, w_conv, w_conv, b3, b3)


def _rope_tables(n_tokens, rot_dim):
    n_rows = n_tokens // GRID_W
    row = jnp.repeat(jnp.arange(n_rows), GRID_W).astype(F32)
    col = jnp.tile(jnp.arange(GRID_W), n_rows).astype(F32)
    n_freq = rot_dim // 4
    freqs = ROPE_BASE ** (-jnp.arange(n_freq, dtype=F32) / n_freq)
    ang = jnp.concatenate([row[:, None] * freqs, col[:, None] * freqs], axis=-1)
    cos, sin = jnp.cos(ang), jnp.sin(ang)
    reps = LANES // rot_dim
    c = jnp.tile(jnp.concatenate([cos, cos], axis=-1), (1, reps))
    s = jnp.tile(jnp.concatenate([-sin, sin], axis=-1), (1, reps))
    return c, s


def _trunk(x, grp, mod, caches, P):
    ckv_p = krope_p = kb_p = vb_p = k_c = v_c = h_next = None
    depth = P["w_ffn_up"].shape[0]
    for l in range(depth):
        i = l // 2
        norm_mix = (P["g_norm_mix"], mod, l, 1, 0)
        if l % 2 == 0:
            (q_mla, kv, ckv, krope, krope_pad, qb, kb, vb) = _inproj_ab(
                x, norm_mix, [w[i] for w in P["w_ab"]],
                P["g_mla_q"][i:i + 1], P["g_mla_kv"][i:i + 1], P["g_gqa_q"][i:i + 1],
                P["g_gqa_k"][i:i + 1], grp, P["dims_ab"])
            ckv_p, krope_p, kb_p, vb_p = ckv, krope, kb, vb
            heads = P["dims_ab"][5]
            segs_a, segs_b = [], []
            if caches is not None:
                c_ckv, c_kr, c_k, c_v = (cc[i] for cc in caches[:4])
                past = c_ckv.shape[0] // grp.batch
                segs_a.append((_kvup(c_ckv, P["w_ab"][4][i]), c_kr, past))
                segs_b.append((c_k, c_v, past))
            segs_a.append((kv, krope_pad, grp.seq))
            segs_b.append((kb, vb, grp.seq))
            tq = min(grp.seq, 512)
            out_a = _mla_attention(q_mla, segs_a, grp, heads, tq, heads)
            out_b = _gqa_attention(qb, segs_b, grp, qb.shape[1] // HEAD_DIM, GQA_KV_HEADS, tq,
                                   GQA_KV_HEADS)
            x = _mm_res([out_a, out_b], P["w_out_ab"], i, x, mod, l, 2, grp, tk=None)
        else:
            w_in_c = P["w_in_c"][i:i + 1]
            nq = w_in_c.shape[2] // 3
            kv_dtype = F32 if caches is None else BF16
            h = h_next
            q = _proj(h, w_in_c, 0, nq, BF16, grp, True, scale=HEAD_DIM ** -0.5 * LOG2E)
            k = _proj(h, w_in_c, nq, nq, kv_dtype, grp, True)
            v = _proj(h, w_in_c, 2 * nq, nq, kv_dtype, grp, False)
            k_c, v_c = k, v
            segs = []
            if caches is not None:
                c_k, c_v = (cc[i] for cc in caches[4:])
                segs.append((c_k, c_v, c_k.shape[0] // grp.batch))
            segs.append((k, v, grp.seq))
            lambda_init = 0.8 - 0.6 * math.exp(-0.3 * l)
            lams = [P[n][i:i + 1] for n in ("lambda_q1", "lambda_k1", "lambda_q2", "lambda_k2")]
            dheads = nq // (2 * HEAD_DIM)
            o = _diff_attention(q, segs, lams, P["g_diff_sub"][i:i + 1], grp, dheads,
                                min(grp.seq, 1024), dheads if grp.seq <= 256 else 2, lambda_init)
            x = _mm_res([o], P["w_out_c"], i, x, mod, l, 2, grp, tk=o.shape[1])
        g = _ffn_up(x, (P["g_norm_ffn"], mod, 4, 3), P["w_ffn_up"], P["w_ffn_conv"],
                    P["b_ffn_conv"], l, grp)
        if l + 1 < depth:
            x, h_next = _mm_res([g], P["w_ffn_down"], l, x, mod, l, 5, grp, tk=512, tm=1024,
                                next_norm=(P["g_norm_mix"], l + 1, 1, 0))
        else:
            x = _mm_res([g], P["w_ffn_down"], l, x, mod, l, 5, grp, tk=512, tm=1024,
                        out_gain=P["g_final"])
    return x, (ckv_p, krope_p, kb_p, vb_p), (k_c, v_c)


def kernel(x_prompt, x_sample, cache_mla_ckv, cache_mla_krope, cache_gqa_k, cache_gqa_v,
           cache_diff_k, cache_diff_v, c, c_ctx, w_ada, b_ada, g_norm_mix, g_norm_ffn,
           w_in_ab, g_mla_q, w_mla_q_up, g_mla_kv, w_mla_kv_up, g_gqa_q, g_gqa_k, w_out_ab,
           w_in_c, lambda_q1, lambda_k1, lambda_q2, lambda_k2, g_diff_sub, w_out_c,
           w_ffn_up, w_ffn_conv, b_ffn_conv, w_ffn_down, g_final):
    bp, tp, d = x_prompt.shape
    bs, ts, _ = x_sample.shape
    depth = w_ada.shape[0]
    n_even = w_in_ab.shape[0]
    assert depth == 2 and n_even == 1 and w_in_c.shape[0] == 1, "one MLA/GQA layer + one diff layer"
    past = cache_mla_ckv.shape[2]
    n_qlat = g_mla_q.shape[1]
    n_ckv = g_mla_kv.shape[1]
    mla_heads = w_mla_q_up.shape[2] // (MLA_NOPE + MLA_ROPE)
    n_kb = cache_gqa_k.shape[3] * HEAD_DIM
    n_vb = n_kb
    n_qb = w_in_ab.shape[2] - n_qlat - n_ckv - MLA_ROPE - n_kb - n_vb

    rows = 8
    cond = jnp.zeros((rows, d), F32).at[:bs].set(c).at[bs].set(c_ctx)
    mod = _modulation(cond, w_ada, b_ada)
    mod = mod.reshape(depth, rows, 1, mod.shape[2])

    o0 = n_qlat + n_ckv
    o1 = o0 + MLA_ROPE
    w_a = w_in_ab[:, :, :o0].astype(BF16)
    w_b = w_in_ab[:, :, o1:].astype(BF16)
    w_k = jnp.pad(w_in_ab[:, :, o0:o1], ((0, 0), (0, 0), (0, LANES - MLA_ROPE))).astype(BF16)
    wq = w_mla_q_up.reshape(n_even, n_qlat, mla_heads, MLA_NOPE + MLA_ROPE)
    wq = wq * ((MLA_NOPE + MLA_ROPE) ** -0.5 * LOG2E)
    w_q_up_p = jnp.pad(wq, ((0, 0), (0, 0), (0, 0), (0, LANES - MLA_ROPE))).reshape(
        n_even, n_qlat, mla_heads * (MLA_NOPE + LANES)).astype(BF16)
    P = dict(
        w_ab=(w_a, w_b, w_k, w_q_up_p, w_mla_kv_up.astype(BF16)),
        g_mla_q=g_mla_q, g_mla_kv=g_mla_kv, g_gqa_q=g_gqa_q * (HEAD_DIM ** -0.5 * LOG2E),
        g_gqa_k=g_gqa_k,
        w_out_ab=w_out_ab, w_in_c=w_in_c, lambda_q1=lambda_q1, lambda_k1=lambda_k1,
        lambda_q2=lambda_q2, lambda_k2=lambda_k2, g_diff_sub=g_diff_sub, w_out_c=w_out_c,
        w_ffn_up=w_ffn_up, w_ffn_conv=w_ffn_conv, b_ffn_conv=b_ffn_conv, w_ffn_down=w_ffn_down,
        g_final=g_final,
        g_norm_mix=g_norm_mix.reshape(depth, 1, d), g_norm_ffn=g_norm_ffn.reshape(depth, 1, d),
        dims_ab=(n_qlat, n_ckv, n_qb, n_kb, n_vb, mla_heads),
    )

    grp_p = _Group(bp, tp, bs, False, None)
    y_p, ab_p, c_p = _trunk(x_prompt.reshape(bp * tp, d), grp_p, mod, None, P)

    c128, s128 = _rope_tables(ts, HEAD_DIM)
    c64, s64 = _rope_tables(ts, MLA_ROPE)
    grp_s = _Group(bs, ts, 0, True, dict(c128=c128, s128=s128, c64=c64, s64=s64))
    n_odd = cache_diff_k.shape[1]
    kr_pad = jnp.pad(cache_mla_krope, ((0, 0), (0, 0), (0, 0), (0, LANES - MLA_ROPE))).astype(BF16)
    per_layer = lambda a, n: [a[:, j].reshape(bs * past, -1) for j in range(n)]
    caches = (per_layer(cache_mla_ckv, n_even), per_layer(kr_pad, n_even),
              per_layer(cache_gqa_k, n_even), per_layer(cache_gqa_v, n_even),
              per_layer(cache_diff_k.astype(BF16), n_odd),
              per_layer(cache_diff_v.astype(BF16), n_odd))
    y_s, _, _ = _trunk(x_sample.reshape(bs * ts, d), grp_s, mod, caches, P)

    ckv, krope, kb, vb = ab_p
    k_c, v_c = c_p
    kvh = n_kb // HEAD_DIM
    dh = k_c.shape[1] // (2 * HEAD_DIM)
    return (y_p.reshape(bp, tp, d), y_s.reshape(bs, ts, d),
            ckv.reshape(bp, 1, tp, n_ckv), krope.reshape(bp, 1, tp, MLA_ROPE),
            kb.reshape(bp, 1, tp, kvh, HEAD_DIM), vb.reshape(bp, 1, tp, kvh, HEAD_DIM),
            k_c.reshape(bp, 1, tp, dh, 2, HEAD_DIM), v_c.reshape(bp, 1, tp, dh, 2 * HEAD_DIM))
```

```python
import functools
import math

import jax
import jax.numpy as jnp
from jax import lax
from jax.experimental import pallas as pl
from jax.experimental.pallas import tpu as pltpu

F32 = jnp.float32
BF16 = jnp.bfloat16

HEAD_DIM = 128
GRID_W = 64
ROPE_BASE = 10000.0
EPS = 1e-6
MLA_NOPE = 128
MLA_ROPE = 64
MLA_V = 128
GQA_KV_HEADS = 2
LOG2E = 1.4426950408889634
LANES = 128
FFN_EDGE_ROWS = 16
NORM_ROWS = 32
VMEM_LIMIT = 56 * 1024 * 1024


def _params(*sem, fuse=None):
    return pltpu.CompilerParams(dimension_semantics=sem, vmem_limit_bytes=VMEM_LIMIT,
                                allow_input_fusion=fuse)


def _rms(x, g):
    ms = jnp.mean(x * x, axis=-1, keepdims=True)
    return x * lax.rsqrt(ms + EPS) * g


def _silu(x):
    return x / (1.0 + jnp.exp(-x))


def _rope_half(x, c, s, half):
    if 2 * half == LANES:
        sw = pltpu.roll(x, half, 1)
    else:
        lane = lax.broadcasted_iota(jnp.int32, x.shape, 1)
        first = (lane % (2 * half)) < half
        sw = jnp.where(first, pltpu.roll(x, LANES - half, 1), pltpu.roll(x, half, 1))
    return x * c + sw * s


def _mod_kernel(cond_ref, w_ref, b_ref, o_ref):
    s = _silu(cond_ref[...]).astype(BF16)
    o_ref[...] = jnp.dot(s, w_ref[...].astype(BF16), preferred_element_type=F32) + b_ref[...]


def _modulation(cond, w_ada, b_ada, tn=1024):
    depth, d, n = w_ada.shape
    rows = cond.shape[0]
    return pl.pallas_call(
        _mod_kernel,
        out_shape=jax.ShapeDtypeStruct((depth, rows, n), F32),
        grid=(depth, n // tn),
        in_specs=[
            pl.BlockSpec((rows, d), lambda l, j: (0, 0)),
            pl.BlockSpec((None, d, tn), lambda l, j: (l, 0, j)),
            pl.BlockSpec((None, 1, tn), lambda l, j: (l, 0, j)),
        ],
        out_specs=pl.BlockSpec((None, rows, tn), lambda l, j: (l, 0, j)),
        compiler_params=_params("arbitrary", "arbitrary"),
        name="adaln_mod",
    )(cond, w_ada, b_ada.reshape(depth, 1, n))


def _norm_mod_tile(x_ref, g_ref, sc_ref, sh_ref):
    y = _rms(x_ref[...], g_ref[...])
    return (y * (1.0 + sc_ref[...]) + sh_ref[...]).astype(BF16)


def _norm_mod_into(h_ref, x_ref, g_ref, sc_ref, sh_ref):
    g, sc1, sh = g_ref[...], 1.0 + sc_ref[...], sh_ref[...]
    for r0 in range(0, x_ref.shape[0], NORM_ROWS):
        rows = slice(r0, r0 + NORM_ROWS)
        h_ref[rows, :] = (_rms(x_ref[rows, :], g) * sc1 + sh).astype(BF16)


class _Group:
    def __init__(self, batch, seq, mod_row0, per_batch_mod, rope):
        self.batch = batch
        self.seq = seq
        self.mod_row0 = mod_row0
        self.per_batch_mod = per_batch_mod
        self.rope = rope

    def mod_rows(self, tm):
        if self.per_batch_mod and tm > self.seq:
            assert tm % self.seq == 0 and self.mod_row0 % (tm // self.seq) == 0
            return tm // self.seq
        return 1

    def mod_spec(self, layer, col, width, tm):
        nb = self.mod_rows(tm)
        if self.per_batch_mod:
            assert tm % self.seq == 0 or self.seq % tm == 0
            row = lambda i: (self.mod_row0 + (i * tm) // self.seq) // nb
        else:
            row = lambda i: self.mod_row0
        col_fn = col if callable(col) else (lambda *idx: col)
        return pl.BlockSpec((None, nb, 1, width),
                            lambda *idx: (layer, row(idx[0]), 0, col_fn(*idx)))

    def norm_specs(self, layer, chunk_sc, chunk_sh, d, tm):
        assert self.mod_rows(tm) == 1
        vec = lambda chunk: pl.BlockSpec(
            (None, None, 1, d), self.mod_spec(layer, chunk, d, tm).index_map)
        return [pl.BlockSpec((None, 1, d), lambda *idx: (layer, 0, 0)), vec(chunk_sc), vec(chunk_sh)]


def _inproj_ab_kernel(*refs, rope, n_qlat, n_ckv, n_qb, n_kb, n_vb, mla_heads):
    (x_ref, gn_ref, sc_ref, sh_ref, wa_ref, wb_ref, wk_ref, wq_ref, wkv_ref,
     gq_ref, gkv_ref, gqn_ref, gkn_ref) = refs[:13]
    (qmla_ref, kv_ref, ckv_ref, kr_ref, krp_ref, qb_ref, kb_ref, vb_ref) = refs[-8:]
    if rope:
        c128, s128, c64, s64 = (r[...] for r in refs[13:17])
    h = _norm_mod_tile(x_ref, gn_ref, sc_ref, sh_ref)

    ra = jnp.dot(h, wa_ref[...], preferred_element_type=F32)
    rb = jnp.dot(h, wb_ref[...], preferred_element_type=F32)
    qn = _rms(ra[:, :n_qlat], gq_ref[...]).astype(BF16)
    qa = jnp.dot(qn, wq_ref[...], preferred_element_type=F32)
    hw = MLA_NOPE + LANES
    for i in range(mla_heads):
        qmla_ref[:, i * hw:i * hw + MLA_NOPE] = qa[:, i * hw:i * hw + MLA_NOPE].astype(BF16)
        rp = qa[:, i * hw + MLA_NOPE:(i + 1) * hw]
        if rope:
            rp = _rope_half(rp, c64, s64, MLA_ROPE // 2)
        qmla_ref[:, i * hw + MLA_NOPE:(i + 1) * hw] = rp.astype(BF16)
    cn = _rms(ra[:, n_qlat:n_qlat + n_ckv], gkv_ref[...])
    ckv_ref[...] = cn
    kv_ref[...] = jnp.dot(cn.astype(BF16), wkv_ref[...], preferred_element_type=F32).astype(BF16)
    r = rb[:, :n_qb]
    for i in range(n_qb // HEAD_DIM):
        y = _rms(r[:, i * HEAD_DIM:(i + 1) * HEAD_DIM], gqn_ref[...])
        if rope:
            y = _rope_half(y, c128, s128, HEAD_DIM // 2)
        qb_ref[:, i * HEAD_DIM:(i + 1) * HEAD_DIM] = y.astype(BF16)
    r = rb[:, n_qb:n_qb + n_kb]
    for i in range(n_kb // HEAD_DIM):
        y = _rms(r[:, i * HEAD_DIM:(i + 1) * HEAD_DIM], gkn_ref[...])
        if rope:
            y = _rope_half(y, c128, s128, HEAD_DIM // 2)
        kb_ref[:, i * HEAD_DIM:(i + 1) * HEAD_DIM] = y
    vb_ref[...] = rb[:, n_qb + n_kb:n_qb + n_kb + n_vb]
    r = jnp.dot(h, wk_ref[...], preferred_element_type=F32)
    if rope:
        r = _rope_half(r, c64, s64, MLA_ROPE // 2)
    kr_ref[...] = r[:, :MLA_ROPE]
    krp_ref[...] = r.astype(BF16)


def _inproj_ab(x, norm, weights, g_q, g_kv, g_qn, g_kn, grp, dims, tm=256):
    m, d = x.shape
    n_qlat, n_ckv, n_qb, n_kb, n_vb, mla_heads = dims
    g_norm, mod, layer, chunk_sc, chunk_sh = norm
    rope = grp.rope is not None
    one = pl.Buffered(1)
    nq_out = mla_heads * (MLA_NOPE + LANES)
    nkv_out = weights[4].shape[1]
    in_specs = [pl.BlockSpec((tm, d), lambda i: (i, 0))]
    in_specs += grp.norm_specs(layer, chunk_sc, chunk_sh, d, tm)
    in_specs += [pl.BlockSpec(w.shape, lambda i: (0, 0), pipeline_mode=one) for w in weights]
    in_specs += [
        pl.BlockSpec((1, n_qlat), lambda i: (0, 0)),
        pl.BlockSpec((1, n_ckv), lambda i: (0, 0)),
        pl.BlockSpec((1, HEAD_DIM), lambda i: (0, 0)),
        pl.BlockSpec((1, HEAD_DIM), lambda i: (0, 0)),
    ]
    args = [x, g_norm, mod, mod, *weights, g_q, g_kv, g_qn, g_kn]
    if rope:
        per = grp.seq // tm
        for name in ("c128", "s128", "c64", "s64"):
            in_specs.append(pl.BlockSpec((tm, LANES), lambda i: (i % per, 0)))
            args.append(grp.rope[name])
    widths = [(nq_out, BF16), (nkv_out, BF16), (n_ckv, F32), (MLA_ROPE, F32), (LANES, BF16),
              (n_qb, BF16), (n_kb, F32), (n_vb, F32)]
    return pl.pallas_call(
        functools.partial(_inproj_ab_kernel, rope=rope, n_qlat=n_qlat, n_ckv=n_ckv, n_qb=n_qb,
                          n_kb=n_kb, n_vb=n_vb, mla_heads=mla_heads),
        out_shape=[jax.ShapeDtypeStruct((m, w), dt) for w, dt in widths],
        grid=(m // tm,),
        in_specs=in_specs,
        out_specs=[pl.BlockSpec((tm, w), lambda i: (i, 0)) for w, _ in widths],
        compiler_params=_params("arbitrary", fuse=[4 <= n < 4 + len(weights) for n in range(len(args))]),
        name="inproj_ab",
    )(*args)


def _kvup_kernel(c_ref, w_ref, o_ref):
    o_ref[...] = jnp.dot(c_ref[...].astype(BF16), w_ref[...],
                         preferred_element_type=F32).astype(o_ref.dtype)


def _kvup(ckv, wkv, tm=512):
    m, k = ckv.shape
    n = wkv.shape[1]
    return pl.pallas_call(
        _kvup_kernel,
        out_shape=jax.ShapeDtypeStruct((m, n), BF16),
        grid=(m // tm,),
        in_specs=[pl.BlockSpec((tm, k), lambda i: (i, 0)),
                  pl.BlockSpec((k, n), lambda i: (0, 0))],
        out_specs=pl.BlockSpec((tm, n), lambda i: (i, 0)),
        compiler_params=_params("arbitrary"),
        name="mla_kv_up",
    )(ckv, wkv)


def _proj_kernel(*refs, rope, scale):
    if rope:
        h_ref, w_ref, c_ref, s_ref, o_ref = refs
    else:
        h_ref, w_ref, o_ref = refs
    r = jnp.dot(h_ref[...], w_ref[...].astype(BF16), preferred_element_type=F32)
    if rope:
        c, s = c_ref[...], s_ref[...]
        for i in range(r.shape[1] // HEAD_DIM):
            sl = slice(i * HEAD_DIM, (i + 1) * HEAD_DIM)
            y = _rope_half(r[:, sl], c, s, HEAD_DIM // 2)
            o_ref[:, sl] = (y if scale is None else y * scale).astype(o_ref.dtype)
    else:
        o_ref[...] = (r if scale is None else r * scale).astype(o_ref.dtype)


def _proj(h, w3, col0, ncols, out_dtype, grp, use_rope, scale=None, tm=2048, tn=512):
    m, d = h.shape
    rope = use_rope and grp.rope is not None
    tm = min(tm, m)
    off = col0 // tn
    in_specs = [pl.BlockSpec((tm, d), lambda i, j: (i, 0)),
                pl.BlockSpec((None, d, tn), lambda i, j: (0, 0, off + j))]
    args = [h, w3]
    if rope:
        assert tm % grp.seq == 0
        for name in ("c128", "s128"):
            in_specs.append(pl.BlockSpec((tm, LANES), lambda i, j: (0, 0)))
            args.append(jnp.tile(grp.rope[name], (tm // grp.seq, 1)))
    return pl.pallas_call(
        functools.partial(_proj_kernel, rope=rope, scale=scale),
        out_shape=jax.ShapeDtypeStruct((m, ncols), out_dtype),
        grid=(m // tm, ncols // tn),
        in_specs=in_specs,
        out_specs=pl.BlockSpec((tm, tn), lambda i, j: (i, j)),
        compiler_params=_params("arbitrary", "arbitrary"),
        name="proj_c",
    )(*args)


def _softmax_parts(scores):
    mx = jnp.max(scores[0], axis=-1, keepdims=True)
    for s in scores[1:]:
        mx = jnp.maximum(mx, jnp.max(s, axis=-1, keepdims=True))
    ps = [jnp.exp2(s - mx) for s in scores]
    l = jnp.sum(ps[0], axis=-1, keepdims=True)
    for p in ps[1:]:
        l = l + jnp.sum(p, axis=-1, keepdims=True)
    return ps, l


def _attend(q, ks, vs):
    scores = [lax.dot_general(q, k, _NT, preferred_element_type=F32) for k in ks]
    ps, l = _softmax_parts(scores)
    o = jnp.dot(ps[0].astype(BF16), vs[0], preferred_element_type=F32)
    for p, v in zip(ps[1:], vs[1:]):
        o = o + jnp.dot(p.astype(BF16), v, preferred_element_type=F32)
    return o, l


_NT = (((1,), (1,)), ((), ()))


def _mla_kernel(*refs, n_seg, hg):
    q_ref = refs[0]
    o_ref = refs[-1]
    hw = MLA_NOPE + LANES
    for h in range(hg):
        ks, vs = [], []
        for i in range(n_seg):
            kv_ref, kr_ref = refs[1 + 2 * i], refs[2 + 2 * i]
            ks.append(jnp.concatenate([kv_ref[:, h * hw:h * hw + MLA_NOPE], kr_ref[...]], axis=1))
            vs.append(kv_ref[:, h * hw + MLA_NOPE:(h + 1) * hw])
        o, l = _attend(q_ref[:, h * hw:(h + 1) * hw], ks, vs)
        o_ref[:, h * MLA_V:(h + 1) * MLA_V] = (o / l).astype(o_ref.dtype)


def _mla_attention(q, segs, grp, heads, tq, hg):
    m = q.shape[0]
    nq = grp.seq // tq
    hw = MLA_NOPE + LANES
    in_specs = [pl.BlockSpec((tq, hg * hw), lambda b, h, i: (b * nq + i, h))]
    args = [q]
    for kv, kr, rows in segs:
        in_specs.append(pl.BlockSpec((rows, hg * (MLA_NOPE + MLA_V)), lambda b, h, i: (b, h)))
        in_specs.append(pl.BlockSpec((rows, LANES), lambda b, h, i: (b, 0)))
        args += [kv, kr]
    return pl.pallas_call(
        functools.partial(_mla_kernel, n_seg=len(segs), hg=hg),
        out_shape=jax.ShapeDtypeStruct((m, heads * MLA_V), BF16),
        grid=(grp.batch, heads // hg, nq),
        in_specs=in_specs,
        out_specs=pl.BlockSpec((tq, hg * MLA_V), lambda b, h, i: (b * nq + i, h)),
        compiler_params=_params("arbitrary", "arbitrary", "arbitrary"),
        name="attn_mla",
    )(*args)


def _gqa_kernel(*refs, n_seg, group, kvg):
    q_ref = refs[0]
    o_ref = refs[-1]
    for kh in range(kvg):
        ksl = slice(kh * HEAD_DIM, (kh + 1) * HEAD_DIM)
        ks = [refs[1 + 2 * i][:, ksl].astype(BF16) for i in range(n_seg)]
        vs = [refs[2 + 2 * i][:, ksl].astype(BF16) for i in range(n_seg)]
        for g in range(group):
            sl = slice((kh * group + g) * HEAD_DIM, (kh * group + g + 1) * HEAD_DIM)
            o, l = _attend(q_ref[:, sl], ks, vs)
            o_ref[:, sl] = (o / l).astype(o_ref.dtype)


def _gqa_attention(q, segs, grp, heads, kv_heads, tq, kvg):
    m = q.shape[0]
    nq = grp.seq // tq
    group = heads // kv_heads
    gw = kvg * group * HEAD_DIM
    in_specs = [pl.BlockSpec((tq, gw), lambda b, h, i: (b * nq + i, h))]
    args = [q]
    for k, v, rows in segs:
        in_specs.append(pl.BlockSpec((rows, kvg * HEAD_DIM), lambda b, h, i: (b, h)))
        in_specs.append(pl.BlockSpec((rows, kvg * HEAD_DIM), lambda b, h, i: (b, h)))
        args += [k, v]
    return pl.pallas_call(
        functools.partial(_gqa_kernel, n_seg=len(segs), group=group, kvg=kvg),
        out_shape=jax.ShapeDtypeStruct((m, heads * HEAD_DIM), BF16),
        grid=(grp.batch, kv_heads // kvg, nq),
        in_specs=in_specs,
        out_specs=pl.BlockSpec((tq, gw), lambda b, h, i: (b * nq + i, h)),
        compiler_params=_params("arbitrary", "arbitrary", "arbitrary"),
        name="attn_gqa",
    )(*args)


def _diff_kernel(*refs, n_seg, hg, lambda_init):
    q_ref, lq1_ref, lk1_ref, lq2_ref, lk2_ref, g_ref = refs[:6]
    o_ref = refs[-1]
    lam = (jnp.exp(jnp.sum(lq1_ref[...] * lk1_ref[...], axis=-1, keepdims=True))
           - jnp.exp(jnp.sum(lq2_ref[...] * lk2_ref[...], axis=-1, keepdims=True)) + lambda_init)
    hw = 2 * HEAD_DIM
    for h in range(hg):
        vs = [refs[7 + 2 * i][:, h * hw:(h + 1) * hw].astype(BF16) for i in range(n_seg)]
        halves = []
        for j in range(2):
            sl = slice(h * hw + j * HEAD_DIM, h * hw + (j + 1) * HEAD_DIM)
            ks = [refs[6 + 2 * i][:, sl].astype(BF16) for i in range(n_seg)]
            o, l = _attend(q_ref[:, sl], ks, vs)
            halves.append(o * (1.0 / l))
        o = halves[0] - lam * halves[1]
        o_ref[:, h * hw:(h + 1) * hw] = (_rms(o, g_ref[...]) * (1.0 - lambda_init)).astype(o_ref.dtype)


def _diff_attention(q, segs, lams, g_sub, grp, heads, tq, hg, lambda_init):
    m = q.shape[0]
    nq = grp.seq // tq
    hw = 2 * HEAD_DIM
    vec = lambda w: pl.BlockSpec((1, w), lambda b, h, i: (0, 0))
    in_specs = [pl.BlockSpec((tq, hg * hw), lambda b, h, i: (b * nq + i, h))]
    in_specs += [vec(HEAD_DIM)] * 4 + [vec(hw)]
    args = [q, *lams, g_sub]
    for k, v, rows in segs:
        in_specs.append(pl.BlockSpec((rows, hg * hw), lambda b, h, i: (b, h)))
        in_specs.append(pl.BlockSpec((rows, hg * hw), lambda b, h, i: (b, h)))
        args += [k, v]
    return pl.pallas_call(
        functools.partial(_diff_kernel, n_seg=len(segs), hg=hg, lambda_init=lambda_init),
        out_shape=jax.ShapeDtypeStruct((m, heads * hw), BF16),
        grid=(grp.batch, heads // hg, nq),
        in_specs=in_specs,
        out_specs=pl.BlockSpec((tq, hg * hw), lambda b, h, i: (b * nq + i, h)),
        compiler_params=_params("arbitrary", "arbitrary", "arbitrary",
                                fuse=[len(segs) > 1 and n in (6, 7) for n in range(len(args))]),
        name="attn_diff",
    )(*args)


def _mm_res_kernel(*refs, n_a, nk, nb, out_norm, next_norm):
    a_refs = refs[:n_a]
    w_refs = refs[n_a:2 * n_a]
    x_ref, gt_ref = refs[2 * n_a:2 * n_a + 2]
    o_ref = refs[-2] if next_norm else refs[-1]
    seg = o_ref.shape[0] // nb
    if nk == 1:
        ws = [w_ref[...].astype(BF16) for w_ref in w_refs]
        for r in range(nb):
            rows = slice(r * seg, (r + 1) * seg)
            part = None
            for a_ref, w in zip(a_refs, ws):
                p = jnp.dot(a_ref[rows, :], w, preferred_element_type=F32)
                part = p if part is None else part + p
            o_ref[rows, :] = x_ref[rows, :] + gt_ref[r] * part
        return
    k = pl.program_id(2)

    def step(first):
        w32 = w_refs[0][...]
        for r in range(nb):
            rows = slice(r * seg, (r + 1) * seg)
            w = (w32 * gt_ref[r]).astype(BF16)
            p = jnp.dot(a_refs[0][rows, :], w, preferred_element_type=F32)
            if first:
                o_ref[rows, :] = x_ref[rows, :] + p
            else:
                o_ref[rows, :] += p

    @pl.when(k == 0)
    def _():
        step(True)

    @pl.when(k > 0)
    def _():
        step(False)

    if out_norm:
        gf_ref = refs[2 * n_a + 2]

        @pl.when(k == nk - 1)
        def _():
            for r0 in range(0, o_ref.shape[0], NORM_ROWS):
                rows = slice(r0, r0 + NORM_ROWS)
                o_ref[rows, :] = _rms(o_ref[rows, :], gf_ref[...])

    if next_norm:
        gn_ref, sc_ref, sh_ref = refs[2 * n_a + 2:2 * n_a + 5]

        @pl.when(k == nk - 1)
        def _():
            _norm_mod_into(refs[-1], o_ref, gn_ref, sc_ref, sh_ref)


def _mm_res(a_list, w3, w_layer, x, mod, layer, chunk, grp, tk, tm=2048, tn=512, out_gain=None,
            next_norm=None):
    m, n = x.shape
    if out_gain is not None or next_norm is not None:
        tn = n
    n_a = len(a_list)
    ka = a_list[0].shape[1]
    if n_a > 1:
        tk = ka
    nk = ka // tk
    ncol = n // tn
    in_specs = [pl.BlockSpec((tm, tk), lambda i, j, k: (i, k)) for _ in a_list]
    for idx in range(n_a):
        in_specs.append(pl.BlockSpec((None, tk, tn),
                                     lambda i, j, k, idx=idx: (w_layer, idx * (ka // tk) + k, j)))
    in_specs.append(pl.BlockSpec((tm, tn), lambda i, j, k: (i, j)))
    in_specs.append(grp.mod_spec(layer, lambda i, j, k: chunk * ncol + j, tn, tm))
    args = [*a_list, *([w3] * n_a), x, mod]
    if out_gain is not None:
        assert nk > 1 and ncol == 1
        in_specs.append(pl.BlockSpec((1, n), lambda i, j, k: (0, 0)))
        args.append(out_gain.reshape(1, n))
    out_shape = [jax.ShapeDtypeStruct((m, n), F32)]
    out_specs = [pl.BlockSpec((tm, tn), lambda i, j, k: (i, j))]
    if next_norm is not None:
        assert nk > 1 and ncol == 1 and out_gain is None
        g_next, layer_next, chunk_sc, chunk_sh = next_norm
        in_specs += grp.norm_specs(layer_next, chunk_sc, chunk_sh, n, tm)
        args += [g_next, mod, mod]
        out_shape.append(jax.ShapeDtypeStruct((m, n), BF16))
        out_specs.append(pl.BlockSpec((tm, n), lambda i, j, k: (i, 0)))
    res = pl.pallas_call(
        functools.partial(_mm_res_kernel, n_a=n_a, nk=nk, nb=grp.mod_rows(tm),
                          out_norm=out_gain is not None, next_norm=next_norm is not None),
        out_shape=out_shape,
        grid=(m // tm, ncol, nk),
        in_specs=in_specs,
        out_specs=out_specs,
        compiler_params=_params("arbitrary", "arbitrary", "arbitrary"),
        name="mm_res",
    )(*args)
    return res if next_norm is not None else res[0]


def _ffn_up_kernel(x_ref, gn_ref, sc_ref, sh_ref, wg_ref, wv_ref, cg_ref, cv_ref, bg_ref, bv_ref,
                   o_ref, h_ref, *, seq):
    @pl.when(pl.program_id(1) == 0)
    def _():
        _norm_mod_into(h_ref, x_ref, gn_ref, sc_ref, sh_ref)

    h = h_ref[...]
    tm = h.shape[0]

    def taps(w_ref):
        u = jnp.dot(h, w_ref[...].astype(BF16), preferred_element_type=F32)
        return pltpu.roll(u, 1, 0), u, pltpu.roll(u, tm - 1, 0)

    def conv(t, c_ref, b_ref):
        return t[0] * c_ref[0:1, :] + t[1] * c_ref[1:2, :] + t[2] * c_ref[2:3, :] + b_ref[...]

    tg, tv = taps(wg_ref), taps(wv_ref)
    o_ref[...] = (_silu(conv(tg, cg_ref, bg_ref)) * conv(tv, cv_ref, bv_ref)).astype(o_ref.dtype)
    sub = lax.broadcasted_iota(jnp.int32, (FFN_EDGE_ROWS, 1), 0)
    for r0 in sorted({r for b in range(0, tm, seq) for r in (b, b + seq - FFN_EDGE_ROWS)}):
        rows = slice(r0, r0 + FFN_EDGE_ROWS)

        def edge(t):
            prev, cur, nxt = (a[rows, :] for a in t)
            if r0 % seq == 0:
                prev = jnp.where(sub == 0, 0.0, prev)
            if (r0 + FFN_EDGE_ROWS) % seq == 0:
                nxt = jnp.where(sub == FFN_EDGE_ROWS - 1, 0.0, nxt)
            return prev, cur, nxt

        o_ref[rows, :] = (_silu(conv(edge(tg), cg_ref, bg_ref))
                          * conv(edge(tv), cv_ref, bv_ref)).astype(o_ref.dtype)


def _ffn_up(x, norm, w_up, w_conv, b_conv, layer, grp, tm=1024, tn=512):
    m, d = x.shape
    d_ff = w_up.shape[2] // 2
    nj = d_ff // tn
    assert tm % grp.seq == 0 and m % tm == 0
    g_norm, mod, chunk_sc, chunk_sh = norm
    b3 = b_conv.reshape(b_conv.shape[0], 1, b_conv.shape[1])
    return pl.pallas_call(
        functools.partial(_ffn_up_kernel, seq=grp.seq),
        out_shape=jax.ShapeDtypeStruct((m, d_ff), BF16),
        grid=(m // tm, nj),
        in_specs=[
            pl.BlockSpec((tm, d), lambda i, j: (i, 0)),
            *grp.norm_specs(layer, chunk_sc, chunk_sh, d, tm),
            pl.BlockSpec((None, d, tn), lambda i, j: (layer, 0, j)),
            pl.BlockSpec((None, d, tn), lambda i, j: (layer, 0, nj + j)),
            pl.BlockSpec((None, w_conv.shape[1], tn), lambda i, j: (layer, 0, j)),
            pl.BlockSpec((None, w_conv.shape[1], tn), lambda i, j: (layer, 0, nj + j)),
            pl.BlockSpec((None, 1, tn), lambda i, j: (layer, 0, j)),
            pl.BlockSpec((None, 1, tn), lambda i, j: (layer, 0, nj + j)),
        ],
        out_specs=pl.BlockSpec((tm, tn), lambda i, j: (i, j)),
        scratch_shapes=[pltpu.VMEM((tm, d), BF16)],
        compiler_params=_params("arbitrary", "arbitrary"),
        name="ffn_up",
    )(x, g_norm, mod, mod, w_up, w_up, w_conv, w_conv, b3, b3)


def _rope_tables(n_tokens, rot_dim):
    n_rows = n_tokens // GRID_W
    row = jnp.repeat(jnp.arange(n_rows), GRID_W).astype(F32)
    col = jnp.tile(jnp.arange(GRID_W), n_rows).astype(F32)
    n_freq = rot_dim // 4
    freqs = ROPE_BASE ** (-jnp.arange(n_freq, dtype=F32) / n_freq)
    ang = jnp.concatenate([row[:, None] * freqs, col[:, None] * freqs], axis=-1)
    cos, sin = jnp.cos(ang), jnp.sin(ang)
    reps = LANES // rot_dim
    c = jnp.tile(jnp.concatenate([cos, cos], axis=-1), (1, reps))
    s = jnp.tile(jnp.concatenate([-sin, sin], axis=-1), (1, reps))
    return c, s


def _trunk(x, grp, mod, caches, P):
    ckv_p = krope_p = kb_p = vb_p = k_c = v_c = h_next = None
    depth = P["w_ffn_up"].shape[0]
    for l in range(depth):
        i = l // 2
        norm_mix = (P["g_norm_mix"], mod, l, 1, 0)
        if l % 2 == 0:
            (q_mla, kv, ckv, krope, krope_pad, qb, kb, vb) = _inproj_ab(
                x, norm_mix, [w[i] for w in P["w_ab"]],
                P["g_mla_q"][i:i + 1], P["g_mla_kv"][i:i + 1], P["g_gqa_q"][i:i + 1],
                P["g_gqa_k"][i:i + 1], grp, P["dims_ab"])
            ckv_p, krope_p, kb_p, vb_p = ckv, krope, kb, vb
            heads = P["dims_ab"][5]
            segs_a, segs_b = [], []
            if caches is not None:
                c_ckv, c_kr, c_k, c_v = (cc[i] for cc in caches[:4])
                past = c_ckv.shape[0] // grp.batch
                segs_a.append((_kvup(c_ckv, P["w_ab"][4][i]), c_kr, past))
                segs_b.append((c_k, c_v, past))
            segs_a.append((kv, krope_pad, grp.seq))
            segs_b.append((kb, vb, grp.seq))
            tq = min(grp.seq, 512)
            out_a = _mla_attention(q_mla, segs_a, grp, heads, tq, heads)
            out_b = _gqa_attention(qb, segs_b, grp, qb.shape[1] // HEAD_DIM, GQA_KV_HEADS, tq,
                                   GQA_KV_HEADS)
            x = _mm_res([out_a, out_b], P["w_out_ab"], i, x, mod, l, 2, grp, tk=None)
        else:
            w_in_c = P["w_in_c"][i:i + 1]
            nq = w_in_c.shape[2] // 3
            kv_dtype = F32 if caches is None else BF16
            h = h_next
            q = _proj(h, w_in_c, 0, nq, BF16, grp, True, scale=HEAD_DIM ** -0.5 * LOG2E)
            k = _proj(h, w_in_c, nq, nq, kv_dtype, grp, True)
            v = _proj(h, w_in_c, 2 * nq, nq, kv_dtype, grp, False)
            k_c, v_c = k, v
            segs = []
            if caches is not None:
                c_k, c_v = (cc[i] for cc in caches[4:])
                segs.append((c_k, c_v, c_k.shape[0] // grp.batch))
            segs.append((k, v, grp.seq))
            lambda_init = 0.8 - 0.6 * math.exp(-0.3 * l)
            lams = [P[n][i:i + 1] for n in ("lambda_q1", "lambda_k1", "lambda_q2", "lambda_k2")]
            dheads = nq // (2 * HEAD_DIM)
            o = _diff_attention(q, segs, lams, P["g_diff_sub"][i:i + 1], grp, dheads,
                                min(grp.seq, 1024), dheads if grp.seq <= 256 else 2, lambda_init)
            x = _mm_res([o], P["w_out_c"], i, x, mod, l, 2, grp, tk=o.shape[1])
        g = _ffn_up(x, (P["g_norm_ffn"], mod, 4, 3), P["w_ffn_up"], P["w_ffn_conv"],
                    P["b_ffn_conv"], l, grp)
        if l + 1 < depth:
            x, h_next = _mm_res([g], P["w_ffn_down"], l, x, mod, l, 5, grp, tk=512, tm=1024,
                                next_norm=(P["g_norm_mix"], l + 1, 1, 0))
        else:
            x = _mm_res([g], P["w_ffn_down"], l, x, mod, l, 5, grp, tk=512, tm=1024,
                        out_gain=P["g_final"])
    return x, (ckv_p, krope_p, kb_p, vb_p), (k_c, v_c)


def kernel(x_prompt, x_sample, cache_mla_ckv, cache_mla_krope, cache_gqa_k, cache_gqa_v,
           cache_diff_k, cache_diff_v, c, c_ctx, w_ada, b_ada, g_norm_mix, g_norm_ffn,
           w_in_ab, g_mla_q, w_mla_q_up, g_mla_kv, w_mla_kv_up, g_gqa_q, g_gqa_k, w_out_ab,
           w_in_c, lambda_q1, lambda_k1, lambda_q2, lambda_k2, g_diff_sub, w_out_c,
           w_ffn_up, w_ffn_conv, b_ffn_conv, w_ffn_down, g_final):
    bp, tp, d = x_prompt.shape
    bs, ts, _ = x_sample.shape
    depth = w_ada.shape[0]
    n_even = w_in_ab.shape[0]
    assert depth == 2 and n_even == 1 and w_in_c.shape[0] == 1, "one MLA/GQA layer + one diff layer"
    past = cache_mla_ckv.shape[2]
    n_qlat = g_mla_q.shape[1]
    n_ckv = g_mla_kv.shape[1]
    mla_heads = w_mla_q_up.shape[2] // (MLA_NOPE + MLA_ROPE)
    n_kb = cache_gqa_k.shape[3] * HEAD_DIM
    n_vb = n_kb
    n_qb = w_in_ab.shape[2] - n_qlat - n_ckv - MLA_ROPE - n_kb - n_vb

    rows = 8
    cond = jnp.zeros((rows, d), F32).at[:bs].set(c).at[bs].set(c_ctx)
    mod = _modulation(cond, w_ada, b_ada)
    mod = mod.reshape(depth, rows, 1, mod.shape[2])

    o0 = n_qlat + n_ckv
    o1 = o0 + MLA_ROPE
    w_a = w_in_ab[:, :, :o0].astype(BF16)
    w_b = w_in_ab[:, :, o1:].astype(BF16)
    w_k = jnp.pad(w_in_ab[:, :, o0:o1], ((0, 0), (0, 0), (0, LANES - MLA_ROPE))).astype(BF16)
    wq = w_mla_q_up.reshape(n_even, n_qlat, mla_heads, MLA_NOPE + MLA_ROPE)
    wq = wq * ((MLA_NOPE + MLA_ROPE) ** -0.5 * LOG2E)
    w_q_up_p = jnp.pad(wq, ((0, 0), (0, 0), (0, 0), (0, LANES - MLA_ROPE))).reshape(
        n_even, n_qlat, mla_heads * (MLA_NOPE + LANES)).astype(BF16)
    P = dict(
        w_ab=(w_a, w_b, w_k, w_q_up_p, w_mla_kv_up.astype(BF16)),
        g_mla_q=g_mla_q, g_mla_kv=g_mla_kv, g_gqa_q=g_gqa_q * (HEAD_DIM ** -0.5 * LOG2E),
        g_gqa_k=g_gqa_k,
        w_out_ab=w_out_ab, w_in_c=w_in_c, lambda_q1=lambda_q1, lambda_k1=lambda_k1,
        lambda_q2=lambda_q2, lambda_k2=lambda_k2, g_diff_sub=g_diff_sub, w_out_c=w_out_c,
        w_ffn_up=w_ffn_up, w_ffn_conv=w_ffn_conv, b_ffn_conv=b_ffn_conv, w_ffn_down=w_ffn_down,
        g_final=g_final,
        g_norm_mix=g_norm_mix.reshape(depth, 1, d), g_norm_ffn=g_norm_ffn.reshape(depth, 1, d),
        dims_ab=(n_qlat, n_ckv, n_qb, n_kb, n_vb, mla_heads),
    )

    grp_p = _Group(bp, tp, bs, False, None)
    y_p, ab_p, c_p = _trunk(x_prompt.reshape(bp * tp, d), grp_p, mod, None, P)

    c128, s128 = _rope_tables(ts, HEAD_DIM)
    c64, s64 = _rope_tables(ts, MLA_ROPE)
    grp_s = _Group(bs, ts, 0, True, dict(c128=c128, s128=s128, c64=c64, s64=s64))
    n_odd = cache_diff_k.shape[1]
    kr_pad = jnp.pad(cache_mla_krope, ((0, 0), (0, 0), (0, 0), (0, LANES - MLA_ROPE))).astype(BF16)
    per_layer = lambda a, n: [a[:, j].reshape(bs * past, -1) for j in range(n)]
    caches = (per_layer(cache_mla_ckv, n_even), per_layer(kr_pad, n_even),
              per_layer(cache_gqa_k, n_even), per_layer(cache_gqa_v, n_even),
              per_layer(cache_diff_k.astype(BF16), n_odd),
              per_layer(cache_diff_v.astype(BF16), n_odd))
    y_s, _, _ = _trunk(x_sample.reshape(bs * ts, d), grp_s, mod, caches, P)

    ckv, krope, kb, vb = ab_p
    k_c, v_c = c_p
    kvh = n_kb // HEAD_DIM
    dh = k_c.shape[1] // (2 * HEAD_DIM)
    return (y_p.reshape(bp, tp, d), y_s.reshape(bs, ts, d),
            ckv.reshape(bp, 1, tp, n_ckv), krope.reshape(bp, 1, tp, MLA_ROPE),
            kb.reshape(bp, 1, tp, kvh, HEAD_DIM), vb.reshape(bp, 1, tp, kvh, HEAD_DIM),
            k_c.reshape(bp, 1, tp, dh, 2, HEAD_DIM), v_c.reshape(bp, 1, tp, dh, 2 * HEAD_DIM))
```
